```python
import math
import jax, jax.numpy as jnp
from jax import lax
import numpy as np

D_MODEL = 1024
BATCH = 32
SEQ = 2048
DEPTH = 4

CHUNK = 64
MEM_LEN = 256
Q_BLOCK = 128
CONV_WIDTH = 4
ROPE_THETA = 10000.0
NORM_EPS = 1e-6
LN_EPS = 1e-5

GDN_HEADS = 4
GDN_DK = 128
GDN_DV = 128
GDN_CONV_CH = 2 * GDN_HEADS * GDN_DK + GDN_HEADS * GDN_DV

DIFF_HEADS = 4
DIFF_DQK = 64
DIFF_DV = 2 * DIFF_DQK

SSM_HEADS = 8
SSM_HEADDIM = 64
SSM_GROUPS = 2
SSM_STATE = 128
SSM_INNER = SSM_HEADS * SSM_HEADDIM
SSM_CONV_CH = SSM_INNER + 2 * SSM_GROUPS * SSM_STATE

MEM_HEADS = 4
MEM_HEADDIM = 128

BRANCH_WIDTH = 512
N_BRANCH = 4

N_EXPERTS = 32
TOP_K = 4
D_FF = D_MODEL
SWIGLU_LIMIT = 7.0
SWIGLU_ALPHA = 1.702

DEEPNORM_ALPHA = (2.0 * DEPTH) ** 0.25
DEEPNORM_BETA = (8.0 * DEPTH) ** -0.25

IN_SPLITS = (GDN_HEADS * GDN_DK, GDN_HEADS * GDN_DK, GDN_HEADS * GDN_DV, GDN_HEADS * GDN_DV, GDN_HEADS, GDN_HEADS,
             DIFF_HEADS * 2 * DIFF_DQK, DIFF_HEADS * 2 * DIFF_DQK, DIFF_HEADS * DIFF_DV,
             SSM_INNER, SSM_CONV_CH, SSM_HEADS,
             MEM_HEADS * MEM_HEADDIM,
             N_BRANCH * D_MODEL)
D_IN = sum(IN_SPLITS)

kernel_name = 'hybrid_gdn_diff_ssd_mem_moe_deepnorm'


def _split_cols(h):
    points = [int(c) for c in np.cumsum(IN_SPLITS)[:-1]]
    return jnp.split(h, points, axis=-1)


def _rms_norm(x, w):
    xf = x.astype(jnp.float32)
    y = xf * lax.rsqrt(jnp.mean(xf * xf, axis=-1, keepdims=True) + NORM_EPS)
    return (y * w).astype(x.dtype)


def _layer_norm(x, g, b):
    xf = x.astype(jnp.float32)
    mu = jnp.mean(xf, axis=-1, keepdims=True)
    xc = xf - mu
    var = jnp.mean(xc * xc, axis=-1, keepdims=True)
    return (xc * lax.rsqrt(var + LN_EPS) * g + b).astype(x.dtype)


def _l2norm(x):
    xf = x.astype(jnp.float32)
    return xf * lax.rsqrt(jnp.sum(xf * xf, axis=-1, keepdims=True) + NORM_EPS)


def _causal_dwconv(x, w):
    width, ch = w.shape
    return lax.conv_general_dilated(x, w[:, None, :], window_strides=(1,), padding=((width - 1, 0),),
                                    dimension_numbers=('NWC', 'WIO', 'NWC'), feature_group_count=ch)


def _rope(x, pos):
    half = x.shape[-1] // 2
    inv_freq = ROPE_THETA ** (-jnp.arange(half, dtype=jnp.float32) / half)
    ang = pos.astype(jnp.float32)[:, None] * inv_freq[None, :]
    ang = ang.reshape((ang.shape[0],) + (1,) * (x.ndim - 3) + (half,))
    cos = jnp.cos(ang).astype(x.dtype)
    sin = jnp.sin(ang).astype(x.dtype)
    x1, x2 = x[..., :half], x[..., half:]
    return jnp.concatenate([x1 * cos - x2 * sin, x2 * cos + x1 * sin], axis=-1)


def _gdn_chunked(q, k, v, g, beta):
    bsz, seq, nh, dk = q.shape
    dv = v.shape[-1]
    n = seq // CHUNK
    f32 = jnp.float32

    def chunks(t):
        t = t.astype(f32).reshape((bsz, n, CHUNK, nh) + t.shape[3:])
        return jnp.moveaxis(jnp.moveaxis(t, 1, 0), 2, 3)

    qc, kc, vc, bc = chunks(q), chunks(k), chunks(v), chunks(beta)
    gc = jnp.cumsum(chunks(g), axis=-1)
    idx = jnp.arange(CHUNK)
    tri = idx[:, None] >= idx[None, :]
    decay = jnp.exp(jnp.where(tri, gc[..., :, None] - gc[..., None, :], -jnp.inf))
    kb = kc * bc[..., None]
    a_strict = jnp.where(idx[:, None] > idx[None, :],
                         jnp.einsum('nbhik,nbhjk->nbhij', kb, kc) * decay, 0.0)
    rhs = jnp.concatenate([vc * bc[..., None], kb * jnp.exp(gc)[..., None]], axis=-1)
    sol = lax.linalg.triangular_solve(jnp.eye(CHUNK, dtype=f32) + a_strict, rhs,
                                      left_side=True, lower=True, unit_diagonal=True)
    u, w = sol[..., :dv], sol[..., dv:]
    qk = jnp.einsum('nbhik,nbhjk->nbhij', qc, kc) * decay
    q_dec = qc * jnp.exp(gc)[..., None]
    k_dec = kc * jnp.exp(gc[..., -1:] - gc)[..., None]
    g_last = jnp.exp(gc[..., -1])

    def step(state, inp):
        q_d, k_d, u_n, w_n, qk_n, gl = inp
        v_new = u_n - jnp.einsum('bhik,bhkv->bhiv', w_n, state)
        o = jnp.einsum('bhik,bhkv->bhiv', q_d, state) + jnp.einsum('bhij,bhjv->bhiv', qk_n, v_new)
        state = state * gl[..., None, None] + jnp.einsum('bhik,bhiv->bhkv', k_d, v_new)
        return state, o

    s0 = jnp.zeros((bsz, nh, dk, dv), f32)
    _, o = lax.scan(step, s0, (q_dec, k_dec, u, w, qk, g_last))
    o = jnp.moveaxis(jnp.moveaxis(o, 3, 2), 0, 1)
    return o.reshape(bsz, seq, nh, dv).astype(q.dtype)


def _gated_deltanet(q, k, v, z, b, a, conv_w, a_log, dt_bias, norm_w):
    bsz, seq, _ = q.shape
    qkv = jax.nn.silu(_causal_dwconv(jnp.concatenate([q, k, v], axis=-1), conv_w))
    q, k, v = jnp.split(qkv, [GDN_HEADS * GDN_DK, 2 * GDN_HEADS * GDN_DK], axis=-1)
    q = _l2norm(q.reshape(bsz, seq, GDN_HEADS, GDN_DK)) * GDN_DK ** -0.5
    k = _l2norm(k.reshape(bsz, seq, GDN_HEADS, GDN_DK))
    v = v.reshape(bsz, seq, GDN_HEADS, GDN_DV)
    beta = jax.nn.sigmoid(b.astype(jnp.float32))
    g = -jnp.exp(a_log.astype(jnp.float32)) * jax.nn.softplus((a + dt_bias).astype(jnp.float32))
    o = _gdn_chunked(q, k, v, g, beta).astype(v.dtype)
    o = _rms_norm(o, norm_w) * jax.nn.silu(z.reshape(bsz, seq, GDN_HEADS, GDN_DV))
    return o.reshape(bsz, seq, GDN_HEADS * GDN_DV)


def _diff_attention(q, k, v, lam_params, norm_w, lambda_init):
    bsz, seq, _ = q.shape
    pos = jnp.arange(seq)
    q = _rope(q.reshape(bsz, seq, DIFF_HEADS, 2, DIFF_DQK), pos) * DIFF_DQK ** -0.5
    k = _rope(k.reshape(bsz, seq, DIFF_HEADS, 2, DIFF_DQK), pos)
    v = v.reshape(bsz, seq, DIFF_HEADS, DIFF_DV)
    lp = lam_params.astype(jnp.float32)
    lam = jnp.exp(jnp.sum(lp[0] * lp[1])) - jnp.exp(jnp.sum(lp[2] * lp[3])) + lambda_init
    chunk_id = pos // CHUNK
    outs = []
    for blk in range(seq // Q_BLOCK):
        q0 = blk * Q_BLOCK
        kend = q0 + Q_BLOCK
        s = jnp.einsum('bqhmd,bkhmd->bhmqk', q[:, q0:kend], k[:, :kend]).astype(jnp.float32)
        mask = chunk_id[q0:kend, None] >= chunk_id[None, :kend]
        p = jax.nn.softmax(jnp.where(mask, s, -jnp.inf), axis=-1)
        p = p[:, :, 0] - lam * p[:, :, 1]
        outs.append(jnp.einsum('bhqk,bkhd->bqhd', p.astype(v.dtype), v[:, :kend]))
    o = jnp.concatenate(outs, axis=1)
    o = _rms_norm(o, norm_w) * (1.0 - lambda_init)
    return o.reshape(bsz, seq, DIFF_HEADS * DIFF_DV)


def _ssd_chunked(x, a, bm, cm):
    bsz, seq, nh, hp = x.shape
    ng, ns = bm.shape[2], bm.shape[3]
    r = nh // ng
    n = seq // CHUNK
    f32 = jnp.float32
    xc = x.astype(f32).reshape(bsz, n, CHUNK, ng, r, hp).transpose(1, 0, 2, 3, 4, 5)
    bc = bm.astype(f32).reshape(bsz, n, CHUNK, ng, ns).transpose(1, 0, 2, 3, 4)
    cc = cm.astype(f32).reshape(bsz, n, CHUNK, ng, ns).transpose(1, 0, 2, 3, 4)
    ac = a.astype(f32).reshape(bsz, n, CHUNK, ng, r).transpose(1, 0, 3, 4, 2)
    acum = jnp.cumsum(ac, axis=-1)
    idx = jnp.arange(CHUNK)
    tri = idx[:, None] >= idx[None, :]
    seg = jnp.exp(jnp.where(tri, acum[..., :, None] - acum[..., None, :], -jnp.inf))
    cb = jnp.einsum('nbigd,nbjgd->nbgij', cc, bc)
    y_diag = jnp.einsum('nbgrij,nbjgrp->nbigrp', seg * cb[:, :, :, None], xc)
    decay_in = jnp.exp(acum[..., -1:] - acum)
    states = jnp.einsum('nbjgd,nbgrj,nbjgrp->nbgrpd', bc, decay_in, xc)
    chunk_decay = jnp.exp(acum[..., -1])

    def step(h, inp):
        st, cd = inp
        return h * cd[..., None, None] + st, h

    h0 = jnp.zeros((bsz, ng, r, hp, ns), f32)
    _, h_in = lax.scan(step, h0, (states, chunk_decay))
    y_off = jnp.einsum('nbigd,nbgrpd,nbgri->nbigrp', cc, h_in, jnp.exp(acum))
    y = (y_diag + y_off).transpose(1, 0, 2, 3, 4, 5).reshape(bsz, seq, nh, hp)
    return y.astype(x.dtype)


def _mamba2(z, xbc, dt, conv_w, conv_b, a_log, dt_bias, d_skip, norm_w):
    bsz, seq, _ = z.shape
    xbc = jax.nn.silu(_causal_dwconv(xbc, conv_w) + conv_b)
    xs, bm, cm = jnp.split(xbc, [SSM_INNER, SSM_INNER + SSM_GROUPS * SSM_STATE], axis=-1)
    xs = xs.reshape(bsz, seq, SSM_HEADS, SSM_HEADDIM)
    bm = bm.reshape(bsz, seq, SSM_GROUPS, SSM_STATE)
    cm = cm.reshape(bsz, seq, SSM_GROUPS, SSM_STATE)
    dt = jax.nn.softplus((dt + dt_bias).astype(jnp.float32))
    a = -jnp.exp(a_log.astype(jnp.float32)) * dt
    y = _ssd_chunked(xs * dt[..., None].astype(xs.dtype), a, bm, cm) + d_skip[:, None] * xs
    gated = (y.reshape(bsz, seq, SSM_INNER) * jax.nn.silu(z)).reshape(bsz, seq, SSM_GROUPS, SSM_INNER // SSM_GROUPS)
    y = _rms_norm(gated, norm_w.reshape(SSM_GROUPS, SSM_INNER // SSM_GROUPS))
    return y.reshape(bsz, seq, SSM_INNER)


def _memory_attention(q, mem, w_mem):
    bsz, seq, _ = q.shape
    m = mem.shape[1]
    k, v = jnp.split(mem @ w_mem, 2, axis=-1)
    q = q.reshape(bsz, seq, MEM_HEADS, MEM_HEADDIM) * MEM_HEADDIM ** -0.5
    k = k.reshape(bsz, m, MEM_HEADS, MEM_HEADDIM)
    v = v.reshape(bsz, m, MEM_HEADS, MEM_HEADDIM)
    p = jax.nn.softmax(jnp.einsum('bshd,bmhd->bhsm', q, k).astype(jnp.float32), axis=-1)
    o = jnp.einsum('bhsm,bmhd->bshd', p.astype(v.dtype), v)
    return o.reshape(bsz, seq, MEM_HEADS * MEM_HEADDIM)


def _hybrid_mixer(x, mem, w_in, gdn_conv_w, gdn_a_log, gdn_dt_bias, gdn_norm_w, diff_lambda, diff_norm_w,
                  ssm_conv_w, ssm_conv_b, ssm_a_log, ssm_dt_bias, ssm_d, ssm_norm_w, w_mem, w_branch, w_out,
                  lambda_init):
    bsz, seq, _ = x.shape
    (gq, gk, gv, gz, gb, ga, dq, dk, dv, sz, sxbc, sdt, mq, gate_logits) = _split_cols(x @ w_in)
    y_gdn = _gated_deltanet(gq, gk, gv, gz, gb, ga, gdn_conv_w, gdn_a_log, gdn_dt_bias, gdn_norm_w)
    y_diff = _diff_attention(dq, dk, dv, diff_lambda, diff_norm_w, lambda_init)
    y_ssm = _mamba2(sz, sxbc, sdt, ssm_conv_w, ssm_conv_b, ssm_a_log, ssm_dt_bias, ssm_d, ssm_norm_w)
    y_mem = _memory_attention(mq, mem, w_mem)
    gates = jax.nn.sigmoid(gate_logits.reshape(bsz, seq, N_BRANCH, D_MODEL))
    merged = sum(gates[:, :, i] * (y_i @ w_branch[i]) for i, y_i in enumerate((y_gdn, y_diff, y_ssm, y_mem)))
    return merged @ w_out


def _moe(x, router_w, router_b, w_gate_up, b_gate_up, w_down, b_down):
    bsz, seq, d = x.shape
    xt = x.reshape(bsz * seq, d)
    logits = (xt @ router_w + router_b).astype(jnp.float32)
    top_val, top_idx = lax.top_k(logits, TOP_K)
    top_w = jax.nn.softmax(top_val, axis=-1)
    gate = jnp.einsum('tk,tke->te', top_w, jax.nn.one_hot(top_idx, N_EXPERTS, dtype=jnp.float32)).astype(x.dtype)
    y = jnp.zeros_like(xt)
    for e in range(N_EXPERTS):
        gu = xt @ w_gate_up[e] + b_gate_up[e]
        glu, lin = gu[:, 0::2], gu[:, 1::2]
        glu = jnp.minimum(glu, SWIGLU_LIMIT)
        lin = jnp.clip(lin, -SWIGLU_LIMIT, SWIGLU_LIMIT)
        act = (lin + 1.0) * glu * jax.nn.sigmoid(SWIGLU_ALPHA * glu)
        y = y + gate[:, e:e + 1] * (act @ w_down[e] + b_down[e])
    return y.reshape(bsz, seq, d)


def setup_inputs(seed: int = 0) -> dict:
    key = jax.random.key(seed)
    ks = jax.random.split(key, 28)
    f32 = jnp.float32
    L = DEPTH

    def nrm(k, shape, scale):
        return scale * jax.random.normal(k, shape, f32)

    def gain(k, shape):
        return 1.0 + 0.02 * jax.random.normal(k, shape, f32)

    def dt_bias(k, shape):
        dt = jnp.exp(jax.random.uniform(k, shape, f32, math.log(1e-3), math.log(1e-1)))
        return dt + jnp.log(-jnp.expm1(-dt))

    def a_log(k, shape):
        return jnp.log(jax.random.uniform(k, shape, f32, 1.0, 16.0))

    return {
        'x': nrm(ks[0], (BATCH, SEQ, D_MODEL), 1.0),
        'mem': nrm(ks[1], (BATCH, MEM_LEN, D_MODEL), 1.0),
        'w_in': nrm(ks[2], (L, D_MODEL, D_IN), D_MODEL ** -0.5),
        'gdn_conv_w': nrm(ks[3], (L, CONV_WIDTH, GDN_CONV_CH), CONV_WIDTH ** -0.5),
        'gdn_a_log': a_log(ks[4], (L, GDN_HEADS)),
        'gdn_dt_bias': dt_bias(ks[5], (L, GDN_HEADS)),
        'gdn_norm_w': gain(ks[6], (L, GDN_DV)),
        'diff_lambda': nrm(ks[7], (L, 4, DIFF_DQK), 0.1),
        'diff_norm_w': gain(ks[8], (L, DIFF_DV)),
        'ssm_conv_w': nrm(ks[9], (L, CONV_WIDTH, SSM_CONV_CH), CONV_WIDTH ** -0.5),
        'ssm_conv_b': nrm(ks[10], (L, SSM_CONV_CH), 0.02),
        'ssm_a_log': a_log(ks[11], (L, SSM_HEADS)),
        'ssm_dt_bias': dt_bias(ks[12], (L, SSM_HEADS)),
        'ssm_d': gain(ks[13], (L, SSM_HEADS)),
        'ssm_norm_w': gain(ks[14], (L, SSM_INNER)),
        'w_mem': nrm(ks[15], (L, D_MODEL, 2 * MEM_HEADS * MEM_HEADDIM), D_MODEL ** -0.5),
        'w_branch': nrm(ks[16], (L, N_BRANCH, BRANCH_WIDTH, D_MODEL), BRANCH_WIDTH ** -0.5),
        'w_out': nrm(ks[17], (L, D_MODEL, D_MODEL), DEEPNORM_BETA * D_MODEL ** -0.5),
        'ln1_g': gain(ks[18], (L, D_MODEL)),
        'ln1_b': nrm(ks[19], (L, D_MODEL), 0.02),
        'router_w': nrm(ks[20], (L, D_MODEL, N_EXPERTS), D_MODEL ** -0.5),
        'router_b': nrm(ks[21], (L, N_EXPERTS), 0.01),
        'w_gate_up': nrm(ks[22], (L, N_EXPERTS, D_MODEL, 2 * D_FF), D_MODEL ** -0.5),
        'b_gate_up': nrm(ks[23], (L, N_EXPERTS, 2 * D_FF), 0.01),
        'w_down': nrm(ks[24], (L, N_EXPERTS, D_FF, D_MODEL), DEEPNORM_BETA * D_FF ** -0.5),
        'b_down': nrm(ks[25], (L, N_EXPERTS, D_MODEL), 0.01),
        'ln2_g': gain(ks[26], (L, D_MODEL)),
        'ln2_b': nrm(ks[27], (L, D_MODEL), 0.02),
    }


def reference(x, mem, w_in, gdn_conv_w, gdn_a_log, gdn_dt_bias, gdn_norm_w, diff_lambda, diff_norm_w,
              ssm_conv_w, ssm_conv_b, ssm_a_log, ssm_dt_bias, ssm_d, ssm_norm_w, w_mem, w_branch, w_out,
              ln1_g, ln1_b, router_w, router_b, w_gate_up, b_gate_up, w_down, b_down, ln2_g, ln2_b):
    for l in range(DEPTH):
        lambda_init = 0.8 - 0.6 * math.exp(-0.3 * l)
        mix = _hybrid_mixer(x, mem, w_in[l], gdn_conv_w[l], gdn_a_log[l], gdn_dt_bias[l], gdn_norm_w[l],
                            diff_lambda[l], diff_norm_w[l], ssm_conv_w[l], ssm_conv_b[l], ssm_a_log[l],
                            ssm_dt_bias[l], ssm_d[l], ssm_norm_w[l], w_mem[l], w_branch[l], w_out[l], lambda_init)
        x = _layer_norm(DEEPNORM_ALPHA * x + mix, ln1_g[l], ln1_b[l])
        ffn = _moe(x, router_w[l], router_b[l], w_gate_up[l], b_gate_up[l], w_down[l], b_down[l])
        x = _layer_norm(DEEPNORM_ALPHA * x + ffn, ln2_g[l], ln2_b[l])
    return x
```

```python
import functools
import math

import jax
import jax.numpy as jnp
from jax import lax
from jax.experimental import pallas as pl
from jax.experimental.pallas import tpu as pltpu
from jax.experimental.pallas import tpu_sc as plsc

F32 = jnp.float32
BF16 = jnp.bfloat16
U32 = jnp.uint32

D_MODEL = 1024
DEPTH = 4
CHUNK = 64
Q_BLOCK = 128
CONV_WIDTH = 4
ROPE_THETA = 10000.0
NORM_EPS = 1e-6
LN_EPS = 1e-5

GDN_HEADS = 4
GDN_DK = 128
GDN_DV = 128
DIFF_HEADS = 4
DIFF_DQK = 64
DIFF_DV = 128
SSM_HEADS = 8
SSM_HEADDIM = 64
SSM_GROUPS = 2
SSM_STATE = 128
SSM_INNER = 512
MEM_HEADS = 4
MEM_HEADDIM = 128
N_BRANCH = 4
N_EXPERTS = 32
TOP_K = 4
D_FF = 1024
SWIGLU_LIMIT = 7.0
SWIGLU_ALPHA = 1.702
DEEPNORM_ALPHA = (2.0 * DEPTH) ** 0.25

LANES = 128
SUBLANES = 8
VMEM_LIMIT = 56 * 1024 * 1024

HB = 512
COL_GATES = 0
COL_GQ, COL_GK, COL_GV, COL_GZ = 8, 9, 10, 11
COL_DQ, COL_DK, COL_DV = 12, 13, 14
COL_SZ = 15
COL_SXBC = 8
COL_MQ = 18
H_WIDTH = 19 * HB

SEQ_TILE = 512
ROW_TILE = 512
EXPERT_TILE = 512
SC_WINDOW = 128
SC_WORDS = 256


def _cparams(*sem):
    return pltpu.CompilerParams(dimension_semantics=sem, vmem_limit_bytes=VMEM_LIMIT)


def _dot(a, b):
    return jnp.dot(a.astype(BF16), b.astype(BF16), preferred_element_type=F32)


def _dot_nt(a, b):
    return lax.dot_general(a.astype(BF16), b.astype(BF16), (((1,), (1,)), ((), ())), preferred_element_type=F32)


def _bdot(a, b):
    return jnp.einsum('cmk,ckn->cmn', a.astype(BF16), b.astype(BF16), preferred_element_type=F32)


def _bdot_nt(a, b):
    return jnp.einsum('cmk,cnk->cmn', a.astype(BF16), b.astype(BF16), preferred_element_type=F32)


def _split2(a):
    hi = a.astype(BF16)
    lo = (a - hi.astype(F32)).astype(BF16)
    return hi, lo


def _bdot_hp(a, b):
    ah, al = _split2(a)
    bh, bl = _split2(b)
    return _bdot(ah, bh) + (_bdot(ah, bl) + _bdot(al, bh))


def _silu(x):
    return x * jax.nn.sigmoid(x)


def _softplus(x):
    return jnp.maximum(x, 0.0) + jnp.log1p(jnp.exp(-jnp.abs(x)))


def _rms(x, w):
    return x * lax.rsqrt(jnp.mean(x * x, axis=-1, keepdims=True) + NORM_EPS) * w


def _layer_norm(x, g, b):
    mu = jnp.mean(x, axis=-1, keepdims=True)
    xc = x - mu
    var = jnp.mean(xc * xc, axis=-1, keepdims=True)
    return xc * lax.rsqrt(var + LN_EPS) * g + b


def _mm_body(x_ref, w_ref, o_ref):
    o_ref[...] = jnp.dot(x_ref[...], w_ref[...], preferred_element_type=F32).astype(o_ref.dtype)


def _matmul(x, w, out_dtype, tm, tn, name):
    m, k = x.shape
    n = w.shape[1]
    return pl.pallas_call(
        _mm_body,
        grid=(m // tm, n // tn),
        in_specs=[pl.BlockSpec((tm, k), lambda i, j: (i, 0)), pl.BlockSpec((k, tn), lambda i, j: (0, j))],
        out_specs=pl.BlockSpec((tm, tn), lambda i, j: (i, j)),
        out_shape=jax.ShapeDtypeStruct((m, n), out_dtype),
        compiler_params=_cparams("parallel", "parallel"),
        name=name,
    )(x, w)


def _conv_silu(x, tail_ref, slot, xbuf_ref, w, bias):
    rows = x.shape[0]
    xbuf_ref[0:SUBLANES, :] = tail_ref[slot]
    xbuf_ref[SUBLANES:SUBLANES + rows, :] = x
    tail_ref[slot] = x[rows - SUBLANES:rows, :]
    y = x * w[CONV_WIDTH - 1:CONV_WIDTH, :]
    for j in range(CONV_WIDTH - 1):
        off = SUBLANES - (CONV_WIDTH - 1) + j
        y = y + xbuf_ref[off:off + rows, :] * w[j:j + 1, :]
    if bias is not None:
        y = y + bias
    return _silu(y)


def _chunk_iota(nc):
    r = lax.broadcasted_iota(jnp.int32, (nc, CHUNK, CHUNK), 1)
    c = lax.broadcasted_iota(jnp.int32, (nc, CHUNK, CHUNK), 2)
    return r, c


def _segment_logdecay(g3, r, c):
    lincl = jnp.where(r >= c, 1.0, 0.0).astype(BF16)
    ghi = g3.astype(BF16)
    glo = (g3 - ghi.astype(F32)).astype(BF16)
    upper = r > c
    m_hi = _bdot(lincl, jnp.where(upper, ghi.astype(F32), 0.0))
    m_lo = _bdot(lincl, jnp.where(upper, glo.astype(F32), 0.0))
    return m_hi + m_lo


def _inv_unit_lower(a, r, c):
    d = jnp.where(r == c, 1.0, 0.0) - jnp.where(((r >> 1) == (c >> 1)) & (r > c), a, 0.0)
    for lb in range(1, 6):
        lower_left = ((r >> (lb + 1)) == (c >> (lb + 1))) & ((r >> lb) > (c >> lb))
        am = jnp.where(lower_left, a, 0.0)
        d = d - _bdot_hp(_bdot_hp(d, am), d)
    return d


def _gdn_body(q_ref, k_ref, v_ref, z_ref, hs_ref, cw_ref, par_ref, nw_ref, o_ref,
              tail_ref, xbuf_ref, qs_ref, ks_ref, vs_ref, u_ref, w_ref, qd_ref, kd_ref, qk_ref, gl_ref,
              state_ref):
    ts = q_ref.shape[0]
    nc = ts // CHUNK

    @pl.when(pl.program_id(1) == 0)
    def _():
        tail_ref[...] = jnp.zeros_like(tail_ref)
        state_ref[...] = jnp.zeros_like(state_ref)

    for slot, (src, dst) in enumerate(((q_ref, qs_ref), (k_ref, ks_ref), (v_ref, vs_ref))):
        w = cw_ref[:, slot * HB:(slot + 1) * HB]
        dst[...] = _conv_silu(src[...].astype(F32), tail_ref, slot, xbuf_ref, w, None)

    hs = hs_ref[...]
    beta_all = jax.nn.sigmoid(hs)
    g_all = -jnp.exp(par_ref[0:1, :]) * _softplus(hs + par_ref[1:2, :])
    r, c = _chunk_iota(nc)
    tril = r >= c
    for h in range(GDN_HEADS):
        sl = slice(h * GDN_DK, (h + 1) * GDN_DK)
        qh = qs_ref[:, sl]
        kh = ks_ref[:, sl]
        qn = qh * lax.rsqrt(jnp.sum(qh * qh, axis=-1, keepdims=True) + NORM_EPS) * (GDN_DK ** -0.5)
        kn = kh * lax.rsqrt(jnp.sum(kh * kh, axis=-1, keepdims=True) + NORM_EPS)
        q3 = qn.reshape(nc, CHUNK, GDN_DK)
        k3 = kn.reshape(nc, CHUNK, GDN_DK)
        v3 = vs_ref[:, sl].reshape(nc, CHUNK, GDN_DV)
        b3 = beta_all[:, h:h + 1].reshape(nc, CHUNK, 1)
        g3 = g_all[:, GDN_HEADS + h:GDN_HEADS + h + 1].reshape(nc, CHUNK, 1)
        m = _segment_logdecay(g3, r, c)
        decay = jnp.where(tril, jnp.exp(m), 0.0)
        kk = _bdot_nt(k3, k3)
        a = jnp.where(r > c, kk * b3 * decay, 0.0)
        t = _inv_unit_lower(a, r, c)
        gc = m[:, :, 0:1] + g3[:, 0:1, :]
        eg = jnp.exp(gc)
        g_last = gc[:, CHUNK - 1:CHUNK, :]
        rhs = jnp.concatenate([v3 * b3, k3 * (b3 * eg)], axis=-1)
        sol = _bdot_hp(t, rhs)
        u_ref[h] = sol[:, :, :GDN_DV].reshape(ts, GDN_DV)
        w_ref[h] = sol[:, :, GDN_DV:].reshape(ts, GDN_DK)
        qd_ref[h] = (q3 * eg).reshape(ts, GDN_DK)
        kd_ref[h] = (k3 * jnp.exp(g_last - gc)).reshape(ts, GDN_DK)
        qk_ref[h] = (_bdot_nt(q3, k3) * decay).reshape(ts, CHUNK)
        gl_ref[h] = jnp.broadcast_to(jnp.exp(g_last), (nc, SUBLANES, LANES))

    nw = nw_ref[...]

    def chunk_step(ci, carry):
        r0 = pl.multiple_of(ci * CHUNK, CHUNK)
        rows = pl.ds(r0, CHUNK)
        for h in range(GDN_HEADS):
            sl = slice(h * GDN_DV, (h + 1) * GDN_DV)
            s = state_ref[h]
            sb = s.astype(BF16)
            v_new = u_ref[h, rows, :] - _dot(w_ref[h, rows, :], sb)
            vb = v_new.astype(BF16)
            o = _dot(qd_ref[h, rows, :], sb) + _dot(qk_ref[h, rows, :], vb)
            kd_t = kd_ref[h, rows, :].T
            state_ref[h] = s * gl_ref[h, ci][0:1, :] + _dot(kd_t, vb)
            zz = z_ref[rows, sl].astype(F32)
            o_ref[rows, sl] = (_rms(o, nw) * _silu(zz)).astype(o_ref.dtype)
        return carry

    lax.fori_loop(0, nc, chunk_step, 0)


def _gdn(h, hs, conv_w, par, norm_w, bsz, seq):
    ts = min(SEQ_TILE, seq)
    ns = seq // ts
    tok = lambda col: pl.BlockSpec((ts, HB), lambda b, s: (b * ns + s, col))
    full = lambda shape: pl.BlockSpec(shape, lambda b, s: (0,) * len(shape))
    nc = ts // CHUNK
    return pl.pallas_call(
        _gdn_body,
        grid=(bsz, ns),
        in_specs=[tok(COL_GQ), tok(COL_GK), tok(COL_GV), tok(COL_GZ),
                  pl.BlockSpec((ts, LANES), lambda b, s: (b * ns + s, 0)),
                  full((CONV_WIDTH, 3 * HB)), full((SUBLANES, LANES)), full((1, GDN_DV))],
        out_specs=pl.BlockSpec((ts, HB), lambda b, s: (b * ns + s, 0)),
        out_shape=jax.ShapeDtypeStruct((bsz * seq, HB), BF16),
        scratch_shapes=[
            pltpu.VMEM((3, SUBLANES, HB), F32),
            pltpu.VMEM((ts + SUBLANES, HB), F32),
            pltpu.VMEM((ts, HB), F32), pltpu.VMEM((ts, HB), F32), pltpu.VMEM((ts, HB), F32),
            pltpu.VMEM((GDN_HEADS, ts, GDN_DV), F32), pltpu.VMEM((GDN_HEADS, ts, GDN_DK), F32),
            pltpu.VMEM((GDN_HEADS, ts, GDN_DK), F32), pltpu.VMEM((GDN_HEADS, ts, GDN_DK), F32),
            pltpu.VMEM((GDN_HEADS, ts, CHUNK), F32),
            pltpu.VMEM((GDN_HEADS, nc, SUBLANES, LANES), F32),
            pltpu.VMEM((GDN_HEADS, GDN_DK, GDN_DV), F32),
        ],
        compiler_params=_cparams("parallel", "arbitrary"),
        name="gdn",
    )(h, h, h, h, hs, conv_w, par, norm_w)


def _expand_heads(x, expand_hi):
    hi = x.astype(BF16)
    r1 = x - hi.astype(F32)
    mid = r1.astype(BF16)
    lo = (r1 - mid.astype(F32)).astype(BF16)
    e = expand_hi
    return (jnp.dot(hi, e, preferred_element_type=F32) + jnp.dot(mid, e, preferred_element_type=F32)
            + jnp.dot(lo, e, preferred_element_type=F32))


def _ssd_body(z_ref, xbc_ref, hs_ref, cw_ref, cb_ref, par_ref, dskip_ref, nw_ref, o_ref,
              tail_ref, xbuf_ref, xs_ref, bc_ref, yd_ref, hin_ref, state_ref):
    ts = z_ref.shape[0]
    nc = ts // CHUNK
    gw = SSM_INNER // SSM_GROUPS

    @pl.when(pl.program_id(1) == 0)
    def _():
        tail_ref[...] = jnp.zeros_like(tail_ref)
        state_ref[...] = jnp.zeros_like(state_ref)

    for slot, dst in enumerate((xs_ref, bc_ref)):
        cols = slice(slot * HB, (slot + 1) * HB)
        dst[...] = _conv_silu(xbc_ref[:, cols].astype(F32), tail_ref, slot, xbuf_ref, cw_ref[:, cols],
                              cb_ref[:, cols])

    hs = hs_ref[...]
    dt = _softplus(hs + par_ref[1:2, :])
    a_all = -jnp.exp(par_ref[0:1, :]) * dt
    r, c = _chunk_iota(nc)
    tril = r >= c
    lincl = jnp.where(tril, 1.0, 0.0).astype(BF16)
    a3 = a_all.reshape(nc, CHUNK, LANES)
    ahi = a3.astype(BF16)
    ar = a3 - ahi.astype(F32)
    amid = ar.astype(BF16)
    alo = (ar - amid.astype(F32)).astype(BF16)
    acum = _bdot(lincl, ahi) + _bdot(lincl, amid) + _bdot(lincl, alo)
    a_last = acum[:, CHUNK - 1:CHUNK, :]
    lane = lax.broadcasted_iota(jnp.int32, (LANES, SSM_INNER), 0)
    chan = lax.broadcasted_iota(jnp.int32, (LANES, SSM_INNER), 1)
    expand = jnp.where(lane - SSM_HEADS == chan // SSM_HEADDIM, 1.0, 0.0).astype(BF16)
    dt_x = _expand_heads(dt, expand)
    ea_x = _expand_heads(jnp.exp(acum).reshape(ts, LANES), expand)
    din_x = _expand_heads(jnp.exp(a_last - acum).reshape(ts, LANES), expand)
    cd_x = _expand_heads(jnp.broadcast_to(jnp.exp(a_last), (nc, SUBLANES, LANES)).reshape(nc * SUBLANES, LANES),
                         expand).reshape(nc, SUBLANES, SSM_INNER)

    xs = xs_ref[...]
    xdt = xs * dt_x
    xw = xdt * din_x
    for g in range(SSM_GROUPS):
        gcols = slice(g * gw, (g + 1) * gw)
        b3 = bc_ref[:, g * SSM_STATE:(g + 1) * SSM_STATE].reshape(nc, CHUNK, SSM_STATE)
        c3 = bc_ref[:, gw + g * SSM_STATE:gw + (g + 1) * SSM_STATE].reshape(nc, CHUNK, SSM_STATE)
        cb = _bdot_nt(c3, b3)
        for hr in range(SSM_HEADS // SSM_GROUPS):
            hd = g * (SSM_HEADS // SSM_GROUPS) + hr
            hcols = slice(hd * SSM_HEADDIM, (hd + 1) * SSM_HEADDIM)
            ah3 = a_all[:, SSM_HEADS + hd:SSM_HEADS + hd + 1].reshape(nc, CHUNK, 1)
            seg = jnp.where(tril, jnp.exp(_segment_logdecay(ah3, r, c)), 0.0)
            yd = _bdot(seg * cb, xdt[:, hcols].reshape(nc, CHUNK, SSM_HEADDIM))
            yd_ref[:, hcols] = yd.reshape(ts, SSM_HEADDIM)
        bt = jnp.swapaxes(b3, 1, 2)
        st = _bdot(bt, xw[:, gcols].reshape(nc, CHUNK, gw))
        hcur = state_ref[g]
        for ci in range(nc):
            hin_ref[ci] = hcur
            hcur = hcur * cd_x[ci, 0:1, gcols] + st[ci]
        state_ref[g] = hcur
        y_off = _bdot(c3, hin_ref[...]).reshape(ts, gw) * ea_x[:, gcols]
        y = yd_ref[:, gcols] + y_off + dskip_ref[:, gcols] * xs[:, gcols]
        gated = y * _silu(z_ref[:, gcols].astype(F32))
        o_ref[:, gcols] = _rms(gated, nw_ref[:, gcols]).astype(o_ref.dtype)


def _ssd(h, hs, conv_w, conv_b, par, dskip, norm_w, bsz, seq):
    ts = min(SEQ_TILE, seq)
    ns = seq // ts
    nc = ts // CHUNK
    gw = SSM_INNER // SSM_GROUPS
    full = lambda shape: pl.BlockSpec(shape, lambda b, s: (0,) * len(shape))
    return pl.pallas_call(
        _ssd_body,
        grid=(bsz, ns),
        in_specs=[pl.BlockSpec((ts, HB), lambda b, s: (b * ns + s, COL_SZ)),
                  pl.BlockSpec((ts, 2 * HB), lambda b, s: (b * ns + s, COL_SXBC)),
                  pl.BlockSpec((ts, LANES), lambda b, s: (b * ns + s, 0)),
                  full((CONV_WIDTH, 2 * HB)), full((1, 2 * HB)), full((SUBLANES, LANES)),
                  full((1, SSM_INNER)), full((1, SSM_INNER))],
        out_specs=pl.BlockSpec((ts, HB), lambda b, s: (b * ns + s, 0)),
        out_shape=jax.ShapeDtypeStruct((bsz * seq, HB), BF16),
        scratch_shapes=[
            pltpu.VMEM((2, SUBLANES, HB), F32),
            pltpu.VMEM((ts + SUBLANES, HB), F32),
            pltpu.VMEM((ts, HB), F32), pltpu.VMEM((ts, HB), F32), pltpu.VMEM((ts, HB), F32),
            pltpu.VMEM((nc, SSM_STATE, gw), F32),
            pltpu.VMEM((SSM_GROUPS, SSM_STATE, gw), F32),
        ],
        compiler_params=_cparams("parallel", "arbitrary"),
        name="ssd",
    )(h, h, hs, conv_w, conv_b, par, dskip, norm_w)


def _rope(x, cos, sin_signed):
    width = x.shape[-1]
    half = DIFF_DQK // 2
    lane = lax.broadcasted_iota(jnp.int32, x.shape, x.ndim - 1)
    fwd = pltpu.roll(x, width - half, axis=x.ndim - 1)
    bwd = pltpu.roll(x, half, axis=x.ndim - 1)
    partner = jnp.where((lane & half) == 0, fwd, bwd)
    return x * cos + partner * sin_signed


def _diff_body(q_ref, k_ref, v_ref, cos_ref, sin_ref, lam_ref, nw_ref, o_ref, kr_ref, *, lambda_init):
    qb = pl.program_id(1)
    tq = q_ref.shape[0]
    q0 = pl.multiple_of(qb * tq, tq)
    cos = cos_ref[...]
    sin = sin_ref[...]
    kr_ref[pl.ds(q0, tq), :] = _rope(k_ref[...].astype(F32), cos, sin).astype(BF16)
    qr = _rope(q_ref[...].astype(F32), cos, sin) * (DIFF_DQK ** -0.5)

    lp = lam_ref[...]
    prod01 = jnp.sum(lp[0:1, :] * lp[1:2, :], axis=-1, keepdims=True)
    prod23 = jnp.sum(lp[2:3, :] * lp[3:4, :], axis=-1, keepdims=True)
    lam = jnp.exp(prod01) - jnp.exp(prod23) + lambda_init

    lane = lax.broadcasted_iota(jnp.int32, (tq, 2 * DIFF_DQK), 1)
    row = lax.broadcasted_iota(jnp.int32, (tq, tq), 0)
    col = lax.broadcasted_iota(jnp.int32, (tq, tq), 1)
    diag_ok = (row // CHUNK) >= (col // CHUNK)
    nw = nw_ref[...]

    for h in range(DIFF_HEADS):
        hcols = slice(h * DIFF_DV, (h + 1) * DIFF_DV)
        qh = qr[:, hcols]
        qmaps = (jnp.where(lane < DIFF_DQK, qh, 0.0).astype(BF16), jnp.where(lane >= DIFF_DQK, qh, 0.0).astype(BF16))

        def update(kblk, vblk, mask, carry):
            out = []
            for mi in range(2):
                m_prev, l_prev, acc = carry[mi]
                s = _dot_nt(qmaps[mi], kblk)
                if mask is not None:
                    s = jnp.where(mask, s, -jnp.inf)
                m_new = jnp.maximum(m_prev, jnp.max(s, axis=-1, keepdims=True))
                alpha = jnp.exp(m_prev - m_new)
                p = jnp.exp(s - m_new)
                l_new = alpha * l_prev + jnp.sum(p, axis=-1, keepdims=True)
                acc = alpha * acc + _dot(p, vblk)
                out.append((m_new, l_new, acc))
            return tuple(out)

        def kv_step(j, carry):
            k0 = pl.multiple_of(j * tq, tq)
            return update(kr_ref[pl.ds(k0, tq), hcols], v_ref[pl.ds(k0, tq), hcols], None, carry)

        init = tuple((jnp.full((tq, 1), -jnp.inf, F32), jnp.zeros((tq, 1), F32), jnp.zeros((tq, DIFF_DV), F32))
                     for _ in range(2))
        carry = lax.fori_loop(0, qb, kv_step, init)
        carry = update(kr_ref[pl.ds(q0, tq), hcols], v_ref[pl.ds(q0, tq), hcols], diag_ok, carry)
        (_, l0, acc0), (_, l1, acc1) = carry
        o = acc0 / l0 - lam * (acc1 / l1)
        o_ref[:, hcols] = (_rms(o, nw) * (1.0 - lambda_init)).astype(o_ref.dtype)


def _diff_attention(h, cos, sin, lam_params, norm_w, lambda_init, bsz, seq):
    tq = Q_BLOCK
    nq = seq // tq
    full = lambda shape: pl.BlockSpec(shape, lambda b, q: (0,) * len(shape))
    return pl.pallas_call(
        functools.partial(_diff_body, lambda_init=lambda_init),
        grid=(bsz, nq),
        in_specs=[pl.BlockSpec((tq, HB), lambda b, q: (b * nq + q, COL_DQ)),
                  pl.BlockSpec((tq, HB), lambda b, q: (b * nq + q, COL_DK)),
                  pl.BlockSpec((seq, HB), lambda b, q: (b, COL_DV)),
                  pl.BlockSpec((tq, HB), lambda b, q: (q, 0)),
                  pl.BlockSpec((tq, HB), lambda b, q: (q, 0)),
                  full((4, DIFF_DQK)), full((1, DIFF_DV))],
        out_specs=pl.BlockSpec((tq, HB), lambda b, q: (b * nq + q, 0)),
        out_shape=jax.ShapeDtypeStruct((bsz * seq, HB), BF16),
        scratch_shapes=[pltpu.VMEM((seq, HB), BF16)],
        compiler_params=_cparams("parallel", "arbitrary"),
        name="diff_attn",
    )(h, h, h, cos, sin, lam_params, norm_w)


def _mem_body(q_ref, kv_ref, o_ref):
    width = MEM_HEADS * MEM_HEADDIM
    for h in range(MEM_HEADS):
        cols = slice(h * MEM_HEADDIM, (h + 1) * MEM_HEADDIM)
        q = q_ref[:, cols].astype(F32) * (MEM_HEADDIM ** -0.5)
        s = _dot_nt(q, kv_ref[:, cols])
        m = jnp.max(s, axis=-1, keepdims=True)
        p = jnp.exp(s - m)
        p = p / jnp.sum(p, axis=-1, keepdims=True)
        o_ref[:, cols] = _dot(p, kv_ref[:, width + h * MEM_HEADDIM:width + (h + 1) * MEM_HEADDIM]).astype(o_ref.dtype)


def _mem_attention(h, kv, bsz, seq, mem_len):
    tm = min(ROW_TILE, seq)
    ns = seq // tm
    return pl.pallas_call(
        _mem_body,
        grid=(bsz, ns),
        in_specs=[pl.BlockSpec((tm, HB), lambda b, s: (b * ns + s, COL_MQ)),
                  pl.BlockSpec((mem_len, 2 * HB), lambda b, s: (b, 0))],
        out_specs=pl.BlockSpec((tm, HB), lambda b, s: (b * ns + s, 0)),
        out_shape=jax.ShapeDtypeStruct((bsz * seq, HB), BF16),
        compiler_params=_cparams("parallel", "parallel"),
        name="mem_attn",
    )(h, kv)


def _pack_halves(x):
    n = x.shape[1] // 2
    lo = pltpu.bitcast(x[:, :n].astype(BF16).astype(F32), U32)
    hi = pltpu.bitcast(x[:, n:].astype(BF16).astype(F32), U32)
    return (lo >> 16) | hi


def _unpack_halves(w):
    lo = pltpu.bitcast(w << 16, F32)
    hi = pltpu.bitcast(w & jnp.uint32(0xFFFF0000), F32)
    return lo.astype(BF16), hi.astype(BF16)


def _merge_body(g_ref, y0_ref, y1_ref, y2_ref, y3_ref, x_ref, wb_ref, wo_ref, lng_ref, lnb_ref,
                rwh_ref, rwl_ref, rb_ref, xo_ref, pa_ref, pb_ref, route_ref, cnt_ref):
    tm = x_ref.shape[0]
    merged = jnp.zeros((tm, D_MODEL), F32)
    for i, y_ref in enumerate((y0_ref, y1_ref, y2_ref, y3_ref)):
        gate = jax.nn.sigmoid(g_ref[:, i * D_MODEL:(i + 1) * D_MODEL].astype(F32))
        merged = merged + gate * jnp.dot(y_ref[...], wb_ref[i], preferred_element_type=F32)
    mix = _dot(merged, wo_ref[...])
    xn = _layer_norm(DEEPNORM_ALPHA * x_ref[...] + mix, lng_ref[...], lnb_ref[...])
    xo_ref[...] = xn
    pa_ref[...] = _pack_halves(xn[:, :D_MODEL // 2])
    pb_ref[...] = _pack_halves(xn[:, D_MODEL // 2:])

    xh, xl = _split2(xn)
    logits = (jnp.dot(xh, rwh_ref[...], preferred_element_type=F32)
              + (jnp.dot(xh, rwl_ref[...], preferred_element_type=F32)
                 + jnp.dot(xl, rwh_ref[...], preferred_element_type=F32))) + rb_ref[...]
    lane = lax.broadcasted_iota(jnp.int32, (tm, LANES), 1)
    work = jnp.where(lane < N_EXPERTS, logits, -jnp.inf)
    vals, ids, sels = [], [], []
    for _ in range(TOP_K):
        mx = jnp.max(work, axis=-1, keepdims=True)
        idx = jnp.min(jnp.where(work == mx, lane, LANES), axis=-1, keepdims=True)
        sel = lane == idx
        vals.append(mx)
        ids.append(idx)
        sels.append(sel)
        work = jnp.where(sel, -jnp.inf, work)
    exps = [jnp.exp(v - vals[0]) for v in vals]
    denom = exps[0] + exps[1] + exps[2] + exps[3]
    chosen = jnp.where(sels[0] | sels[1] | sels[2] | sels[3], 1.0, 0.0)
    r = lax.broadcasted_iota(jnp.int32, (tm, tm), 0)
    c = lax.broadcasted_iota(jnp.int32, (tm, tm), 1)
    ltri = jnp.where(r >= c, 1.0, 0.0).astype(BF16)
    cum = jnp.dot(ltri, chosen.astype(BF16), preferred_element_type=F32)
    out = jnp.zeros((tm, LANES), F32)
    for k in range(TOP_K):
        rank = jnp.sum(jnp.where(sels[k], cum, 0.0), axis=-1, keepdims=True) - 1.0
        out = jnp.where(lane == k, exps[k] / denom, out)
        out = jnp.where(lane == TOP_K + k, ids[k].astype(F32), out)
        out = jnp.where(lane == 2 * TOP_K + k, rank, out)
    route_ref[...] = out
    cnt_ref[0] = jnp.broadcast_to(cum[tm - 1:tm, :], (SUBLANES, LANES))


def _merge(h, ys, x, wb, wo, lng, lnb, rwh, rwl, rb):
    t = x.shape[0]
    tm = min(ROW_TILE, t)
    nt = t // tm
    tok = lambda width: pl.BlockSpec((tm, width), lambda i: (i, 0))
    full = lambda shape: pl.BlockSpec(shape, lambda i: (0,) * len(shape))
    return pl.pallas_call(
        _merge_body,
        grid=(nt,),
        in_specs=[tok(N_BRANCH * D_MODEL), tok(HB), tok(HB), tok(HB), tok(HB), tok(D_MODEL),
                  full((N_BRANCH, HB, D_MODEL)), full((D_MODEL, D_MODEL)), full((1, D_MODEL)), full((1, D_MODEL)),
                  full((D_MODEL, LANES)), full((D_MODEL, LANES)), full((1, LANES))],
        out_specs=[tok(D_MODEL), tok(SC_WORDS), tok(SC_WORDS), tok(LANES),
                   pl.BlockSpec((1, SUBLANES, LANES), lambda i: (i, 0, 0))],
        out_shape=[jax.ShapeDtypeStruct((t, D_MODEL), F32), jax.ShapeDtypeStruct((t, SC_WORDS), U32),
                   jax.ShapeDtypeStruct((t, SC_WORDS), U32), jax.ShapeDtypeStruct((t, LANES), F32),
                   jax.ShapeDtypeStruct((nt, SUBLANES, LANES), F32)],
        compiler_params=_cparams("parallel"),
        name="merge_ln_router",
    )(h, *ys, x, wb, wo, lng, lnb, rwh, rwl, rb)


def _sc_mesh():
    return plsc.VectorSubcoreMesh(core_axis_name="core", subcore_axis_name="subcore")


def _sc_dispatch(xa, xb, pos, n_out):
    t, words = xa.shape
    k = pos.shape[0]
    out = jax.ShapeDtypeStruct((n_out, words), xa.dtype)

    @functools.partial(pl.kernel, out_type=(out, out), mesh=_sc_mesh())
    def scatter_rows(xa_hbm, xb_hbm, i_hbm, oa_hbm, ob_hbm):
        for x_hbm, o_hbm in ((xa_hbm, oa_hbm), (xb_hbm, ob_hbm)):
            def body(x_vmem, i_vmem, o_hbm=o_hbm):
                for kk in range(k):
                    pltpu.sync_copy(x_vmem, o_hbm.at[i_vmem.at[kk]])

            pltpu.emit_pipeline(
                body, grid=(t // SC_WINDOW,),
                in_specs=[pl.BlockSpec((SC_WINDOW, words), lambda i: (i, 0)),
                          pl.BlockSpec((k, SC_WINDOW), lambda i: (0, i))],
                out_specs=[],
                core_axis_name=("core", "subcore"),
                dimension_semantics=(pltpu.PARALLEL,),
            )(x_hbm, i_hbm)

    return scatter_rows(xa, xb, pos)


def _sc_collect(ya, yb, pos_flat):
    n = pos_flat.shape[1]
    words = ya.shape[1]
    out = jax.ShapeDtypeStruct((n, words), ya.dtype)

    @functools.partial(pl.kernel, out_type=(out, out), mesh=_sc_mesh())
    def gather_rows(ya_hbm, yb_hbm, i_hbm, oa_hbm, ob_hbm):
        for y_hbm, o_hbm in ((ya_hbm, oa_hbm), (yb_hbm, ob_hbm)):
            def body(i_vmem, o_vmem, y_hbm=y_hbm):
                pltpu.sync_copy(y_hbm.at[i_vmem.at[0]], o_vmem)

            pltpu.emit_pipeline(
                body, grid=(n // SC_WINDOW,),
                in_specs=[pl.BlockSpec((1, SC_WINDOW), lambda i: (0, i))],
                out_specs=[pl.BlockSpec((SC_WINDOW, words), lambda i: (i, 0))],
                core_axis_name=("core", "subcore"),
                dimension_semantics=(pltpu.PARALLEL,),
            )(i_hbm, o_hbm)

    return gather_rows(ya, yb, pos_flat)


def _expert_body(te_ref, xa_ref, xb_ref, wg_ref, wl_ref, bg_ref, bl_ref, wd_ref, bd_ref, ya_ref, yb_ref):
    q = D_MODEL // 4
    x0, x1 = _unpack_halves(xa_ref[...])
    x2, x3 = _unpack_halves(xb_ref[...])
    parts = (x0, x1, x2, x3)

    def proj(w_ref, b_ref):
        acc = b_ref[0]
        for i, xp in enumerate(parts):
            acc = acc + jnp.dot(xp, w_ref[0, i * q:(i + 1) * q, :], preferred_element_type=F32)
        return acc

    glu = jnp.minimum(proj(wg_ref, bg_ref), SWIGLU_LIMIT)
    lin = jnp.clip(proj(wl_ref, bl_ref), -SWIGLU_LIMIT, SWIGLU_LIMIT)
    act = (lin + 1.0) * glu * jax.nn.sigmoid(SWIGLU_ALPHA * glu)
    y = _dot(act, wd_ref[0]) + bd_ref[0]
    ya_ref[...] = _pack_halves(y[:, :D_MODEL // 2])
    yb_ref[...] = _pack_halves(y[:, D_MODEL // 2:])


def _expert_mlp(tile_expert, xa, xb, wg, wl, bg, bl, wd, bd):
    n = xa.shape[0]
    tr = EXPERT_TILE
    tok = pl.BlockSpec((tr, SC_WORDS), lambda i, te: (i, 0))
    per_expert = lambda shape: pl.BlockSpec((1,) + shape, lambda i, te: (te[i],) + (0,) * len(shape))
    out = jax.ShapeDtypeStruct((n, SC_WORDS), U32)
    return pl.pallas_call(
        _expert_body,
        grid_spec=pltpu.PrefetchScalarGridSpec(
            num_scalar_prefetch=1,
            grid=(n // tr,),
            in_specs=[tok, tok, per_expert((D_MODEL, D_FF)), per_expert((D_MODEL, D_FF)),
                      per_expert((1, D_FF)), per_expert((1, D_FF)),
                      per_expert((D_FF, D_MODEL)), per_expert((1, D_MODEL))],
            out_specs=[tok, tok],
        ),
        out_shape=[out, out],
        compiler_params=_cparams("arbitrary"),
        name="expert_mlp",
    )(tile_expert, xa, xb, wg, wl, bg, bl, wd, bd)


def _combine_body(ya_ref, yb_ref, route_ref, x_ref, lng_ref, lnb_ref, xo_ref, xb_ref):
    x = x_ref[...]
    route = route_ref[...]
    half = D_MODEL // 2
    q = D_MODEL // 4
    ffn_a = jnp.zeros((x.shape[0], half), F32)
    ffn_b = jnp.zeros((x.shape[0], half), F32)
    for k in range(TOP_K):
        wk = route[:, k:k + 1]
        a0, a1 = _unpack_halves(ya_ref[k])
        b0, b1 = _unpack_halves(yb_ref[k])
        ffn_a = ffn_a + wk * jnp.concatenate([a0.astype(F32), a1.astype(F32)], axis=-1)
        ffn_b = ffn_b + wk * jnp.concatenate([b0.astype(F32), b1.astype(F32)], axis=-1)
    del q
    ffn = jnp.concatenate([ffn_a, ffn_b], axis=-1)
    xn = _layer_norm(DEEPNORM_ALPHA * x + ffn, lng_ref[...], lnb_ref[...])
    xo_ref[...] = xn
    xb_ref[...] = xn.astype(BF16)


def _combine(ya, yb, route, x, lng, lnb):
    t = x.shape[0]
    tm = min(ROW_TILE, t)
    tok = lambda width: pl.BlockSpec((tm, width), lambda i: (i, 0))
    gathered = pl.BlockSpec((TOP_K, tm, SC_WORDS), lambda i: (0, i, 0))
    full = lambda shape: pl.BlockSpec(shape, lambda i: (0,) * len(shape))
    return pl.pallas_call(
        _combine_body,
        grid=(t // tm,),
        in_specs=[gathered, gathered, tok(LANES), tok(D_MODEL), full((1, D_MODEL)), full((1, D_MODEL))],
        out_specs=[tok(D_MODEL), tok(D_MODEL)],
        out_shape=[jax.ShapeDtypeStruct((t, D_MODEL), F32), jax.ShapeDtypeStruct((t, D_MODEL), BF16)],
        compiler_params=_cparams("parallel"),
        name="combine_ln",
    )(ya, yb, route, x, lng, lnb)


def _route_positions(route, counts):
    t = route.shape[0]
    nt = counts.shape[0]
    tm = t // nt
    ids = route[:, TOP_K:2 * TOP_K].astype(jnp.int32)
    rank = route[:, 2 * TOP_K:3 * TOP_K].astype(jnp.int32)
    cnt = counts[:, 0, :N_EXPERTS].astype(jnp.int32)
    before = jnp.cumsum(cnt, axis=0) - cnt
    total = jnp.sum(cnt, axis=0)
    padded = ((total + EXPERT_TILE - 1) // EXPERT_TILE) * EXPERT_TILE
    ends = jnp.cumsum(padded)
    base = ends - padded
    tile_of_tok = jnp.arange(t, dtype=jnp.int32) // tm
    pos = base[ids] + before[tile_of_tok[:, None], ids] + rank
    n_rows = TOP_K * t + N_EXPERTS * EXPERT_TILE
    tile_start = jnp.arange(n_rows // EXPERT_TILE, dtype=jnp.int32) * EXPERT_TILE
    tile_expert = jnp.minimum(jnp.searchsorted(ends, tile_start, side='right'), N_EXPERTS - 1).astype(jnp.int32)
    return pos.T, tile_expert, n_rows


def _rope_tables(seq):
    half = DIFF_DQK // 2
    inv_freq = ROPE_THETA ** (-jnp.arange(half, dtype=F32) / half)
    ang = jnp.arange(seq, dtype=F32)[:, None] * inv_freq[None, :]
    cos = jnp.cos(ang)
    sin = jnp.sin(ang)
    reps = HB // DIFF_DQK
    cos_t = jnp.tile(jnp.concatenate([cos, cos], axis=-1), (1, reps))
    sin_t = jnp.tile(jnp.concatenate([-sin, sin], axis=-1), (1, reps))
    return cos_t, sin_t


def _lane_row(vec, offset):
    return jnp.zeros((LANES,), F32).at[offset:offset + vec.shape[0]].set(vec.astype(F32))


def _layer(l, x, xbf, kv_src, p, bsz, seq, cos, sin):
    t = bsz * seq
    w_in = p['w_in'][l]
    w_big = jnp.concatenate([w_in[:, 5648:9744], w_in[:, 0:2048], w_in[:, 2056:5128], w_in[:, 5136:5648]],
                            axis=1).astype(BF16)
    w_small = jnp.concatenate([w_in[:, 2048:2056], w_in[:, 5128:5136],
                               jnp.zeros((D_MODEL, LANES - 16), F32)], axis=1).astype(BF16)
    h = _matmul(xbf, w_big, BF16, min(1024, t), HB, "in_proj")
    hs = _matmul(xbf, w_small, F32, min(1024, t), LANES, "in_proj_small")

    gdn_par = jnp.zeros((SUBLANES, LANES), F32)
    gdn_par = gdn_par.at[0].set(_lane_row(p['gdn_a_log'][l], GDN_HEADS)).at[1].set(_lane_row(p['gdn_dt_bias'][l], GDN_HEADS))
    y_gdn = _gdn(h, hs, p['gdn_conv_w'][l], gdn_par, p['gdn_norm_w'][l][None, :], bsz, seq)

    lambda_init = 0.8 - 0.6 * math.exp(-0.3 * l)
    y_diff = _diff_attention(h, cos, sin, p['diff_lambda'][l], p['diff_norm_w'][l][None, :], lambda_init, bsz, seq)

    ssm_par = jnp.zeros((SUBLANES, LANES), F32)
    ssm_par = ssm_par.at[0].set(_lane_row(p['ssm_a_log'][l], SSM_HEADS)).at[1].set(_lane_row(p['ssm_dt_bias'][l], SSM_HEADS))
    dskip = jnp.repeat(p['ssm_d'][l], SSM_HEADDIM)[None, :]
    y_ssm = _ssd(h, hs, p['ssm_conv_w'][l], p['ssm_conv_b'][l][None, :], ssm_par, dskip,
                 p['ssm_norm_w'][l][None, :], bsz, seq)

    mem_len = kv_src.shape[0] // bsz
    kv = _matmul(kv_src, p['w_mem'][l].astype(BF16), BF16, min(1024, kv_src.shape[0]), HB, "mem_kv")
    y_mem = _mem_attention(h, kv, bsz, seq, mem_len)

    rw = jnp.concatenate([p['router_w'][l], jnp.zeros((D_MODEL, LANES - N_EXPERTS), F32)], axis=1)
    rwh = rw.astype(BF16)
    rwl = (rw - rwh.astype(F32)).astype(BF16)
    rb = _lane_row(p['router_b'][l], 0)[None, :]
    x1, xa, xb, route, counts = _merge(h, (y_gdn, y_diff, y_ssm, y_mem), x, p['w_branch'][l].astype(BF16),
                                       p['w_out'][l].astype(BF16), p['ln1_g'][l][None, :], p['ln1_b'][l][None, :],
                                       rwh, rwl, rb)

    pos, tile_expert, n_rows = _route_positions(route, counts)
    xsa, xsb = _sc_dispatch(xa, xb, pos, n_rows)
    wgu = p['w_gate_up'][l]
    bgu = p['b_gate_up'][l]
    ysa, ysb = _expert_mlp(tile_expert, xsa, xsb, wgu[:, :, 0::2].astype(BF16), wgu[:, :, 1::2].astype(BF16),
                           bgu[:, None, 0::2], bgu[:, None, 1::2], p['w_down'][l].astype(BF16),
                           p['b_down'][l][:, None, :])
    ga, gb = _sc_collect(ysa, ysb, pos.reshape(1, TOP_K * t))
    x2, x2bf = _combine(ga.reshape(TOP_K, t, SC_WORDS), gb.reshape(TOP_K, t, SC_WORDS), route, x1,
                        p['ln2_g'][l][None, :], p['ln2_b'][l][None, :])
    return x2, x2bf


def kernel(x, mem, w_in, gdn_conv_w, gdn_a_log, gdn_dt_bias, gdn_norm_w, diff_lambda, diff_norm_w, ssm_conv_w, ssm_conv_b, ssm_a_log, ssm_dt_bias, ssm_d, ssm_norm_w, w_mem, w_branch, w_out, ln1_g, ln1_b, router_w, router_b, w_gate_up, b_gate_up, w_down, b_down, ln2_g, ln2_b):
    p = dict(w_in=w_in, gdn_conv_w=gdn_conv_w, gdn_a_log=gdn_a_log, gdn_dt_bias=gdn_dt_bias, gdn_norm_w=gdn_norm_w,
             diff_lambda=diff_lambda, diff_norm_w=diff_norm_w, ssm_conv_w=ssm_conv_w, ssm_conv_b=ssm_conv_b,
             ssm_a_log=ssm_a_log, ssm_dt_bias=ssm_dt_bias, ssm_d=ssm_d, ssm_norm_w=ssm_norm_w, w_mem=w_mem,
             w_branch=w_branch, w_out=w_out, ln1_g=ln1_g, ln1_b=ln1_b, router_w=router_w, router_b=router_b,
             w_gate_up=w_gate_up, b_gate_up=b_gate_up, w_down=w_down, b_down=b_down, ln2_g=ln2_g, ln2_b=ln2_b)
    bsz, seq, d = x.shape
    t = bsz * seq
    cos, sin = _rope_tables(seq)
    xf = x.reshape(t, d)
    xbf = xf.astype(BF16)
    kv_src = mem.reshape(bsz * mem.shape[1], d).astype(BF16)
    for l in range(w_in.shape[0]):
        xf, xbf = _layer(l, xf, xbf, kv_src, p, bsz, seq, cos, sin)
    return xf.reshape(bsz, seq, d)
```

```python
import functools
import math

import jax
import jax.numpy as jnp
from jax import lax
from jax.experimental import pallas as pl
from jax.experimental.pallas import tpu as pltpu
from jax.experimental.pallas import tpu_sc as plsc

F32 = jnp.float32
BF16 = jnp.bfloat16
U32 = jnp.uint32

D_MODEL = 1024
DEPTH = 4
CHUNK = 64
Q_BLOCK = 128
CONV_WIDTH = 4
ROPE_THETA = 10000.0
NORM_EPS = 1e-6
LN_EPS = 1e-5

GDN_HEADS = 4
GDN_DK = 128
GDN_DV = 128
DIFF_HEADS = 4
DIFF_DQK = 64
DIFF_DV = 128
SSM_HEADS = 8
SSM_HEADDIM = 64
SSM_GROUPS = 2
SSM_STATE = 128
SSM_INNER = 512
MEM_HEADS = 4
MEM_HEADDIM = 128
N_BRANCH = 4
N_EXPERTS = 32
TOP_K = 4
D_FF = 1024
SWIGLU_LIMIT = 7.0
SWIGLU_ALPHA = 1.702
DEEPNORM_ALPHA = (2.0 * DEPTH) ** 0.25

LANES = 128
SUBLANES = 8
VMEM_LIMIT = 56 * 1024 * 1024

HB = 512
COL_GATES = 0
COL_GQ, COL_GK, COL_GV, COL_GZ = 8, 9, 10, 11
COL_DQ, COL_DK, COL_DV = 12, 13, 14
COL_SZ = 15
COL_SXBC = 8
COL_MQ = 18
H_WIDTH = 19 * HB

SEQ_TILE = 512
ROW_TILE = 512
EXPERT_TILE = 512
SC_WINDOW = 128
SC_WORDS = 256


def _cparams(*sem):
    return pltpu.CompilerParams(dimension_semantics=sem, vmem_limit_bytes=VMEM_LIMIT)


def _dot(a, b):
    return jnp.dot(a.astype(BF16), b.astype(BF16), preferred_element_type=F32)


def _dot_nt(a, b):
    return lax.dot_general(a.astype(BF16), b.astype(BF16), (((1,), (1,)), ((), ())), preferred_element_type=F32)


def _bdot(a, b):
    return jnp.einsum('cmk,ckn->cmn', a.astype(BF16), b.astype(BF16), preferred_element_type=F32)


def _bdot_nt(a, b):
    return jnp.einsum('cmk,cnk->cmn', a.astype(BF16), b.astype(BF16), preferred_element_type=F32)


def _split2(a):
    hi = a.astype(BF16)
    lo = (a - hi.astype(F32)).astype(BF16)
    return hi, lo


def _silu(x):
    return x * jax.nn.sigmoid(x)


def _softplus(x):
    return jnp.maximum(x, 0.0) + jnp.log1p(jnp.exp(-jnp.abs(x)))


def _rms(x, w):
    return x * lax.rsqrt(jnp.mean(x * x, axis=-1, keepdims=True) + NORM_EPS) * w


def _layer_norm(x, g, b):
    mu = jnp.mean(x, axis=-1, keepdims=True)
    xc = x - mu
    var = jnp.mean(xc * xc, axis=-1, keepdims=True)
    return xc * lax.rsqrt(var + LN_EPS) * g + b


def _mm_body(x_ref, w_ref, o_ref):
    o_ref[...] = jnp.dot(x_ref[...], w_ref[...], preferred_element_type=F32).astype(o_ref.dtype)


def _matmul(x, w, out_dtype, tm, tn, name):
    m, k = x.shape
    n = w.shape[1]
    return pl.pallas_call(
        _mm_body,
        grid=(m // tm, n // tn),
        in_specs=[pl.BlockSpec((tm, k), lambda i, j: (i, 0)), pl.BlockSpec((k, tn), lambda i, j: (0, j))],
        out_specs=pl.BlockSpec((tm, tn), lambda i, j: (i, j)),
        out_shape=jax.ShapeDtypeStruct((m, n), out_dtype),
        compiler_params=_cparams("parallel", "parallel"),
        name=name,
    )(x, w)


def _conv_silu(x, tail_ref, slot, xbuf_ref, w, bias):
    rows = x.shape[0]
    xbuf_ref[0:SUBLANES, :] = tail_ref[slot]
    xbuf_ref[SUBLANES:SUBLANES + rows, :] = x
    tail_ref[slot] = x[rows - SUBLANES:rows, :]
    y = x * w[CONV_WIDTH - 1:CONV_WIDTH, :]
    for j in range(CONV_WIDTH - 1):
        off = SUBLANES - (CONV_WIDTH - 1) + j
        y = y + xbuf_ref[off:off + rows, :] * w[j:j + 1, :]
    if bias is not None:
        y = y + bias
    return _silu(y)


def _chunk_iota(nc):
    r = lax.broadcasted_iota(jnp.int32, (nc, CHUNK, CHUNK), 1)
    c = lax.broadcasted_iota(jnp.int32, (nc, CHUNK, CHUNK), 2)
    return r, c


def _segment_logdecay(g3, r, c):
    lincl = jnp.where(r >= c, 1.0, 0.0).astype(BF16)
    ghi = g3.astype(BF16)
    glo = (g3 - ghi.astype(F32)).astype(BF16)
    upper = r > c
    m_hi = _bdot(lincl, jnp.where(upper, ghi.astype(F32), 0.0))
    m_lo = _bdot(lincl, jnp.where(upper, glo.astype(F32), 0.0))
    return m_hi + m_lo


def _inv_unit_lower(a, r, c):
    d = jnp.where(r == c, 1.0, 0.0) - jnp.where(((r >> 1) == (c >> 1)) & (r > c), a, 0.0)
    for lb in range(1, 6):
        lower_left = ((r >> (lb + 1)) == (c >> (lb + 1))) & ((r >> lb) > (c >> lb))
        am = jnp.where(lower_left, a, 0.0)
        d = d - _bdot(_bdot(d, am), d)
    return d


def _gdn_body(q_ref, k_ref, v_ref, z_ref, hs_ref, cw_ref, par_ref, nw_ref, o_ref,
              tail_ref, xbuf_ref, qs_ref, ks_ref, vs_ref, u_ref, w_ref, qd_ref, kd_ref, qk_ref, gl_ref,
              state_ref):
    ts = q_ref.shape[0]
    nc = ts // CHUNK

    @pl.when(pl.program_id(1) == 0)
    def _():
        tail_ref[...] = jnp.zeros_like(tail_ref)
        state_ref[...] = jnp.zeros_like(state_ref)

    for slot, (src, dst) in enumerate(((q_ref, qs_ref), (k_ref, ks_ref), (v_ref, vs_ref))):
        w = cw_ref[:, slot * HB:(slot + 1) * HB]
        dst[...] = _conv_silu(src[...].astype(F32), tail_ref, slot, xbuf_ref, w, None)

    hs = hs_ref[...]
    beta_all = jax.nn.sigmoid(hs)
    g_all = -jnp.exp(par_ref[0:1, :]) * _softplus(hs + par_ref[1:2, :])
    r, c = _chunk_iota(nc)
    tril = r >= c
    for h in range(GDN_HEADS):
        sl = slice(h * GDN_DK, (h + 1) * GDN_DK)
        qh = qs_ref[:, sl]
        kh = ks_ref[:, sl]
        qn = qh * lax.rsqrt(jnp.sum(qh * qh, axis=-1, keepdims=True) + NORM_EPS) * (GDN_DK ** -0.5)
        kn = kh * lax.rsqrt(jnp.sum(kh * kh, axis=-1, keepdims=True) + NORM_EPS)
        q3 = qn.reshape(nc, CHUNK, GDN_DK)
        k3 = kn.reshape(nc, CHUNK, GDN_DK)
        v3 = vs_ref[:, sl].reshape(nc, CHUNK, GDN_DV)
        b3 = beta_all[:, h:h + 1].reshape(nc, CHUNK, 1)
        g3 = g_all[:, GDN_HEADS + h:GDN_HEADS + h + 1].reshape(nc, CHUNK, 1)
        m = _segment_logdecay(g3, r, c)
        decay = jnp.where(tril, jnp.exp(m), 0.0)
        kk = _bdot_nt(k3, k3)
        a = jnp.where(r > c, kk * b3 * decay, 0.0)
        t = _inv_unit_lower(a, r, c)
        gc = m[:, :, 0:1] + g3[:, 0:1, :]
        eg = jnp.exp(gc)
        g_last = gc[:, CHUNK - 1:CHUNK, :]
        rhs = jnp.concatenate([v3 * b3, k3 * (b3 * eg)], axis=-1)
        sol = _bdot(t, rhs)
        u_ref[h] = sol[:, :, :GDN_DV].reshape(ts, GDN_DV)
        w_ref[h] = sol[:, :, GDN_DV:].reshape(ts, GDN_DK)
        qd_ref[h] = (q3 * eg).reshape(ts, GDN_DK)
        kd_ref[h] = (k3 * jnp.exp(g_last - gc)).reshape(ts, GDN_DK)
        qk_ref[h] = (_bdot_nt(q3, k3) * decay).reshape(ts, CHUNK)
        gl_ref[h] = jnp.broadcast_to(jnp.exp(g_last), (nc, SUBLANES, LANES))

    nw = nw_ref[...]

    def chunk_step(ci, carry):
        r0 = pl.multiple_of(ci * CHUNK, CHUNK)
        rows = pl.ds(r0, CHUNK)
        for h in range(GDN_HEADS):
            sl = slice(h * GDN_DV, (h + 1) * GDN_DV)
            s = state_ref[h]
            sb = s.astype(BF16)
            v_new = u_ref[h, rows, :] - _dot(w_ref[h, rows, :], sb)
            vb = v_new.astype(BF16)
            o = _dot(qd_ref[h, rows, :], sb) + _dot(qk_ref[h, rows, :], vb)
            kd_t = kd_ref[h, rows, :].T
            state_ref[h] = s * gl_ref[h, ci][0:1, :] + _dot(kd_t, vb)
            zz = z_ref[rows, sl].astype(F32)
            o_ref[rows, sl] = (_rms(o, nw) * _silu(zz)).astype(o_ref.dtype)
        return carry

    lax.fori_loop(0, nc, chunk_step, 0)


def _gdn(h, hs, conv_w, par, norm_w, bsz, seq):
    ts = min(SEQ_TILE, seq)
    ns = seq // ts
    tok = lambda col: pl.BlockSpec((ts, HB), lambda b, s: (b * ns + s, col))
    full = lambda shape: pl.BlockSpec(shape, lambda b, s: (0,) * len(shape))
    nc = ts // CHUNK
    return pl.pallas_call(
        _gdn_body,
        grid=(bsz, ns),
        in_specs=[tok(COL_GQ), tok(COL_GK), tok(COL_GV), tok(COL_GZ),
                  pl.BlockSpec((ts, LANES), lambda b, s: (b * ns + s, 0)),
                  full((CONV_WIDTH, 3 * HB)), full((SUBLANES, LANES)), full((1, GDN_DV))],
        out_specs=pl.BlockSpec((ts, HB), lambda b, s: (b * ns + s, 0)),
        out_shape=jax.ShapeDtypeStruct((bsz * seq, HB), BF16),
        scratch_shapes=[
            pltpu.VMEM((3, SUBLANES, HB), F32),
            pltpu.VMEM((ts + SUBLANES, HB), F32),
            pltpu.VMEM((ts, HB), F32), pltpu.VMEM((ts, HB), F32), pltpu.VMEM((ts, HB), F32),
            pltpu.VMEM((GDN_HEADS, ts, GDN_DV), F32), pltpu.VMEM((GDN_HEADS, ts, GDN_DK), F32),
            pltpu.VMEM((GDN_HEADS, ts, GDN_DK), F32), pltpu.VMEM((GDN_HEADS, ts, GDN_DK), F32),
            pltpu.VMEM((GDN_HEADS, ts, CHUNK), F32),
            pltpu.VMEM((GDN_HEADS, nc, SUBLANES, LANES), F32),
            pltpu.VMEM((GDN_HEADS, GDN_DK, GDN_DV), F32),
        ],
        compiler_params=_cparams("parallel", "arbitrary"),
        name="gdn",
    )(h, h, h, h, hs, conv_w, par, norm_w)


def _expand_heads(x, expand_hi):
    hi = x.astype(BF16)
    r1 = x - hi.astype(F32)
    mid = r1.astype(BF16)
    lo = (r1 - mid.astype(F32)).astype(BF16)
    e = expand_hi
    return (jnp.dot(hi, e, preferred_element_type=F32) + jnp.dot(mid, e, preferred_element_type=F32)
            + jnp.dot(lo, e, preferred_element_type=F32))


def _ssd_body(z_ref, xbc_ref, hs_ref, cw_ref, cb_ref, par_ref, dskip_ref, nw_ref, o_ref,
              tail_ref, xbuf_ref, xs_ref, bc_ref, yd_ref, hin_ref, state_ref):
    ts = z_ref.shape[0]
    nc = ts // CHUNK
    gw = SSM_INNER // SSM_GROUPS

    @pl.when(pl.program_id(1) == 0)
    def _():
        tail_ref[...] = jnp.zeros_like(tail_ref)
        state_ref[...] = jnp.zeros_like(state_ref)

    for slot, dst in enumerate((xs_ref, bc_ref)):
        cols = slice(slot * HB, (slot + 1) * HB)
        dst[...] = _conv_silu(xbc_ref[:, cols].astype(F32), tail_ref, slot, xbuf_ref, cw_ref[:, cols],
                              cb_ref[:, cols])

    hs = hs_ref[...]
    dt = _softplus(hs + par_ref[1:2, :])
    a_all = -jnp.exp(par_ref[0:1, :]) * dt
    r, c = _chunk_iota(nc)
    tril = r >= c
    lincl = jnp.where(tril, 1.0, 0.0).astype(BF16)
    a3 = a_all.reshape(nc, CHUNK, LANES)
    ahi = a3.astype(BF16)
    ar = a3 - ahi.astype(F32)
    amid = ar.astype(BF16)
    alo = (ar - amid.astype(F32)).astype(BF16)
    acum = _bdot(lincl, ahi) + _bdot(lincl, amid) + _bdot(lincl, alo)
    a_last = acum[:, CHUNK - 1:CHUNK, :]
    lane = lax.broadcasted_iota(jnp.int32, (LANES, SSM_INNER), 0)
    chan = lax.broadcasted_iota(jnp.int32, (LANES, SSM_INNER), 1)
    expand = jnp.where(lane - SSM_HEADS == chan // SSM_HEADDIM, 1.0, 0.0).astype(BF16)
    dt_x = _expand_heads(dt, expand)
    ea_x = _expand_heads(jnp.exp(acum).reshape(ts, LANES), expand)
    din_x = _expand_heads(jnp.exp(a_last - acum).reshape(ts, LANES), expand)
    cd_x = _expand_heads(jnp.broadcast_to(jnp.exp(a_last), (nc, SUBLANES, LANES)).reshape(nc * SUBLANES, LANES),
                         expand).reshape(nc, SUBLANES, SSM_INNER)

    xs = xs_ref[...]
    xdt = xs * dt_x
    xw = xdt * din_x
    for g in range(SSM_GROUPS):
        gcols = slice(g * gw, (g + 1) * gw)
        b3 = bc_ref[:, g * SSM_STATE:(g + 1) * SSM_STATE].reshape(nc, CHUNK, SSM_STATE)
        c3 = bc_ref[:, gw + g * SSM_STATE:gw + (g + 1) * SSM_STATE].reshape(nc, CHUNK, SSM_STATE)
        cb = _bdot_nt(c3, b3)
        for hr in range(SSM_HEADS // SSM_GROUPS):
            hd = g * (SSM_HEADS // SSM_GROUPS) + hr
            hcols = slice(hd * SSM_HEADDIM, (hd + 1) * SSM_HEADDIM)
            ah3 = a_all[:, SSM_HEADS + hd:SSM_HEADS + hd + 1].reshape(nc, CHUNK, 1)
            seg = jnp.where(tril, jnp.exp(_segment_logdecay(ah3, r, c)), 0.0)
            yd = _bdot(seg * cb, xdt[:, hcols].reshape(nc, CHUNK, SSM_HEADDIM))
            yd_ref[:, hcols] = yd.reshape(ts, SSM_HEADDIM)
        bt = jnp.swapaxes(b3, 1, 2)
        st = _bdot(bt, xw[:, gcols].reshape(nc, CHUNK, gw))
        hcur = state_ref[g]
        for ci in range(nc):
            hin_ref[ci] = hcur
            hcur = hcur * cd_x[ci, 0:1, gcols] + st[ci]
        state_ref[g] = hcur
        y_off = _bdot(c3, hin_ref[...]).reshape(ts, gw) * ea_x[:, gcols]
        y = yd_ref[:, gcols] + y_off + dskip_ref[:, gcols] * xs[:, gcols]
        gated = y * _silu(z_ref[:, gcols].astype(F32))
        o_ref[:, gcols] = _rms(gated, nw_ref[:, gcols]).astype(o_ref.dtype)


def _ssd(h, hs, conv_w, conv_b, par, dskip, norm_w, bsz, seq):
    ts = min(SEQ_TILE, seq)
    ns = seq // ts
    nc = ts // CHUNK
    gw = SSM_INNER // SSM_GROUPS
    full = lambda shape: pl.BlockSpec(shape, lambda b, s: (0,) * len(shape))
    return pl.pallas_call(
        _ssd_body,
        grid=(bsz, ns),
        in_specs=[pl.BlockSpec((ts, HB), lambda b, s: (b * ns + s, COL_SZ)),
                  pl.BlockSpec((ts, 2 * HB), lambda b, s: (b * ns + s, COL_SXBC)),
                  pl.BlockSpec((ts, LANES), lambda b, s: (b * ns + s, 0)),
                  full((CONV_WIDTH, 2 * HB)), full((1, 2 * HB)), full((SUBLANES, LANES)),
                  full((1, SSM_INNER)), full((1, SSM_INNER))],
        out_specs=pl.BlockSpec((ts, HB), lambda b, s: (b * ns + s, 0)),
        out_shape=jax.ShapeDtypeStruct((bsz * seq, HB), BF16),
        scratch_shapes=[
            pltpu.VMEM((2, SUBLANES, HB), F32),
            pltpu.VMEM((ts + SUBLANES, HB), F32),
            pltpu.VMEM((ts, HB), F32), pltpu.VMEM((ts, HB), F32), pltpu.VMEM((ts, HB), F32),
            pltpu.VMEM((nc, SSM_STATE, gw), F32),
            pltpu.VMEM((SSM_GROUPS, SSM_STATE, gw), F32),
        ],
        compiler_params=_cparams("parallel", "arbitrary"),
        name="ssd",
    )(h, h, hs, conv_w, conv_b, par, dskip, norm_w)


def _rope(x, cos, sin_signed):
    width = x.shape[-1]
    half = DIFF_DQK // 2
    lane = lax.broadcasted_iota(jnp.int32, x.shape, x.ndim - 1)
    fwd = pltpu.roll(x, width - half, axis=x.ndim - 1)
    bwd = pltpu.roll(x, half, axis=x.ndim - 1)
    partner = jnp.where((lane & half) == 0, fwd, bwd)
    return x * cos + partner * sin_signed


def _diff_body(q_ref, k_ref, v_ref, cos_ref, sin_ref, lam_ref, nw_ref, o_ref, kr_ref, *, lambda_init, nq):
    qi = pl.program_id(2)
    tq = q_ref.shape[0]
    q0 = pl.multiple_of(qi * tq, tq)
    cos = cos_ref[...]
    sin = sin_ref[...]
    kr_ref[pl.ds(q0, tq), :] = _rope(k_ref[...].astype(F32), cos, sin).astype(BF16)
    qr = _rope(q_ref[...].astype(F32), cos, sin) * (DIFF_DQK ** -0.5)

    lp = lam_ref[...]
    prod01 = jnp.sum(lp[0:1, :] * lp[1:2, :], axis=-1, keepdims=True)
    prod23 = jnp.sum(lp[2:3, :] * lp[3:4, :], axis=-1, keepdims=True)
    lam = jnp.exp(prod01) - jnp.exp(prod23) + lambda_init

    lane = lax.broadcasted_iota(jnp.int32, (tq, 2 * DIFF_DQK), 1)
    qmaps = (jnp.where(lane < DIFF_DQK, qr, 0.0).astype(BF16), jnp.where(lane >= DIFF_DQK, qr, 0.0).astype(BF16))
    row = lax.broadcasted_iota(jnp.int32, (tq, tq), 0)
    col = lax.broadcasted_iota(jnp.int32, (tq, tq), 1)
    diag_ok = (row // CHUNK) >= (col // CHUNK)

    for i in range(nq):

        @pl.when(qi == i)
        def _(i=i):
            lo = i * tq
            outs = []
            for qm in qmaps:
                s_diag = jnp.where(diag_ok, _dot_nt(qm, kr_ref[lo:lo + tq, :]), -jnp.inf)
                mx = jnp.max(s_diag, axis=-1, keepdims=True)
                if i > 0:
                    s_off = _dot_nt(qm, kr_ref[0:lo, :])
                    mx = jnp.maximum(mx, jnp.max(s_off, axis=-1, keepdims=True))
                p_diag = jnp.exp(s_diag - mx)
                denom = jnp.sum(p_diag, axis=-1, keepdims=True)
                acc = _dot(p_diag, v_ref[lo:lo + tq, :])
                if i > 0:
                    p_off = jnp.exp(s_off - mx)
                    denom = denom + jnp.sum(p_off, axis=-1, keepdims=True)
                    acc = acc + _dot(p_off, v_ref[0:lo, :])
                outs.append(acc / denom)
            o = outs[0] - lam * outs[1]
            o_ref[...] = (_rms(o, nw_ref[...]) * (1.0 - lambda_init)).astype(o_ref.dtype)


def _diff_attention(h, cos, sin, lam_params, norm_w, lambda_init, bsz, seq):
    tq = min(SEQ_TILE, seq)
    nq = seq // tq
    per_head = HB // DIFF_DV
    full = lambda shape: pl.BlockSpec(shape, lambda b, hd, q: (0,) * len(shape))
    tok = lambda col: pl.BlockSpec((tq, DIFF_DV), lambda b, hd, q: (b * nq + q, col * per_head + hd))
    return pl.pallas_call(
        functools.partial(_diff_body, lambda_init=lambda_init, nq=nq),
        grid=(bsz, DIFF_HEADS, nq),
        in_specs=[tok(COL_DQ), tok(COL_DK),
                  pl.BlockSpec((seq, DIFF_DV), lambda b, hd, q: (b, COL_DV * per_head + hd)),
                  pl.BlockSpec((tq, DIFF_DV), lambda b, hd, q: (q, 0)),
                  pl.BlockSpec((tq, DIFF_DV), lambda b, hd, q: (q, 0)),
                  full((4, DIFF_DQK)), full((1, DIFF_DV))],
        out_specs=pl.BlockSpec((tq, DIFF_DV), lambda b, hd, q: (b * nq + q, hd)),
        out_shape=jax.ShapeDtypeStruct((bsz * seq, HB), BF16),
        scratch_shapes=[pltpu.VMEM((seq, DIFF_DV), BF16)],
        compiler_params=_cparams("parallel", "parallel", "arbitrary"),
        name="diff_attn",
    )(h, h, h, cos, sin, lam_params, norm_w)


def _mem_body(q_ref, kv_ref, o_ref):
    width = MEM_HEADS * MEM_HEADDIM
    for h in range(MEM_HEADS):
        cols = slice(h * MEM_HEADDIM, (h + 1) * MEM_HEADDIM)
        q = q_ref[:, cols].astype(F32) * (MEM_HEADDIM ** -0.5)
        s = _dot_nt(q, kv_ref[:, cols])
        m = jnp.max(s, axis=-1, keepdims=True)
        p = jnp.exp(s - m)
        p = p / jnp.sum(p, axis=-1, keepdims=True)
        o_ref[:, cols] = _dot(p, kv_ref[:, width + h * MEM_HEADDIM:width + (h + 1) * MEM_HEADDIM]).astype(o_ref.dtype)


def _mem_attention(h, kv, bsz, seq, mem_len):
    tm = min(ROW_TILE, seq)
    ns = seq // tm
    return pl.pallas_call(
        _mem_body,
        grid=(bsz, ns),
        in_specs=[pl.BlockSpec((tm, HB), lambda b, s: (b * ns + s, COL_MQ)),
                  pl.BlockSpec((mem_len, 2 * HB), lambda b, s: (b, 0))],
        out_specs=pl.BlockSpec((tm, HB), lambda b, s: (b * ns + s, 0)),
        out_shape=jax.ShapeDtypeStruct((bsz * seq, HB), BF16),
        compiler_params=_cparams("parallel", "parallel"),
        name="mem_attn",
    )(h, kv)


def _pack_halves(x):
    n = x.shape[1] // 2
    lo = pltpu.bitcast(x[:, :n].astype(BF16).astype(F32), U32)
    hi = pltpu.bitcast(x[:, n:].astype(BF16).astype(F32), U32)
    return (lo >> 16) | hi


def _unpack_halves(w):
    lo = pltpu.bitcast(w << 16, F32)
    hi = pltpu.bitcast(w & jnp.uint32(0xFFFF0000), F32)
    return lo.astype(BF16), hi.astype(BF16)


def _merge_body(g_ref, y0_ref, y1_ref, y2_ref, y3_ref, x_ref, wb_ref, wo_ref, lng_ref, lnb_ref,
                rwh_ref, rwl_ref, rb_ref, xo_ref, pa_ref, pb_ref, route_ref, cnt_ref):
    tm = x_ref.shape[0]
    merged = jnp.zeros((tm, D_MODEL), F32)
    for i, y_ref in enumerate((y0_ref, y1_ref, y2_ref, y3_ref)):
        gate = jax.nn.sigmoid(g_ref[:, i * D_MODEL:(i + 1) * D_MODEL].astype(F32))
        merged = merged + gate * jnp.dot(y_ref[...], wb_ref[i], preferred_element_type=F32)
    mix = _dot(merged, wo_ref[...])
    xn = _layer_norm(DEEPNORM_ALPHA * x_ref[...] + mix, lng_ref[...], lnb_ref[...])
    xo_ref[...] = xn
    pa_ref[...] = _pack_halves(xn[:, :D_MODEL // 2])
    pb_ref[...] = _pack_halves(xn[:, D_MODEL // 2:])

    xh, xl = _split2(xn)
    logits = (jnp.dot(xh, rwh_ref[...], preferred_element_type=F32)
              + (jnp.dot(xh, rwl_ref[...], preferred_element_type=F32)
                 + jnp.dot(xl, rwh_ref[...], preferred_element_type=F32))) + rb_ref[...]
    lane = lax.broadcasted_iota(jnp.int32, (tm, LANES), 1)
    work = jnp.where(lane < N_EXPERTS, logits, -jnp.inf)
    vals, ids, sels = [], [], []
    for _ in range(TOP_K):
        mx = jnp.max(work, axis=-1, keepdims=True)
        idx = jnp.min(jnp.where(work == mx, lane, LANES), axis=-1, keepdims=True)
        sel = lane == idx
        vals.append(mx)
        ids.append(idx)
        sels.append(sel)
        work = jnp.where(sel, -jnp.inf, work)
    exps = [jnp.exp(v - vals[0]) for v in vals]
    denom = exps[0] + exps[1] + exps[2] + exps[3]
    chosen = jnp.where(sels[0] | sels[1] | sels[2] | sels[3], 1.0, 0.0)
    r = lax.broadcasted_iota(jnp.int32, (tm, tm), 0)
    c = lax.broadcasted_iota(jnp.int32, (tm, tm), 1)
    ltri = jnp.where(r >= c, 1.0, 0.0).astype(BF16)
    cum = jnp.dot(ltri, chosen.astype(BF16), preferred_element_type=F32)
    out = jnp.zeros((tm, LANES), F32)
    for k in range(TOP_K):
        rank = jnp.sum(jnp.where(sels[k], cum, 0.0), axis=-1, keepdims=True) - 1.0
        out = jnp.where(lane == k, exps[k] / denom, out)
        out = jnp.where(lane == TOP_K + k, ids[k].astype(F32), out)
        out = jnp.where(lane == 2 * TOP_K + k, rank, out)
    route_ref[...] = out
    cnt_ref[0] = jnp.broadcast_to(cum[tm - 1:tm, :], (SUBLANES, LANES))


def _merge(h, ys, x, wb, wo, lng, lnb, rwh, rwl, rb):
    t = x.shape[0]
    tm = min(ROW_TILE, t)
    nt = t // tm
    tok = lambda width: pl.BlockSpec((tm, width), lambda i: (i, 0))
    full = lambda shape: pl.BlockSpec(shape, lambda i: (0,) * len(shape))
    return pl.pallas_call(
        _merge_body,
        grid=(nt,),
        in_specs=[tok(N_BRANCH * D_MODEL), tok(HB), tok(HB), tok(HB), tok(HB), tok(D_MODEL),
                  full((N_BRANCH, HB, D_MODEL)), full((D_MODEL, D_MODEL)), full((1, D_MODEL)), full((1, D_MODEL)),
                  full((D_MODEL, LANES)), full((D_MODEL, LANES)), full((1, LANES))],
        out_specs=[tok(D_MODEL), tok(SC_WORDS), tok(SC_WORDS), tok(LANES),
                   pl.BlockSpec((1, SUBLANES, LANES), lambda i: (i, 0, 0))],
        out_shape=[jax.ShapeDtypeStruct((t, D_MODEL), F32), jax.ShapeDtypeStruct((t, SC_WORDS), U32),
                   jax.ShapeDtypeStruct((t, SC_WORDS), U32), jax.ShapeDtypeStruct((t, LANES), F32),
                   jax.ShapeDtypeStruct((nt, SUBLANES, LANES), F32)],
        compiler_params=_cparams("parallel"),
        name="merge_ln_router",
    )(h, *ys, x, wb, wo, lng, lnb, rwh, rwl, rb)


def _sc_mesh():
    return plsc.VectorSubcoreMesh(core_axis_name="core", subcore_axis_name="subcore")


def _sc_dispatch(xa, xb, pos, n_out):
    t, words = xa.shape
    k = pos.shape[0]
    out = jax.ShapeDtypeStruct((n_out, words), xa.dtype)

    @functools.partial(pl.kernel, out_type=(out, out), mesh=_sc_mesh())
    def scatter_rows(xa_hbm, xb_hbm, i_hbm, oa_hbm, ob_hbm):
        for x_hbm, o_hbm in ((xa_hbm, oa_hbm), (xb_hbm, ob_hbm)):
            def body(x_vmem, i_vmem, o_hbm=o_hbm):
                for kk in range(k):
                    pltpu.sync_copy(x_vmem, o_hbm.at[i_vmem.at[kk]])

            pltpu.emit_pipeline(
                body, grid=(t // SC_WINDOW,),
                in_specs=[pl.BlockSpec((SC_WINDOW, words), lambda i: (i, 0)),
                          pl.BlockSpec((k, SC_WINDOW), lambda i: (0, i))],
                out_specs=[],
                core_axis_name=("core", "subcore"),
                dimension_semantics=(pltpu.PARALLEL,),
            )(x_hbm, i_hbm)

    return scatter_rows(xa, xb, pos)


def _sc_collect(ya, yb, pos_flat):
    n = pos_flat.shape[1]
    words = ya.shape[1]
    out = jax.ShapeDtypeStruct((n, words), ya.dtype)

    @functools.partial(pl.kernel, out_type=(out, out), mesh=_sc_mesh())
    def gather_rows(ya_hbm, yb_hbm, i_hbm, oa_hbm, ob_hbm):
        for y_hbm, o_hbm in ((ya_hbm, oa_hbm), (yb_hbm, ob_hbm)):
            def body(i_vmem, o_vmem, y_hbm=y_hbm):
                pltpu.sync_copy(y_hbm.at[i_vmem.at[0]], o_vmem)

            pltpu.emit_pipeline(
                body, grid=(n // SC_WINDOW,),
                in_specs=[pl.BlockSpec((1, SC_WINDOW), lambda i: (0, i))],
                out_specs=[pl.BlockSpec((SC_WINDOW, words), lambda i: (i, 0))],
                core_axis_name=("core", "subcore"),
                dimension_semantics=(pltpu.PARALLEL,),
            )(i_hbm, o_hbm)

    return gather_rows(ya, yb, pos_flat)


def _expert_body(te_ref, xa_ref, xb_ref, wgl_ref, bgl_ref, wd_ref, bd_ref, ya_ref, yb_ref):
    q = D_MODEL // 4
    x0, x1 = _unpack_halves(xa_ref[...])
    x2, x3 = _unpack_halves(xb_ref[...])
    gl = bgl_ref[0]
    for i, xp in enumerate((x0, x1, x2, x3)):
        gl = gl + jnp.dot(xp, wgl_ref[0, i * q:(i + 1) * q, :], preferred_element_type=F32)
    glu = jnp.minimum(gl[:, :D_FF], SWIGLU_LIMIT)
    lin = jnp.clip(gl[:, D_FF:], -SWIGLU_LIMIT, SWIGLU_LIMIT)
    act = (lin + 1.0) * glu * jax.nn.sigmoid(SWIGLU_ALPHA * glu)
    y = _dot(act, wd_ref[0]) + bd_ref[0]
    ya_ref[...] = _pack_halves(y[:, :D_MODEL // 2])
    yb_ref[...] = _pack_halves(y[:, D_MODEL // 2:])


def _expert_mlp(tile_expert, xa, xb, wgl, bgl, wd, bd):
    n = xa.shape[0]
    tr = EXPERT_TILE
    tok = pl.BlockSpec((tr, SC_WORDS), lambda i, te: (i, 0))
    per_expert = lambda shape: pl.BlockSpec((1,) + shape, lambda i, te: (te[i],) + (0,) * len(shape))
    out = jax.ShapeDtypeStruct((n, SC_WORDS), U32)
    return pl.pallas_call(
        _expert_body,
        grid_spec=pltpu.PrefetchScalarGridSpec(
            num_scalar_prefetch=1,
            grid=(n // tr,),
            in_specs=[tok, tok, per_expert((D_MODEL, 2 * D_FF)), per_expert((1, 2 * D_FF)),
                      per_expert((D_FF, D_MODEL)), per_expert((1, D_MODEL))],
            out_specs=[tok, tok],
        ),
        out_shape=[out, out],
        compiler_params=_cparams("arbitrary"),
        name="expert_mlp",
    )(tile_expert, xa, xb, wgl, bgl, wd, bd)


def _deinterleave_body(w_ref, p_ref, o_ref):
    o_ref[...] = jnp.dot(w_ref[...].astype(BF16), p_ref[...], preferred_element_type=F32).astype(o_ref.dtype)


def _deinterleave(w):
    rows, n2 = w.shape
    src = lax.broadcasted_iota(jnp.int32, (n2, n2), 0)
    dst = lax.broadcasted_iota(jnp.int32, (n2, n2), 1)
    perm = jnp.where(src == jnp.where(dst < n2 // 2, 2 * dst, 2 * (dst - n2 // 2) + 1), 1.0, 0.0).astype(BF16)
    tm = min(ROW_TILE, rows)
    return pl.pallas_call(
        _deinterleave_body,
        grid=(rows // tm,),
        in_specs=[pl.BlockSpec((tm, n2), lambda i: (i, 0)), pl.BlockSpec((n2, n2), lambda i: (0, 0))],
        out_specs=pl.BlockSpec((tm, n2), lambda i: (i, 0)),
        out_shape=jax.ShapeDtypeStruct((rows, n2), BF16),
        compiler_params=_cparams("parallel"),
        name="deinterleave",
    )(w, perm)


def _combine_body(ya_ref, yb_ref, route_ref, x_ref, lng_ref, lnb_ref, xo_ref, xb_ref):
    x = x_ref[...]
    route = route_ref[...]
    half = D_MODEL // 2
    q = D_MODEL // 4
    ffn_a = jnp.zeros((x.shape[0], half), F32)
    ffn_b = jnp.zeros((x.shape[0], half), F32)
    for k in range(TOP_K):
        wk = route[:, k:k + 1]
        a0, a1 = _unpack_halves(ya_ref[k])
        b0, b1 = _unpack_halves(yb_ref[k])
        ffn_a = ffn_a + wk * jnp.concatenate([a0.astype(F32), a1.astype(F32)], axis=-1)
        ffn_b = ffn_b + wk * jnp.concatenate([b0.astype(F32), b1.astype(F32)], axis=-1)
    del q
    ffn = jnp.concatenate([ffn_a, ffn_b], axis=-1)
    xn = _layer_norm(DEEPNORM_ALPHA * x + ffn, lng_ref[...], lnb_ref[...])
    xo_ref[...] = xn
    xb_ref[...] = xn.astype(BF16)


def _combine(ya, yb, route, x, lng, lnb):
    t = x.shape[0]
    tm = min(ROW_TILE, t)
    tok = lambda width: pl.BlockSpec((tm, width), lambda i: (i, 0))
    gathered = pl.BlockSpec((TOP_K, tm, SC_WORDS), lambda i: (0, i, 0))
    full = lambda shape: pl.BlockSpec(shape, lambda i: (0,) * len(shape))
    return pl.pallas_call(
        _combine_body,
        grid=(t // tm,),
        in_specs=[gathered, gathered, tok(LANES), tok(D_MODEL), full((1, D_MODEL)), full((1, D_MODEL))],
        out_specs=[tok(D_MODEL), tok(D_MODEL)],
        out_shape=[jax.ShapeDtypeStruct((t, D_MODEL), F32), jax.ShapeDtypeStruct((t, D_MODEL), BF16)],
        compiler_params=_cparams("parallel"),
        name="combine_ln",
    )(ya, yb, route, x, lng, lnb)


def _route_positions(route, counts):
    t = route.shape[0]
    nt = counts.shape[0]
    tm = t // nt
    ids = route[:, TOP_K:2 * TOP_K].astype(jnp.int32).reshape(nt, tm, TOP_K)
    rank = route[:, 2 * TOP_K:3 * TOP_K].astype(jnp.int32).reshape(nt, tm, TOP_K)
    cnt = counts[:, 0, :N_EXPERTS].astype(jnp.int32)
    before = jnp.cumsum(cnt, axis=0) - cnt
    total = jnp.sum(cnt, axis=0)
    padded = ((total + EXPERT_TILE - 1) // EXPERT_TILE) * EXPERT_TILE
    ends = jnp.cumsum(padded)
    offs = (ends - padded)[None, :] + before
    experts = jnp.arange(N_EXPERTS, dtype=jnp.int32)
    pos = rank + jnp.sum(jnp.where(ids[..., None] == experts, offs[:, None, None, :], 0), axis=-1)
    n_rows = TOP_K * t + N_EXPERTS * EXPERT_TILE
    tile_start = jnp.arange(n_rows // EXPERT_TILE, dtype=jnp.int32) * EXPERT_TILE
    tile_expert = jnp.minimum(jnp.sum((tile_start[:, None] >= ends[None, :]).astype(jnp.int32), axis=1), N_EXPERTS - 1)
    return pos.reshape(t, TOP_K).T, tile_expert, n_rows


def _rope_tables(seq):
    half = DIFF_DQK // 2
    inv_freq = ROPE_THETA ** (-jnp.arange(half, dtype=F32) / half)
    ang = jnp.arange(seq, dtype=F32)[:, None] * inv_freq[None, :]
    cos = jnp.cos(ang)
    sin = jnp.sin(ang)
    reps = DIFF_DV // DIFF_DQK
    cos_t = jnp.tile(jnp.concatenate([cos, cos], axis=-1), (1, reps))
    sin_t = jnp.tile(jnp.concatenate([-sin, sin], axis=-1), (1, reps))
    return cos_t, sin_t


def _lane_row(vec, offset):
    return jnp.zeros((LANES,), F32).at[offset:offset + vec.shape[0]].set(vec.astype(F32))


def _layer(l, x, xbf, kv_src, p, bsz, seq, cos, sin):
    t = bsz * seq
    w_in = p['w_in'][l]
    w_big = jnp.concatenate([w_in[:, 5648:9744], w_in[:, 0:2048], w_in[:, 2056:5128], w_in[:, 5136:5648]],
                            axis=1).astype(BF16)
    w_small = jnp.concatenate([w_in[:, 2048:2056], w_in[:, 5128:5136],
                               jnp.zeros((D_MODEL, LANES - 16), F32)], axis=1).astype(BF16)
    h = _matmul(xbf, w_big, BF16, min(1024, t), H_WIDTH // 4, "in_proj")
    hs = _matmul(xbf, w_small, F32, min(1024, t), LANES, "in_proj_small")

    gdn_par = jnp.zeros((SUBLANES, LANES), F32)
    gdn_par = gdn_par.at[0].set(_lane_row(p['gdn_a_log'][l], GDN_HEADS)).at[1].set(_lane_row(p['gdn_dt_bias'][l], GDN_HEADS))
    y_gdn = _gdn(h, hs, p['gdn_conv_w'][l], gdn_par, p['gdn_norm_w'][l][None, :], bsz, seq)

    lambda_init = 0.8 - 0.6 * math.exp(-0.3 * l)
    y_diff = _diff_attention(h, cos, sin, p['diff_lambda'][l], p['diff_norm_w'][l][None, :], lambda_init, bsz, seq)

    ssm_par = jnp.zeros((SUBLANES, LANES), F32)
    ssm_par = ssm_par.at[0].set(_lane_row(p['ssm_a_log'][l], SSM_HEADS)).at[1].set(_lane_row(p['ssm_dt_bias'][l], SSM_HEADS))
    dskip = jnp.repeat(p['ssm_d'][l], SSM_HEADDIM)[None, :]
    y_ssm = _ssd(h, hs, p['ssm_conv_w'][l], p['ssm_conv_b'][l][None, :], ssm_par, dskip,
                 p['ssm_norm_w'][l][None, :], bsz, seq)

    mem_len = kv_src.shape[0] // bsz
    kv = _matmul(kv_src, p['w_mem'][l].astype(BF16), BF16, min(1024, kv_src.shape[0]), HB, "mem_kv")
    y_mem = _mem_attention(h, kv, bsz, seq, mem_len)

    rw = jnp.concatenate([p['router_w'][l], jnp.zeros((D_MODEL, LANES - N_EXPERTS), F32)], axis=1)
    rwh = rw.astype(BF16)
    rwl = (rw - rwh.astype(F32)).astype(BF16)
    rb = _lane_row(p['router_b'][l], 0)[None, :]
    x1, xa, xb, route, counts = _merge(h, (y_gdn, y_diff, y_ssm, y_mem), x, p['w_branch'][l].astype(BF16),
                                       p['w_out'][l].astype(BF16), p['ln1_g'][l][None, :], p['ln1_b'][l][None, :],
                                       rwh, rwl, rb)

    pos, tile_expert, n_rows = _route_positions(route, counts)
    xsa, xsb = _sc_dispatch(xa, xb, pos, n_rows)
    bgu = p['b_gate_up'][l]
    bgl = jnp.concatenate([bgu[:, 0::2], bgu[:, 1::2]], axis=1)[:, None, :]
    ysa, ysb = _expert_mlp(tile_expert, xsa, xsb, p['w_gate_lin'][l], bgl, p['w_down'][l].astype(BF16),
                           p['b_down'][l][:, None, :])
    ga, gb = _sc_collect(ysa, ysb, pos.reshape(1, TOP_K * t))
    x2, x2bf = _combine(ga.reshape(TOP_K, t, SC_WORDS), gb.reshape(TOP_K, t, SC_WORDS), route, x1,
                        p['ln2_g'][l][None, :], p['ln2_b'][l][None, :])
    return x2, x2bf


def kernel(x, mem, w_in, gdn_conv_w, gdn_a_log, gdn_dt_bias, gdn_norm_w, diff_lambda, diff_norm_w, ssm_conv_w, ssm_conv_b, ssm_a_log, ssm_dt_bias, ssm_d, ssm_norm_w, w_mem, w_branch, w_out, ln1_g, ln1_b, router_w, router_b, w_gate_up, b_gate_up, w_down, b_down, ln2_g, ln2_b):
    p = dict(w_in=w_in, gdn_conv_w=gdn_conv_w, gdn_a_log=gdn_a_log, gdn_dt_bias=gdn_dt_bias, gdn_norm_w=gdn_norm_w,
             diff_lambda=diff_lambda, diff_norm_w=diff_norm_w, ssm_conv_w=ssm_conv_w, ssm_conv_b=ssm_conv_b,
             ssm_a_log=ssm_a_log, ssm_dt_bias=ssm_dt_bias, ssm_d=ssm_d, ssm_norm_w=ssm_norm_w, w_mem=w_mem,
             w_branch=w_branch, w_out=w_out, ln1_g=ln1_g, ln1_b=ln1_b, router_w=router_w, router_b=router_b,
             w_gate_up=w_gate_up, b_gate_up=b_gate_up, w_down=w_down, b_down=b_down, ln2_g=ln2_g, ln2_b=ln2_b)
    bsz, seq, d = x.shape
    t = bsz * seq
    depth, n_exp = w_gate_up.shape[0], w_gate_up.shape[1]
    p['w_gate_lin'] = _deinterleave(w_gate_up.reshape(depth * n_exp * d, 2 * D_FF)).reshape(depth, n_exp, d, 2 * D_FF)
    cos, sin = _rope_tables(seq)
    xf = x.reshape(t, d)
    xbf = xf.astype(BF16)
    kv_src = mem.reshape(bsz * mem.shape[1], d).astype(BF16)
    for l in range(w_in.shape[0]):
        xf, xbf = _layer(l, xf, xbf, kv_src, p, bsz, seq, cos, sin)
    return xf.reshape(bsz, seq, d)
```

```python
import functools
import math

import jax
import jax.numpy as jnp
from jax import lax
from jax.experimental import pallas as pl
from jax.experimental.pallas import tpu as pltpu
from jax.experimental.pallas import tpu_sc as plsc

F32 = jnp.float32
BF16 = jnp.bfloat16
U32 = jnp.uint32

D_MODEL = 1024
DEPTH = 4
CHUNK = 64
Q_BLOCK = 128
CONV_WIDTH = 4
ROPE_THETA = 10000.0
NORM_EPS = 1e-6
LN_EPS = 1e-5

GDN_HEADS = 4
GDN_DK = 128
GDN_DV = 128
DIFF_HEADS = 4
DIFF_DQK = 64
DIFF_DV = 128
SSM_HEADS = 8
SSM_HEADDIM = 64
SSM_GROUPS = 2
SSM_STATE = 128
SSM_INNER = 512
MEM_HEADS = 4
MEM_HEADDIM = 128
N_BRANCH = 4
N_EXPERTS = 32
TOP_K = 4
D_FF = 1024
SWIGLU_LIMIT = 7.0
SWIGLU_ALPHA = 1.702
DEEPNORM_ALPHA = (2.0 * DEPTH) ** 0.25

LANES = 128
SUBLANES = 8
HALO = 16
VMEM_LIMIT = 56 * 1024 * 1024

HB = 512
COL_GATES = 0
COL_GQ, COL_GK, COL_GV, COL_GZ = 8, 9, 10, 11
COL_DQ, COL_DK, COL_DV = 12, 13, 14
COL_SZ = 15
COL_SXBC = 8
COL_MQ = 18
H_WIDTH = 19 * HB

SEQ_TILE = 512
ROW_TILE = 512
EXPERT_TILE = 512
SC_WINDOW = 128
SC_WORDS = 256
DEINT_BLOCK = 2 * LANES


def _cparams(*sem):
    return pltpu.CompilerParams(dimension_semantics=sem, vmem_limit_bytes=VMEM_LIMIT)


def _dot(a, b):
    return jnp.dot(a.astype(BF16), b.astype(BF16), preferred_element_type=F32)


def _dot_nt(a, b):
    return lax.dot_general(a.astype(BF16), b.astype(BF16), (((1,), (1,)), ((), ())), preferred_element_type=F32)


def _bdot(a, b):
    return jnp.einsum('cmk,ckn->cmn', a.astype(BF16), b.astype(BF16), preferred_element_type=F32)


def _bdot_nt(a, b):
    return jnp.einsum('cmk,cnk->cmn', a.astype(BF16), b.astype(BF16), preferred_element_type=F32)


def _split2(a):
    hi = a.astype(BF16)
    lo = (a - hi.astype(F32)).astype(BF16)
    return hi, lo


def _silu(x):
    return x * jax.nn.sigmoid(x)


def _softplus(x):
    return jnp.maximum(x, 0.0) + jnp.log1p(jnp.exp(-jnp.abs(x)))


def _rms(x, w):
    return x * lax.rsqrt(jnp.mean(x * x, axis=-1, keepdims=True) + NORM_EPS) * w


def _layer_norm(x, g, b):
    mu = jnp.mean(x, axis=-1, keepdims=True)
    xc = x - mu
    var = jnp.mean(xc * xc, axis=-1, keepdims=True)
    return xc * lax.rsqrt(var + LN_EPS) * g + b


def _mm_body(x_ref, w_ref, o_ref):
    o_ref[...] = jnp.dot(x_ref[...], w_ref[...], preferred_element_type=F32).astype(o_ref.dtype)


def _matmul(x, w, out_dtype, tm, tn, name):
    m, k = x.shape
    n = w.shape[1]
    return pl.pallas_call(
        _mm_body,
        grid=(m // tm, n // tn),
        in_specs=[pl.BlockSpec((tm, k), lambda i, j: (i, 0)), pl.BlockSpec((k, tn), lambda i, j: (0, j))],
        out_specs=pl.BlockSpec((tm, tn), lambda i, j: (i, j)),
        out_shape=jax.ShapeDtypeStruct((m, n), out_dtype),
        compiler_params=_cparams("parallel", "parallel"),
        name=name,
    )(x, w)


def _in_proj_body(x_ref, w_ref, ws_ref, o_ref, os_ref):
    x = x_ref[...]
    o_ref[...] = jnp.dot(x, w_ref[...], preferred_element_type=F32).astype(o_ref.dtype)

    @pl.when(pl.program_id(1) == 0)
    def _():
        os_ref[...] = jnp.dot(x, ws_ref[...], preferred_element_type=F32)


def _in_proj(x, w, w_small, tm, tn):
    m, k = x.shape
    n = w.shape[1]
    return pl.pallas_call(
        _in_proj_body,
        grid=(m // tm, n // tn),
        in_specs=[pl.BlockSpec((tm, k), lambda i, j: (i, 0)), pl.BlockSpec((k, tn), lambda i, j: (0, j)),
                  pl.BlockSpec((k, LANES), lambda i, j: (0, 0))],
        out_specs=[pl.BlockSpec((tm, tn), lambda i, j: (i, j)), pl.BlockSpec((tm, LANES), lambda i, j: (i, 0))],
        out_shape=[jax.ShapeDtypeStruct((m, n), BF16), jax.ShapeDtypeStruct((m, LANES), F32)],
        compiler_params=_cparams("parallel", "arbitrary"),
        name="in_proj",
    )(x, w, w_small)


def _shift_taps(nc):
    m = lax.broadcasted_iota(jnp.int32, (1, (CONV_WIDTH - 1) * CHUNK, CHUNK + HALO), 1)
    k = lax.broadcasted_iota(jnp.int32, (1, (CONV_WIDTH - 1) * CHUNK, CHUNK + HALO), 2)
    src = (m % CHUNK) + HALO - (CONV_WIDTH - 1 - m // CHUNK)
    return jnp.broadcast_to(_mask(k == src).astype(BF16), (nc, (CONV_WIDTH - 1) * CHUNK, CHUNK + HALO))


def _conv_silu(x, tail_ref, slot, xbuf_ref, w, bias, taps):
    rows, width = x.shape
    nc = rows // CHUNK
    xbuf_ref[0:HALO, :] = tail_ref[slot]
    xbuf_ref[HALO:HALO + rows, :] = x
    tail_ref[slot] = x[rows - HALO:rows, :]
    xext = jnp.concatenate([xbuf_ref[i * CHUNK:(i + 1) * CHUNK + HALO, :][None] for i in range(nc)], axis=0)
    shifted = _bdot(taps, xext)
    y = x.astype(F32).reshape(nc, CHUNK, width) * w[CONV_WIDTH - 1:CONV_WIDTH, :]
    for j in range(CONV_WIDTH - 1):
        y = y + shifted[:, j * CHUNK:(j + 1) * CHUNK, :] * w[j:j + 1, :]
    y = y.reshape(rows, width)
    if bias is not None:
        y = y + bias
    return _silu(y)


def _chunk_iota():
    r = lax.broadcasted_iota(jnp.int32, (1, CHUNK, CHUNK), 1)
    c = lax.broadcasted_iota(jnp.int32, (1, CHUNK, CHUNK), 2)
    return r, c


def _mask(cond):
    return jnp.where(cond, 1.0, 0.0)


def _segment_logdecay(g3, r, c):
    nb = g3.shape[0]
    lincl = jnp.broadcast_to(_mask(r >= c).astype(BF16), (nb, CHUNK, CHUNK))
    upper = _mask(r > c)
    ghi = g3.astype(BF16).astype(F32)
    glo = g3 - ghi
    return _bdot(lincl, ghi * upper) + _bdot(lincl, glo * upper)


def _inv_unit_lower(a, r, c):
    d = _mask(r == c) - a * _mask(((r >> 1) == (c >> 1)) & (r > c))
    for lb in range(1, 6):
        lower_left = _mask(((r >> (lb + 1)) == (c >> (lb + 1))) & ((r >> lb) > (c >> lb)))
        d = d - _bdot(_bdot(d, a * lower_left), d)
    return d


def _gdn_body(q_ref, k_ref, v_ref, z_ref, hs_ref, cw_ref, par_ref, nw_ref, o_ref,
              tail_ref, xbuf_ref, qs_ref, ks_ref, vs_ref, pt_ref, bt_ref, qwt_ref, qut_ref, ot_ref, gl_ref,
              state_ref):
    ts = q_ref.shape[0]
    nc = ts // CHUNK

    @pl.when(pl.program_id(1) == 0)
    def _():
        tail_ref[...] = jnp.zeros_like(tail_ref)
        state_ref[...] = jnp.zeros_like(state_ref)

    taps = _shift_taps(nc)
    for slot, (src, dst) in enumerate(((q_ref, qs_ref), (k_ref, ks_ref), (v_ref, vs_ref))):
        w = cw_ref[:, slot * HB:(slot + 1) * HB]
        dst[...] = _conv_silu(src[...], tail_ref, slot, xbuf_ref, w, None, taps)

    hs = hs_ref[...]
    beta_all = jax.nn.sigmoid(hs)
    g_all = -jnp.exp(par_ref[0:1, :]) * _softplus(hs + par_ref[1:2, :])

    def stack(per_head):
        return jnp.concatenate([per_head(h) for h in range(GDN_HEADS)], axis=0)

    def l2(ref, h):
        x = ref[:, h * GDN_DK:(h + 1) * GDN_DK]
        return (x * lax.rsqrt(jnp.sum(x * x, axis=-1, keepdims=True) + NORM_EPS)).reshape(nc, CHUNK, GDN_DK)

    q3 = stack(lambda h: l2(qs_ref, h)) * (GDN_DK ** -0.5)
    k3 = stack(lambda h: l2(ks_ref, h))
    v3 = stack(lambda h: vs_ref[:, h * GDN_DV:(h + 1) * GDN_DV].reshape(nc, CHUNK, GDN_DV))
    b3 = stack(lambda h: beta_all[:, h:h + 1].reshape(nc, CHUNK, 1))
    g3 = stack(lambda h: g_all[:, GDN_HEADS + h:GDN_HEADS + h + 1].reshape(nc, CHUNK, 1))
    r, c = _chunk_iota()
    m = _segment_logdecay(g3, r, c)
    em = jnp.exp(m)
    a = _bdot_nt(k3, k3) * b3 * (em * _mask(r > c))
    t = _inv_unit_lower(a, r, c)
    gc = m[:, :, 0:1] + g3[:, 0:1, :]
    eg = jnp.exp(gc)
    g_last = gc[:, CHUNK - 1:CHUNK, :]
    sol = _bdot(t, jnp.concatenate([v3 * b3, k3 * (b3 * eg)], axis=-1))
    u = sol[:, :, :GDN_DV]
    w = sol[:, :, GDN_DV:]
    kd = k3 * jnp.exp(g_last - gc)
    qk = _bdot_nt(q3, k3) * (em * _mask(r >= c))
    pt = _bdot(jnp.swapaxes(w, 1, 2), kd).astype(BF16)
    bt = _bdot(jnp.swapaxes(u, 1, 2), kd)
    qwt = jnp.swapaxes(q3 * eg - _bdot(qk, w), 1, 2).astype(BF16)
    qut = jnp.swapaxes(_bdot(qk, u), 1, 2)
    gl = jnp.broadcast_to(jnp.exp(g_last), (GDN_HEADS * nc, SUBLANES, LANES))
    for h in range(GDN_HEADS):
        hb = slice(h * nc, (h + 1) * nc)
        pt_ref[h] = pt[hb]
        bt_ref[h] = bt[hb]
        qwt_ref[h] = qwt[hb]
        qut_ref[h] = qut[hb]
        gl_ref[h] = gl[hb]

    def chunk_step(ci, carry):
        for h in range(GDN_HEADS):
            st = state_ref[h]
            stb = st.astype(BF16)
            ot_ref[h, ci] = jnp.dot(stb, qwt_ref[h, ci], preferred_element_type=F32) + qut_ref[h, ci]
            state_ref[h] = (st * gl_ref[h, ci][0:1, :] - jnp.dot(stb, pt_ref[h, ci], preferred_element_type=F32)
                            + bt_ref[h, ci])
        return carry

    lax.fori_loop(0, nc, chunk_step, 0)

    for h in range(GDN_HEADS):
        sl = slice(h * GDN_DV, (h + 1) * GDN_DV)
        o = jnp.swapaxes(ot_ref[h], 1, 2).reshape(ts, GDN_DV)
        o_ref[:, sl] = (_rms(o, nw_ref[...]) * _silu(z_ref[:, sl].astype(F32))).astype(o_ref.dtype)


def _gdn(h, hs, conv_w, par, norm_w, bsz, seq):
    ts = min(SEQ_TILE, seq)
    ns = seq // ts
    tok = lambda col: pl.BlockSpec((ts, HB), lambda b, s: (b * ns + s, col))
    full = lambda shape: pl.BlockSpec(shape, lambda b, s: (0,) * len(shape))
    nc = ts // CHUNK
    return pl.pallas_call(
        _gdn_body,
        grid=(bsz, ns),
        in_specs=[tok(COL_GQ), tok(COL_GK), tok(COL_GV), tok(COL_GZ),
                  pl.BlockSpec((ts, LANES), lambda b, s: (b * ns + s, 0)),
                  full((CONV_WIDTH, 3 * HB)), full((SUBLANES, LANES)), full((1, GDN_DV))],
        out_specs=pl.BlockSpec((ts, HB), lambda b, s: (b * ns + s, 0)),
        out_shape=jax.ShapeDtypeStruct((bsz * seq, HB), BF16),
        scratch_shapes=[
            pltpu.VMEM((3, HALO, HB), BF16),
            pltpu.VMEM((ts + HALO, HB), BF16),
            pltpu.VMEM((ts, HB), F32), pltpu.VMEM((ts, HB), F32), pltpu.VMEM((ts, HB), F32),
            pltpu.VMEM((GDN_HEADS, nc, GDN_DK, GDN_DK), BF16), pltpu.VMEM((GDN_HEADS, nc, GDN_DV, GDN_DK), F32),
            pltpu.VMEM((GDN_HEADS, nc, GDN_DK, CHUNK), BF16), pltpu.VMEM((GDN_HEADS, nc, GDN_DV, CHUNK), F32),
            pltpu.VMEM((GDN_HEADS, nc, GDN_DV, CHUNK), F32),
            pltpu.VMEM((GDN_HEADS, nc, SUBLANES, LANES), F32),
            pltpu.VMEM((GDN_HEADS, GDN_DV, GDN_DK), F32),
        ],
        compiler_params=_cparams("parallel", "arbitrary"),
        name="gdn",
    )(h, h, h, h, hs, conv_w, par, norm_w)


def _expand_heads(x, expand_hi):
    hi = x.astype(BF16)
    r1 = x - hi.astype(F32)
    mid = r1.astype(BF16)
    lo = (r1 - mid.astype(F32)).astype(BF16)
    e = expand_hi
    return (jnp.dot(hi, e, preferred_element_type=F32) + jnp.dot(mid, e, preferred_element_type=F32)
            + jnp.dot(lo, e, preferred_element_type=F32))


def _ssd_body(z_ref, xbc_ref, hs_ref, cw_ref, cb_ref, par_ref, dskip_ref, nw_ref, o_ref,
              tail_ref, xbuf_ref, xs_ref, bc_ref, yd_ref, hin_ref, state_ref):
    ts = z_ref.shape[0]
    nc = ts // CHUNK
    gw = SSM_INNER // SSM_GROUPS

    @pl.when(pl.program_id(1) == 0)
    def _():
        tail_ref[...] = jnp.zeros_like(tail_ref)
        state_ref[...] = jnp.zeros_like(state_ref)

    taps = _shift_taps(nc)
    for slot, dst in enumerate((xs_ref, bc_ref)):
        cols = slice(slot * HB, (slot + 1) * HB)
        dst[...] = _conv_silu(xbc_ref[:, cols], tail_ref, slot, xbuf_ref, cw_ref[:, cols], cb_ref[:, cols], taps)

    hs = hs_ref[...]
    dt = _softplus(hs + par_ref[1:2, :])
    a_all = -jnp.exp(par_ref[0:1, :]) * dt
    r, c = _chunk_iota()
    tril = _mask(r >= c)
    lincl = jnp.broadcast_to(tril.astype(BF16), (nc, CHUNK, CHUNK))
    a3 = a_all.reshape(nc, CHUNK, LANES)
    ahi = a3.astype(BF16)
    ar = a3 - ahi.astype(F32)
    amid = ar.astype(BF16)
    alo = (ar - amid.astype(F32)).astype(BF16)
    acum = _bdot(lincl, ahi) + _bdot(lincl, amid) + _bdot(lincl, alo)
    a_last = acum[:, CHUNK - 1:CHUNK, :]
    lane = lax.broadcasted_iota(jnp.int32, (LANES, SSM_INNER), 0)
    chan = lax.broadcasted_iota(jnp.int32, (LANES, SSM_INNER), 1)
    expand = jnp.where(lane - SSM_HEADS == chan // SSM_HEADDIM, 1.0, 0.0).astype(BF16)
    dt_x = _expand_heads(dt, expand)
    ea_x = _expand_heads(jnp.exp(acum).reshape(ts, LANES), expand)
    din_x = _expand_heads(jnp.exp(a_last - acum).reshape(ts, LANES), expand)
    cd_x = _expand_heads(jnp.broadcast_to(jnp.exp(a_last), (nc, SUBLANES, LANES)).reshape(nc * SUBLANES, LANES),
                         expand).reshape(nc, SUBLANES, SSM_INNER)

    xs = xs_ref[...]
    xdt = xs * dt_x
    xw = xdt * din_x
    for g in range(SSM_GROUPS):
        gcols = slice(g * gw, (g + 1) * gw)
        b3 = bc_ref[:, g * SSM_STATE:(g + 1) * SSM_STATE].reshape(nc, CHUNK, SSM_STATE)
        c3 = bc_ref[:, gw + g * SSM_STATE:gw + (g + 1) * SSM_STATE].reshape(nc, CHUNK, SSM_STATE)
        cb = _bdot_nt(c3, b3)
        hpg = SSM_HEADS // SSM_GROUPS
        heads = [g * hpg + hr for hr in range(hpg)]
        ah3 = jnp.concatenate([a_all[:, SSM_HEADS + hd:SSM_HEADS + hd + 1].reshape(nc, CHUNK, 1) for hd in heads],
                              axis=0)
        x4 = jnp.concatenate([xdt[:, hd * SSM_HEADDIM:(hd + 1) * SSM_HEADDIM].reshape(nc, CHUNK, SSM_HEADDIM)
                              for hd in heads], axis=0)
        seg = jnp.exp(_segment_logdecay(ah3, r, c)) * tril
        yd = _bdot(seg * jnp.concatenate([cb] * hpg, axis=0), x4)
        for hr, hd in enumerate(heads):
            yd_ref[:, hd * SSM_HEADDIM:(hd + 1) * SSM_HEADDIM] = yd[hr * nc:(hr + 1) * nc].reshape(ts, SSM_HEADDIM)
        bt = jnp.swapaxes(b3, 1, 2)
        st = _bdot(bt, xw[:, gcols].reshape(nc, CHUNK, gw))
        hcur = state_ref[g]
        for ci in range(nc):
            hin_ref[ci] = hcur
            hcur = hcur * cd_x[ci, 0:1, gcols] + st[ci]
        state_ref[g] = hcur
        y_off = _bdot(c3, hin_ref[...]).reshape(ts, gw) * ea_x[:, gcols]
        y = yd_ref[:, gcols] + y_off + dskip_ref[:, gcols] * xs[:, gcols]
        gated = y * _silu(z_ref[:, gcols].astype(F32))
        o_ref[:, gcols] = _rms(gated, nw_ref[:, gcols]).astype(o_ref.dtype)


def _ssd(h, hs, conv_w, conv_b, par, dskip, norm_w, bsz, seq):
    ts = min(SEQ_TILE, seq)
    ns = seq // ts
    nc = ts // CHUNK
    gw = SSM_INNER // SSM_GROUPS
    full = lambda shape: pl.BlockSpec(shape, lambda b, s: (0,) * len(shape))
    return pl.pallas_call(
        _ssd_body,
        grid=(bsz, ns),
        in_specs=[pl.BlockSpec((ts, HB), lambda b, s: (b * ns + s, COL_SZ)),
                  pl.BlockSpec((ts, 2 * HB), lambda b, s: (b * ns + s, COL_SXBC)),
                  pl.BlockSpec((ts, LANES), lambda b, s: (b * ns + s, 0)),
                  full((CONV_WIDTH, 2 * HB)), full((1, 2 * HB)), full((SUBLANES, LANES)),
                  full((1, SSM_INNER)), full((1, SSM_INNER))],
        out_specs=pl.BlockSpec((ts, HB), lambda b, s: (b * ns + s, 0)),
        out_shape=jax.ShapeDtypeStruct((bsz * seq, HB), BF16),
        scratch_shapes=[
            pltpu.VMEM((2, HALO, HB), BF16),
            pltpu.VMEM((ts + HALO, HB), BF16),
            pltpu.VMEM((ts, HB), F32), pltpu.VMEM((ts, HB), F32), pltpu.VMEM((ts, HB), F32),
            pltpu.VMEM((nc, SSM_STATE, gw), F32),
            pltpu.VMEM((SSM_GROUPS, SSM_STATE, gw), F32),
        ],
        compiler_params=_cparams("parallel", "arbitrary"),
        name="ssd",
    )(h, h, hs, conv_w, conv_b, par, dskip, norm_w)


def _rope(x, cos, sin_signed):
    width = x.shape[-1]
    half = DIFF_DQK // 2
    lane = lax.broadcasted_iota(jnp.int32, x.shape, x.ndim - 1)
    fwd = pltpu.roll(x, width - half, axis=x.ndim - 1)
    bwd = pltpu.roll(x, half, axis=x.ndim - 1)
    partner = jnp.where((lane & half) == 0, fwd, bwd)
    return x * cos + partner * sin_signed


def _diff_body(q_ref, k_ref, v_ref, cos_ref, sin_ref, lam_ref, nw_ref, o_ref, kr_ref, *, lambda_init, nq):
    qi = pl.program_id(2)
    tq = q_ref.shape[0]
    q0 = pl.multiple_of(qi * tq, tq)
    cos = cos_ref[...]
    sin = sin_ref[...]
    kr_ref[pl.ds(q0, tq), :] = _rope(k_ref[...].astype(F32), cos, sin).astype(BF16)
    qr = _rope(q_ref[...].astype(F32), cos, sin) * (DIFF_DQK ** -0.5)

    lp = lam_ref[...]
    prod01 = jnp.sum(lp[0:1, :] * lp[1:2, :], axis=-1, keepdims=True)
    prod23 = jnp.sum(lp[2:3, :] * lp[3:4, :], axis=-1, keepdims=True)
    lam = jnp.exp(prod01) - jnp.exp(prod23) + lambda_init

    lane = lax.broadcasted_iota(jnp.int32, (tq, 2 * DIFF_DQK), 1)
    qmaps = (jnp.where(lane < DIFF_DQK, qr, 0.0).astype(BF16), jnp.where(lane >= DIFF_DQK, qr, 0.0).astype(BF16))
    row = lax.broadcasted_iota(jnp.int32, (tq, tq), 0)
    col = lax.broadcasted_iota(jnp.int32, (tq, tq), 1)
    diag_ok = (row // CHUNK) >= (col // CHUNK)

    for i in range(nq):

        @pl.when(qi == i)
        def _(i=i):
            lo = i * tq
            outs = []
            for qm in qmaps:
                s_diag = jnp.where(diag_ok, _dot_nt(qm, kr_ref[lo:lo + tq, :]), -jnp.inf)
                mx = jnp.max(s_diag, axis=-1, keepdims=True)
                if i > 0:
                    s_off = _dot_nt(qm, kr_ref[0:lo, :])
                    mx = jnp.maximum(mx, jnp.max(s_off, axis=-1, keepdims=True))
                p_diag = jnp.exp(s_diag - mx)
                denom = jnp.sum(p_diag, axis=-1, keepdims=True)
                acc = _dot(p_diag, v_ref[lo:lo + tq, :])
                if i > 0:
                    p_off = jnp.exp(s_off - mx)
                    denom = denom + jnp.sum(p_off, axis=-1, keepdims=True)
                    acc = acc + _dot(p_off, v_ref[0:lo, :])
                outs.append(acc / denom)
            o = outs[0] - lam * outs[1]
            o_ref[...] = (_rms(o, nw_ref[...]) * (1.0 - lambda_init)).astype(o_ref.dtype)


def _diff_attention(h, cos, sin, lam_params, norm_w, lambda_init, bsz, seq):
    tq = min(SEQ_TILE, seq)
    nq = seq // tq
    per_head = HB // DIFF_DV
    full = lambda shape: pl.BlockSpec(shape, lambda b, hd, q: (0,) * len(shape))
    tok = lambda col: pl.BlockSpec((tq, DIFF_DV), lambda b, hd, q: (b * nq + q, col * per_head + hd))
    return pl.pallas_call(
        functools.partial(_diff_body, lambda_init=lambda_init, nq=nq),
        grid=(bsz, DIFF_HEADS, nq),
        in_specs=[tok(COL_DQ), tok(COL_DK),
                  pl.BlockSpec((seq, DIFF_DV), lambda b, hd, q: (b, COL_DV * per_head + hd)),
                  pl.BlockSpec((tq, DIFF_DV), lambda b, hd, q: (q, 0)),
                  pl.BlockSpec((tq, DIFF_DV), lambda b, hd, q: (q, 0)),
                  full((4, DIFF_DQK)), full((1, DIFF_DV))],
        out_specs=pl.BlockSpec((tq, DIFF_DV), lambda b, hd, q: (b * nq + q, hd)),
        out_shape=jax.ShapeDtypeStruct((bsz * seq, HB), BF16),
        scratch_shapes=[pltpu.VMEM((seq, DIFF_DV), BF16)],
        compiler_params=_cparams("parallel", "parallel", "arbitrary"),
        name="diff_attn",
    )(h, h, h, cos, sin, lam_params, norm_w)


def _mem_body(q_ref, kv_ref, o_ref):
    width = MEM_HEADS * MEM_HEADDIM
    for h in range(MEM_HEADS):
        cols = slice(h * MEM_HEADDIM, (h + 1) * MEM_HEADDIM)
        q = q_ref[:, cols].astype(F32) * (MEM_HEADDIM ** -0.5)
        s = _dot_nt(q, kv_ref[:, cols])
        m = jnp.max(s, axis=-1, keepdims=True)
        p = jnp.exp(s - m)
        p = p / jnp.sum(p, axis=-1, keepdims=True)
        o_ref[:, cols] = _dot(p, kv_ref[:, width + h * MEM_HEADDIM:width + (h + 1) * MEM_HEADDIM]).astype(o_ref.dtype)


def _mem_attention(h, kv, bsz, seq, mem_len):
    tm = min(ROW_TILE, seq)
    ns = seq // tm
    return pl.pallas_call(
        _mem_body,
        grid=(bsz, ns),
        in_specs=[pl.BlockSpec((tm, HB), lambda b, s: (b * ns + s, COL_MQ)),
                  pl.BlockSpec((mem_len, 2 * HB), lambda b, s: (b, 0))],
        out_specs=pl.BlockSpec((tm, HB), lambda b, s: (b * ns + s, 0)),
        out_shape=jax.ShapeDtypeStruct((bsz * seq, HB), BF16),
        compiler_params=_cparams("parallel", "parallel"),
        name="mem_attn",
    )(h, kv)


def _pack_halves(x):
    n = x.shape[1] // 2
    lo = pltpu.bitcast(x[:, :n].astype(BF16).astype(F32), U32)
    hi = pltpu.bitcast(x[:, n:].astype(BF16).astype(F32), U32)
    return (lo >> 16) | hi


def _unpack_halves(w):
    lo = pltpu.bitcast(w << 16, F32)
    hi = pltpu.bitcast(w & jnp.uint32(0xFFFF0000), F32)
    return lo.astype(BF16), hi.astype(BF16)


def _merge_body(g_ref, y0_ref, y1_ref, y2_ref, y3_ref, x_ref, wb_ref, wo_ref, lng_ref, lnb_ref,
                rwh_ref, rwl_ref, rb_ref, xo_ref, pa_ref, pb_ref, route_ref, cnt_ref):
    tm = x_ref.shape[0]
    merged = jnp.zeros((tm, D_MODEL), F32)
    for i, y_ref in enumerate((y0_ref, y1_ref, y2_ref, y3_ref)):
        gate = jax.nn.sigmoid(g_ref[:, i * D_MODEL:(i + 1) * D_MODEL].astype(F32))
        merged = merged + gate * jnp.dot(y_ref[...], wb_ref[i], preferred_element_type=F32)
    mix = _dot(merged, wo_ref[...])
    xn = _layer_norm(DEEPNORM_ALPHA * x_ref[...] + mix, lng_ref[...], lnb_ref[...])
    xo_ref[...] = xn
    pa_ref[...] = _pack_halves(xn[:, :D_MODEL // 2])
    pb_ref[...] = _pack_halves(xn[:, D_MODEL // 2:])

    xh, xl = _split2(xn)
    logits = (jnp.dot(xh, rwh_ref[...], preferred_element_type=F32)
              + (jnp.dot(xh, rwl_ref[...], preferred_element_type=F32)
                 + jnp.dot(xl, rwh_ref[...], preferred_element_type=F32))) + rb_ref[...]
    lane = lax.broadcasted_iota(jnp.int32, (tm, LANES), 1)
    work = jnp.where(lane < N_EXPERTS, logits, -jnp.inf)
    vals, ids, sels = [], [], []
    for _ in range(TOP_K):
        mx = jnp.max(work, axis=-1, keepdims=True)
        idx = jnp.min(jnp.where(work == mx, lane, LANES), axis=-1, keepdims=True)
        sel = lane == idx
        vals.append(mx)
        ids.append(idx)
        sels.append(sel)
        work = jnp.where(sel, -jnp.inf, work)
    exps = [jnp.exp(v - vals[0]) for v in vals]
    denom = exps[0] + exps[1] + exps[2] + exps[3]
    chosen = jnp.where(sels[0] | sels[1] | sels[2] | sels[3], 1.0, 0.0)
    r = lax.broadcasted_iota(jnp.int32, (tm, tm), 0)
    c = lax.broadcasted_iota(jnp.int32, (tm, tm), 1)
    ltri = jnp.where(r >= c, 1.0, 0.0).astype(BF16)
    cum = jnp.dot(ltri, chosen.astype(BF16), preferred_element_type=F32)
    out = jnp.zeros((tm, LANES), F32)
    for k in range(TOP_K):
        rank = jnp.sum(jnp.where(sels[k], cum, 0.0), axis=-1, keepdims=True) - 1.0
        out = jnp.where(lane == k, exps[k] / denom, out)
        out = jnp.where(lane == TOP_K + k, ids[k].astype(F32), out)
        out = jnp.where(lane == 2 * TOP_K + k, rank, out)
    route_ref[...] = out
    cnt_ref[0] = jnp.broadcast_to(cum[tm - 1:tm, :], (SUBLANES, LANES))


def _merge(h, ys, x, wb, wo, lng, lnb, rwh, rwl, rb):
    t = x.shape[0]
    tm = min(ROW_TILE, t)
    nt = t // tm
    tok = lambda width: pl.BlockSpec((tm, width), lambda i: (i, 0))
    full = lambda shape: pl.BlockSpec(shape, lambda i: (0,) * len(shape))
    return pl.pallas_call(
        _merge_body,
        grid=(nt,),
        in_specs=[tok(N_BRANCH * D_MODEL), tok(HB), tok(HB), tok(HB), tok(HB), tok(D_MODEL),
                  full((N_BRANCH, HB, D_MODEL)), full((D_MODEL, D_MODEL)), full((1, D_MODEL)), full((1, D_MODEL)),
                  full((D_MODEL, LANES)), full((D_MODEL, LANES)), full((1, LANES))],
        out_specs=[tok(D_MODEL), tok(SC_WORDS), tok(SC_WORDS), tok(LANES),
                   pl.BlockSpec((1, SUBLANES, LANES), lambda i: (i, 0, 0))],
        out_shape=[jax.ShapeDtypeStruct((t, D_MODEL), F32), jax.ShapeDtypeStruct((t, SC_WORDS), U32),
                   jax.ShapeDtypeStruct((t, SC_WORDS), U32), jax.ShapeDtypeStruct((t, LANES), F32),
                   jax.ShapeDtypeStruct((nt, SUBLANES, LANES), F32)],
        compiler_params=_cparams("parallel"),
        name="merge_ln_router",
    )(h, *ys, x, wb, wo, lng, lnb, rwh, rwl, rb)


def _sc_mesh():
    return plsc.VectorSubcoreMesh(core_axis_name="core", subcore_axis_name="subcore")


def _sc_dispatch(xa, xb, pos, n_out):
    t, words = xa.shape
    k = pos.shape[0]
    out = jax.ShapeDtypeStruct((n_out, words), xa.dtype)

    @functools.partial(pl.kernel, out_type=(out, out), mesh=_sc_mesh())
    def scatter_rows(xa_hbm, xb_hbm, i_hbm, oa_hbm, ob_hbm):
        for x_hbm, o_hbm in ((xa_hbm, oa_hbm), (xb_hbm, ob_hbm)):
            def body(x_vmem, i_vmem, o_hbm=o_hbm):
                for kk in range(k):
                    pltpu.sync_copy(x_vmem, o_hbm.at[i_vmem.at[kk]])

            pltpu.emit_pipeline(
                body, grid=(t // SC_WINDOW,),
                in_specs=[pl.BlockSpec((SC_WINDOW, words), lambda i: (i, 0)),
                          pl.BlockSpec((k, SC_WINDOW), lambda i: (0, i))],
                out_specs=[],
                core_axis_name=("core", "subcore"),
                dimension_semantics=(pltpu.PARALLEL,),
            )(x_hbm, i_hbm)

    return scatter_rows(xa, xb, pos)


def _sc_collect(ya, yb, pos_flat):
    n = pos_flat.shape[1]
    words = ya.shape[1]
    out = jax.ShapeDtypeStruct((n, words), ya.dtype)

    @functools.partial(pl.kernel, out_type=(out, out), mesh=_sc_mesh())
    def gather_rows(ya_hbm, yb_hbm, i_hbm, oa_hbm, ob_hbm):
        for y_hbm, o_hbm in ((ya_hbm, oa_hbm), (yb_hbm, ob_hbm)):
            def body(i_vmem, o_vmem, y_hbm=y_hbm):
                pltpu.sync_copy(y_hbm.at[i_vmem.at[0]], o_vmem)

            pltpu.emit_pipeline(
                body, grid=(n // SC_WINDOW,),
                in_specs=[pl.BlockSpec((1, SC_WINDOW), lambda i: (0, i))],
                out_specs=[pl.BlockSpec((SC_WINDOW, words), lambda i: (i, 0))],
                core_axis_name=("core", "subcore"),
                dimension_semantics=(pltpu.PARALLEL,),
            )(i_hbm, o_hbm)

    return gather_rows(ya, yb, pos_flat)


def _expert_body(te_ref, xa_ref, xb_ref, wgl_ref, bgl_ref, wd_ref, bd_ref, ya_ref, yb_ref):
    x0, x1 = _unpack_halves(xa_ref[...])
    x2, x3 = _unpack_halves(xb_ref[...])
    x = jnp.concatenate([x0, x1, x2, x3], axis=1)
    gl = jnp.dot(x, wgl_ref[0], preferred_element_type=F32) + bgl_ref[0]
    nblk = 2 * D_FF // DEINT_BLOCK
    half = DEINT_BLOCK // 2
    glu = jnp.concatenate([gl[:, b * DEINT_BLOCK:b * DEINT_BLOCK + half] for b in range(nblk)], axis=1)
    lin = jnp.concatenate([gl[:, b * DEINT_BLOCK + half:(b + 1) * DEINT_BLOCK] for b in range(nblk)], axis=1)
    glu = jnp.minimum(glu, SWIGLU_LIMIT)
    lin = jnp.clip(lin, -SWIGLU_LIMIT, SWIGLU_LIMIT)
    act = (lin + 1.0) * glu * jax.nn.sigmoid(SWIGLU_ALPHA * glu)
    y = _dot(act, wd_ref[0]) + bd_ref[0]
    ya_ref[...] = _pack_halves(y[:, :D_MODEL // 2])
    yb_ref[...] = _pack_halves(y[:, D_MODEL // 2:])


def _expert_mlp(tile_expert, xa, xb, wgl, bgl, wd, bd):
    n = xa.shape[0]
    tr = EXPERT_TILE
    tok = pl.BlockSpec((tr, SC_WORDS), lambda i, te: (i, 0))
    per_expert = lambda shape: pl.BlockSpec((1,) + shape, lambda i, te: (te[i],) + (0,) * len(shape))
    out = jax.ShapeDtypeStruct((n, SC_WORDS), U32)
    return pl.pallas_call(
        _expert_body,
        grid_spec=pltpu.PrefetchScalarGridSpec(
            num_scalar_prefetch=1,
            grid=(n // tr,),
            in_specs=[tok, tok, per_expert((D_MODEL, 2 * D_FF)), per_expert((1, 2 * D_FF)),
                      per_expert((D_FF, D_MODEL)), per_expert((1, D_MODEL))],
            out_specs=[tok, tok],
        ),
        out_shape=[out, out],
        compiler_params=_cparams("arbitrary"),
        name="expert_mlp",
    )(tile_expert, xa, xb, wgl, bgl, wd, bd)


def _deinterleave_body(w_ref, p_ref, o_ref):
    for b in range(w_ref.shape[1] // DEINT_BLOCK):
        cols = slice(b * DEINT_BLOCK, (b + 1) * DEINT_BLOCK)
        o_ref[:, cols] = jnp.dot(w_ref[:, cols].astype(BF16), p_ref[...], preferred_element_type=F32).astype(o_ref.dtype)


def _deinterleave_perm():
    src = lax.broadcasted_iota(jnp.int32, (DEINT_BLOCK, DEINT_BLOCK), 0)
    dst = lax.broadcasted_iota(jnp.int32, (DEINT_BLOCK, DEINT_BLOCK), 1)
    half = DEINT_BLOCK // 2
    return jnp.where(src == jnp.where(dst < half, 2 * dst, 2 * (dst - half) + 1), 1.0, 0.0)


def _deinterleave(w):
    rows, n2 = w.shape
    tm = min(ROW_TILE, rows)
    return pl.pallas_call(
        _deinterleave_body,
        grid=(rows // tm,),
        in_specs=[pl.BlockSpec((tm, n2), lambda i: (i, 0)),
                  pl.BlockSpec((DEINT_BLOCK, DEINT_BLOCK), lambda i: (0, 0))],
        out_specs=pl.BlockSpec((tm, n2), lambda i: (i, 0)),
        out_shape=jax.ShapeDtypeStruct((rows, n2), BF16),
        compiler_params=_cparams("parallel"),
        name="deinterleave",
    )(w, _deinterleave_perm().astype(BF16))


def _combine_body(ya_ref, yb_ref, route_ref, x_ref, lng_ref, lnb_ref, xo_ref, xb_ref):
    x = x_ref[...]
    route = route_ref[...]
    half = D_MODEL // 2
    q = D_MODEL // 4
    ffn_a = jnp.zeros((x.shape[0], half), F32)
    ffn_b = jnp.zeros((x.shape[0], half), F32)
    for k in range(TOP_K):
        wk = route[:, k:k + 1]
        a0, a1 = _unpack_halves(ya_ref[k])
        b0, b1 = _unpack_halves(yb_ref[k])
        ffn_a = ffn_a + wk * jnp.concatenate([a0.astype(F32), a1.astype(F32)], axis=-1)
        ffn_b = ffn_b + wk * jnp.concatenate([b0.astype(F32), b1.astype(F32)], axis=-1)
    del q
    ffn = jnp.concatenate([ffn_a, ffn_b], axis=-1)
    xn = _layer_norm(DEEPNORM_ALPHA * x + ffn, lng_ref[...], lnb_ref[...])
    xo_ref[...] = xn
    xb_ref[...] = xn.astype(BF16)


def _combine(ya, yb, route, x, lng, lnb):
    t = x.shape[0]
    tm = min(ROW_TILE, t)
    tok = lambda width: pl.BlockSpec((tm, width), lambda i: (i, 0))
    gathered = pl.BlockSpec((TOP_K, tm, SC_WORDS), lambda i: (0, i, 0))
    full = lambda shape: pl.BlockSpec(shape, lambda i: (0,) * len(shape))
    return pl.pallas_call(
        _combine_body,
        grid=(t // tm,),
        in_specs=[gathered, gathered, tok(LANES), tok(D_MODEL), full((1, D_MODEL)), full((1, D_MODEL))],
        out_specs=[tok(D_MODEL), tok(D_MODEL)],
        out_shape=[jax.ShapeDtypeStruct((t, D_MODEL), F32), jax.ShapeDtypeStruct((t, D_MODEL), BF16)],
        compiler_params=_cparams("parallel"),
        name="combine_ln",
    )(ya, yb, route, x, lng, lnb)


def _route_positions(route, counts):
    t = route.shape[0]
    nt = counts.shape[0]
    tm = t // nt
    ids = route[:, TOP_K:2 * TOP_K].astype(jnp.int32).reshape(nt, tm, TOP_K)
    rank = route[:, 2 * TOP_K:3 * TOP_K].astype(jnp.int32).reshape(nt, tm, TOP_K)
    cnt = counts[:, 0, :N_EXPERTS].astype(jnp.int32)
    before = jnp.cumsum(cnt, axis=0) - cnt
    total = jnp.sum(cnt, axis=0)
    padded = ((total + EXPERT_TILE - 1) // EXPERT_TILE) * EXPERT_TILE
    ends = jnp.cumsum(padded)
    offs = (ends - padded)[None, :] + before
    experts = jnp.arange(N_EXPERTS, dtype=jnp.int32)
    pos = rank + jnp.sum(jnp.where(ids[..., None] == experts, offs[:, None, None, :], 0), axis=-1)
    n_rows = TOP_K * t + N_EXPERTS * EXPERT_TILE
    tile_start = jnp.arange(n_rows // EXPERT_TILE, dtype=jnp.int32) * EXPERT_TILE
    tile_expert = jnp.minimum(jnp.sum((tile_start[:, None] >= ends[None, :]).astype(jnp.int32), axis=1), N_EXPERTS - 1)
    return pos.reshape(t, TOP_K).T, tile_expert, n_rows


def _rope_tables(seq):
    half = DIFF_DQK // 2
    inv_freq = ROPE_THETA ** (-jnp.arange(half, dtype=F32) / half)
    ang = jnp.arange(seq, dtype=F32)[:, None] * inv_freq[None, :]
    cos = jnp.cos(ang)
    sin = jnp.sin(ang)
    reps = DIFF_DV // DIFF_DQK
    cos_t = jnp.tile(jnp.concatenate([cos, cos], axis=-1), (1, reps))
    sin_t = jnp.tile(jnp.concatenate([-sin, sin], axis=-1), (1, reps))
    return cos_t, sin_t


def _lane_row(vec, offset):
    return jnp.zeros((LANES,), F32).at[offset:offset + vec.shape[0]].set(vec.astype(F32))


def _layer(l, x, xbf, kv_src, p, bsz, seq, cos, sin):
    t = bsz * seq
    w_in = p['w_in'][l]
    w_big = jnp.concatenate([w_in[:, 5648:9744], w_in[:, 0:2048], w_in[:, 2056:5128], w_in[:, 5136:5648]],
                            axis=1).astype(BF16)
    w_small = jnp.concatenate([w_in[:, 2048:2056], w_in[:, 5128:5136],
                               jnp.zeros((D_MODEL, LANES - 16), F32)], axis=1).astype(BF16)
    h, hs = _in_proj(xbf, w_big, w_small, min(1024, t), H_WIDTH // 4)

    gdn_par = jnp.zeros((SUBLANES, LANES), F32)
    gdn_par = gdn_par.at[0].set(_lane_row(p['gdn_a_log'][l], GDN_HEADS)).at[1].set(_lane_row(p['gdn_dt_bias'][l], GDN_HEADS))
    y_gdn = _gdn(h, hs, p['gdn_conv_w'][l], gdn_par, p['gdn_norm_w'][l][None, :], bsz, seq)

    lambda_init = 0.8 - 0.6 * math.exp(-0.3 * l)
    y_diff = _diff_attention(h, cos, sin, p['diff_lambda'][l], p['diff_norm_w'][l][None, :], lambda_init, bsz, seq)

    ssm_par = jnp.zeros((SUBLANES, LANES), F32)
    ssm_par = ssm_par.at[0].set(_lane_row(p['ssm_a_log'][l], SSM_HEADS)).at[1].set(_lane_row(p['ssm_dt_bias'][l], SSM_HEADS))
    dskip = jnp.repeat(p['ssm_d'][l], SSM_HEADDIM)[None, :]
    y_ssm = _ssd(h, hs, p['ssm_conv_w'][l], p['ssm_conv_b'][l][None, :], ssm_par, dskip,
                 p['ssm_norm_w'][l][None, :], bsz, seq)

    mem_len = kv_src.shape[0] // bsz
    kv = _matmul(kv_src, p['w_mem'][l].astype(BF16), BF16, min(1024, kv_src.shape[0]), HB, "mem_kv")
    y_mem = _mem_attention(h, kv, bsz, seq, mem_len)

    rw = jnp.concatenate([p['router_w'][l], jnp.zeros((D_MODEL, LANES - N_EXPERTS), F32)], axis=1)
    rwh = rw.astype(BF16)
    rwl = (rw - rwh.astype(F32)).astype(BF16)
    rb = _lane_row(p['router_b'][l], 0)[None, :]
    x1, xa, xb, route, counts = _merge(h, (y_gdn, y_diff, y_ssm, y_mem), x, p['w_branch'][l].astype(BF16),
                                       p['w_out'][l].astype(BF16), p['ln1_g'][l][None, :], p['ln1_b'][l][None, :],
                                       rwh, rwl, rb)

    pos, tile_expert, n_rows = _route_positions(route, counts)
    xsa, xsb = _sc_dispatch(xa, xb, pos, n_rows)
    bgu = p['b_gate_up'][l]
    bgl = bgu.reshape(-1, 2 * D_FF // DEINT_BLOCK, DEINT_BLOCK // 2, 2).transpose(0, 1, 3, 2).reshape(-1, 1, 2 * D_FF)
    ysa, ysb = _expert_mlp(tile_expert, xsa, xsb, p['w_gate_lin'][l], bgl, p['w_down'][l].astype(BF16),
                           p['b_down'][l][:, None, :])
    ga, gb = _sc_collect(ysa, ysb, pos.reshape(1, TOP_K * t))
    x2, x2bf = _combine(ga.reshape(TOP_K, t, SC_WORDS), gb.reshape(TOP_K, t, SC_WORDS), route, x1,
                        p['ln2_g'][l][None, :], p['ln2_b'][l][None, :])
    return x2, x2bf


def kernel(x, mem, w_in, gdn_conv_w, gdn_a_log, gdn_dt_bias, gdn_norm_w, diff_lambda, diff_norm_w, ssm_conv_w, ssm_conv_b, ssm_a_log, ssm_dt_bias, ssm_d, ssm_norm_w, w_mem, w_branch, w_out, ln1_g, ln1_b, router_w, router_b, w_gate_up, b_gate_up, w_down, b_down, ln2_g, ln2_b):
    p = dict(w_in=w_in, gdn_conv_w=gdn_conv_w, gdn_a_log=gdn_a_log, gdn_dt_bias=gdn_dt_bias, gdn_norm_w=gdn_norm_w,
             diff_lambda=diff_lambda, diff_norm_w=diff_norm_w, ssm_conv_w=ssm_conv_w, ssm_conv_b=ssm_conv_b,
             ssm_a_log=ssm_a_log, ssm_dt_bias=ssm_dt_bias, ssm_d=ssm_d, ssm_norm_w=ssm_norm_w, w_mem=w_mem,
             w_branch=w_branch, w_out=w_out, ln1_g=ln1_g, ln1_b=ln1_b, router_w=router_w, router_b=router_b,
             w_gate_up=w_gate_up, b_gate_up=b_gate_up, w_down=w_down, b_down=b_down, ln2_g=ln2_g, ln2_b=ln2_b)
    bsz, seq, d = x.shape
    t = bsz * seq
    depth, n_exp = w_gate_up.shape[0], w_gate_up.shape[1]
    p['w_gate_lin'] = _deinterleave(w_gate_up.reshape(depth * n_exp * d, 2 * D_FF)).reshape(depth, n_exp, d, 2 * D_FF)
    cos, sin = _rope_tables(seq)
    xf = x.reshape(t, d)
    xbf = xf.astype(BF16)
    kv_src = mem.reshape(bsz * mem.shape[1], d).astype(BF16)
    for l in range(w_in.shape[0]):
        xf, xbf = _layer(l, xf, xbf, kv_src, p, bsz, seq, cos, sin)
    return xf.reshape(bsz, seq, d)
```

```python
import functools
import math

import jax
import jax.numpy as jnp
from jax import lax
from jax.experimental import pallas as pl
from jax.experimental.pallas import tpu as pltpu
from jax.experimental.pallas import tpu_sc as plsc

F32 = jnp.float32
BF16 = jnp.bfloat16
U32 = jnp.uint32

D_MODEL = 1024
DEPTH = 4
CHUNK = 64
Q_BLOCK = 128
CONV_WIDTH = 4
ROPE_THETA = 10000.0
NORM_EPS = 1e-6
LN_EPS = 1e-5

GDN_HEADS = 4
GDN_DK = 128
GDN_DV = 128
DIFF_HEADS = 4
DIFF_DQK = 64
DIFF_DV = 128
SSM_HEADS = 8
SSM_HEADDIM = 64
SSM_GROUPS = 2
SSM_STATE = 128
SSM_INNER = 512
MEM_HEADS = 4
MEM_HEADDIM = 128
N_BRANCH = 4
N_EXPERTS = 32
TOP_K = 4
D_FF = 1024
SWIGLU_LIMIT = 7.0
SWIGLU_ALPHA = 1.702
DEEPNORM_ALPHA = (2.0 * DEPTH) ** 0.25

LANES = 128
SUBLANES = 8
HALO = 16
VMEM_LIMIT = 56 * 1024 * 1024

HB = 512
COL_GATES = 0
COL_GQ, COL_GK, COL_GV, COL_GZ = 8, 9, 10, 11
COL_DQ, COL_DK, COL_DV = 12, 13, 14
COL_SZ = 15
COL_SXBC = 8
COL_MQ = 18
H_WIDTH = 19 * HB

SEQ_TILE = 512
ROW_TILE = 512
EXPERT_TILE = 512
SC_WINDOW = 128
SC_WORDS = 256
DEINT_BLOCK = 2 * LANES
TOKEN_GROUPS = 2


def _cparams(*sem):
    return pltpu.CompilerParams(dimension_semantics=sem, vmem_limit_bytes=VMEM_LIMIT)


def _dot(a, b):
    return jnp.dot(a.astype(BF16), b.astype(BF16), preferred_element_type=F32)


def _dot_nt(a, b):
    return lax.dot_general(a.astype(BF16), b.astype(BF16), (((1,), (1,)), ((), ())), preferred_element_type=F32)


def _bdot(a, b):
    return jnp.einsum('cmk,ckn->cmn', a.astype(BF16), b.astype(BF16), preferred_element_type=F32)


def _bdot_nt(a, b):
    return jnp.einsum('cmk,cnk->cmn', a.astype(BF16), b.astype(BF16), preferred_element_type=F32)


def _split2(a):
    hi = a.astype(BF16)
    lo = (a - hi.astype(F32)).astype(BF16)
    return hi, lo


def _sigmoid(x):
    return 0.5 * jnp.tanh(0.5 * x) + 0.5


def _silu(x):
    return x * _sigmoid(x)


def _softplus(x):
    return jnp.maximum(x, 0.0) + jnp.log1p(jnp.exp(-jnp.abs(x)))


def _rms(x, w):
    return x * lax.rsqrt(jnp.mean(x * x, axis=-1, keepdims=True) + NORM_EPS) * w


def _layer_norm(x, g, b):
    mu = jnp.mean(x, axis=-1, keepdims=True)
    xc = x - mu
    var = jnp.mean(xc * xc, axis=-1, keepdims=True)
    return xc * lax.rsqrt(var + LN_EPS) * g + b


def _mm_body(x_ref, w_ref, o_ref):
    o_ref[...] = jnp.dot(x_ref[...], w_ref[...], preferred_element_type=F32).astype(o_ref.dtype)


def _matmul(x, w, out_dtype, tm, tn, name):
    m, k = x.shape
    n = w.shape[1]
    return pl.pallas_call(
        _mm_body,
        grid=(m // tm, n // tn),
        in_specs=[pl.BlockSpec((tm, k), lambda i, j: (i, 0)), pl.BlockSpec((k, tn), lambda i, j: (0, j))],
        out_specs=pl.BlockSpec((tm, tn), lambda i, j: (i, j)),
        out_shape=jax.ShapeDtypeStruct((m, n), out_dtype),
        compiler_params=_cparams("parallel", "parallel"),
        name=name,
    )(x, w)


def _in_proj_body(x_ref, w_ref, ws_ref, o_ref, os_ref):
    x = x_ref[...]
    o_ref[...] = jnp.dot(x, w_ref[...], preferred_element_type=F32).astype(o_ref.dtype)

    @pl.when(pl.program_id(1) == 0)
    def _():
        os_ref[...] = jnp.dot(x, ws_ref[...], preferred_element_type=F32)


def _in_proj(x, w, w_small, tm, tn):
    m, k = x.shape
    n = w.shape[1]
    return pl.pallas_call(
        _in_proj_body,
        grid=(m // tm, n // tn),
        in_specs=[pl.BlockSpec((tm, k), lambda i, j: (i, 0)), pl.BlockSpec((k, tn), lambda i, j: (0, j)),
                  pl.BlockSpec((k, LANES), lambda i, j: (0, 0))],
        out_specs=[pl.BlockSpec((tm, tn), lambda i, j: (i, j)), pl.BlockSpec((tm, LANES), lambda i, j: (i, 0))],
        out_shape=[jax.ShapeDtypeStruct((m, n), BF16), jax.ShapeDtypeStruct((m, LANES), F32)],
        compiler_params=_cparams("parallel", "arbitrary"),
        name="in_proj",
    )(x, w, w_small)


def _shift_taps(nc):
    m = lax.broadcasted_iota(jnp.int32, (1, (CONV_WIDTH - 1) * CHUNK, CHUNK + HALO), 1)
    k = lax.broadcasted_iota(jnp.int32, (1, (CONV_WIDTH - 1) * CHUNK, CHUNK + HALO), 2)
    src = (m % CHUNK) + HALO - (CONV_WIDTH - 1 - m // CHUNK)
    return jnp.broadcast_to(_mask(k == src).astype(BF16), (nc, (CONV_WIDTH - 1) * CHUNK, CHUNK + HALO))


def _conv_silu(x, tail_ref, slot, xbuf_ref, w, bias, taps):
    rows, width = x.shape
    nc = rows // CHUNK
    xbuf_ref[0:HALO, :] = tail_ref[slot]
    xbuf_ref[HALO:HALO + rows, :] = x
    tail_ref[slot] = x[rows - HALO:rows, :]
    xext = jnp.concatenate([xbuf_ref[i * CHUNK:(i + 1) * CHUNK + HALO, :][None] for i in range(nc)], axis=0)
    shifted = _bdot(taps, xext)
    y = x.astype(F32).reshape(nc, CHUNK, width) * w[CONV_WIDTH - 1:CONV_WIDTH, :]
    for j in range(CONV_WIDTH - 1):
        y = y + shifted[:, j * CHUNK:(j + 1) * CHUNK, :] * w[j:j + 1, :]
    y = y.reshape(rows, width)
    if bias is not None:
        y = y + bias
    return _silu(y)


def _chunk_iota():
    r = lax.broadcasted_iota(jnp.int32, (1, CHUNK, CHUNK), 1)
    c = lax.broadcasted_iota(jnp.int32, (1, CHUNK, CHUNK), 2)
    return r, c


def _mask(cond):
    return jnp.where(cond, 1.0, 0.0)


def _segment_logdecay(g3, r, c):
    nb = g3.shape[0]
    lincl = jnp.broadcast_to(_mask(r >= c).astype(BF16), (nb, CHUNK, CHUNK))
    upper = _mask(r > c)
    ghi = g3.astype(BF16).astype(F32)
    glo = g3 - ghi
    return _bdot(lincl, ghi * upper) + _bdot(lincl, glo * upper)


def _inv_unit_lower(a, r, c):
    d = _mask(r == c) - a * _mask(((r >> 1) == (c >> 1)) & (r > c))
    for lb in range(1, 6):
        lower_left = _mask(((r >> (lb + 1)) == (c >> (lb + 1))) & ((r >> lb) > (c >> lb)))
        d = d - _bdot(_bdot(d, a * lower_left), d)
    return d


def _gdn_body(q_ref, k_ref, v_ref, z_ref, hs_ref, cw_ref, par_ref, nw_ref, o_ref,
              tail_ref, xbuf_ref, qs_ref, ks_ref, vs_ref, pt_ref, bt_ref, qwt_ref, qut_ref, ot_ref, gl_ref,
              state_ref):
    ts = q_ref.shape[0]
    nc = ts // CHUNK

    @pl.when(pl.program_id(1) == 0)
    def _():
        tail_ref[...] = jnp.zeros_like(tail_ref)
        state_ref[...] = jnp.zeros_like(state_ref)

    taps = _shift_taps(nc)
    for slot, (src, dst) in enumerate(((q_ref, qs_ref), (k_ref, ks_ref), (v_ref, vs_ref))):
        w = cw_ref[:, slot * HB:(slot + 1) * HB]
        dst[...] = _conv_silu(src[...], tail_ref, slot, xbuf_ref, w, None, taps)

    hs = hs_ref[...]
    beta_all = _sigmoid(hs)
    g_all = -jnp.exp(par_ref[0:1, :]) * _softplus(hs + par_ref[1:2, :])

    def stack(per_head):
        return jnp.concatenate([per_head(h) for h in range(GDN_HEADS)], axis=0)

    def l2(ref, h):
        x = ref[:, h * GDN_DK:(h + 1) * GDN_DK]
        return (x * lax.rsqrt(jnp.sum(x * x, axis=-1, keepdims=True) + NORM_EPS)).reshape(nc, CHUNK, GDN_DK)

    q3 = stack(lambda h: l2(qs_ref, h)) * (GDN_DK ** -0.5)
    k3 = stack(lambda h: l2(ks_ref, h))
    v3 = stack(lambda h: vs_ref[:, h * GDN_DV:(h + 1) * GDN_DV].reshape(nc, CHUNK, GDN_DV))
    b3 = stack(lambda h: beta_all[:, h:h + 1].reshape(nc, CHUNK, 1))
    g3 = stack(lambda h: g_all[:, GDN_HEADS + h:GDN_HEADS + h + 1].reshape(nc, CHUNK, 1))
    r, c = _chunk_iota()
    m = _segment_logdecay(g3, r, c)
    em = jnp.exp(m)
    a = _bdot_nt(k3, k3) * b3 * (em * _mask(r > c))
    t = _inv_unit_lower(a, r, c)
    gc = m[:, :, 0:1] + g3[:, 0:1, :]
    eg = jnp.exp(gc)
    g_last = gc[:, CHUNK - 1:CHUNK, :]
    sol = _bdot(t, jnp.concatenate([v3 * b3, k3 * (b3 * eg)], axis=-1))
    u = sol[:, :, :GDN_DV]
    w = sol[:, :, GDN_DV:]
    kd = k3 * jnp.exp(g_last - gc)
    qk = _bdot_nt(q3, k3) * (em * _mask(r >= c))
    pt = _bdot(jnp.swapaxes(w, 1, 2), kd).astype(BF16)
    bt = _bdot(jnp.swapaxes(u, 1, 2), kd)
    qwt = jnp.swapaxes(q3 * eg - _bdot(qk, w), 1, 2).astype(BF16)
    qut = jnp.swapaxes(_bdot(qk, u), 1, 2)
    gl = jnp.broadcast_to(jnp.exp(g_last), (GDN_HEADS * nc, SUBLANES, LANES))
    for h in range(GDN_HEADS):
        hb = slice(h * nc, (h + 1) * nc)
        pt_ref[h] = pt[hb]
        bt_ref[h] = bt[hb]
        qwt_ref[h] = qwt[hb]
        qut_ref[h] = qut[hb]
        gl_ref[h] = gl[hb]

    def chunk_step(ci, carry):
        for h in range(GDN_HEADS):
            st = state_ref[h]
            stb = st.astype(BF16)
            ot_ref[h, ci] = jnp.dot(stb, qwt_ref[h, ci], preferred_element_type=F32) + qut_ref[h, ci]
            state_ref[h] = (st * gl_ref[h, ci][0:1, :] - jnp.dot(stb, pt_ref[h, ci], preferred_element_type=F32)
                            + bt_ref[h, ci])
        return carry

    lax.fori_loop(0, nc, chunk_step, 0)

    for h in range(GDN_HEADS):
        sl = slice(h * GDN_DV, (h + 1) * GDN_DV)
        o = jnp.swapaxes(ot_ref[h], 1, 2).reshape(ts, GDN_DV)
        o_ref[:, sl] = (_rms(o, nw_ref[...]) * _silu(z_ref[:, sl].astype(F32))).astype(o_ref.dtype)


def _gdn(h, hs, conv_w, par, norm_w, bsz, seq):
    ts = min(SEQ_TILE, seq)
    ns = seq // ts
    tok = lambda col: pl.BlockSpec((ts, HB), lambda b, s: (b * ns + s, col))
    full = lambda shape: pl.BlockSpec(shape, lambda b, s: (0,) * len(shape))
    nc = ts // CHUNK
    return pl.pallas_call(
        _gdn_body,
        grid=(bsz, ns),
        in_specs=[tok(COL_GQ), tok(COL_GK), tok(COL_GV), tok(COL_GZ),
                  pl.BlockSpec((ts, LANES), lambda b, s: (b * ns + s, 0)),
                  full((CONV_WIDTH, 3 * HB)), full((SUBLANES, LANES)), full((1, GDN_DV))],
        out_specs=pl.BlockSpec((ts, HB), lambda b, s: (b * ns + s, 0)),
        out_shape=jax.ShapeDtypeStruct((bsz * seq, HB), BF16),
        scratch_shapes=[
            pltpu.VMEM((3, HALO, HB), BF16),
            pltpu.VMEM((ts + HALO, HB), BF16),
            pltpu.VMEM((ts, HB), F32), pltpu.VMEM((ts, HB), F32), pltpu.VMEM((ts, HB), F32),
            pltpu.VMEM((GDN_HEADS, nc, GDN_DK, GDN_DK), BF16), pltpu.VMEM((GDN_HEADS, nc, GDN_DV, GDN_DK), F32),
            pltpu.VMEM((GDN_HEADS, nc, GDN_DK, CHUNK), BF16), pltpu.VMEM((GDN_HEADS, nc, GDN_DV, CHUNK), F32),
            pltpu.VMEM((GDN_HEADS, nc, GDN_DV, CHUNK), F32),
            pltpu.VMEM((GDN_HEADS, nc, SUBLANES, LANES), F32),
            pltpu.VMEM((GDN_HEADS, GDN_DV, GDN_DK), F32),
        ],
        compiler_params=_cparams("parallel", "arbitrary"),
        name="gdn",
    )(h, h, h, h, hs, conv_w, par, norm_w)


def _expand_heads(x, expand_hi):
    hi = x.astype(BF16)
    r1 = x - hi.astype(F32)
    mid = r1.astype(BF16)
    lo = (r1 - mid.astype(F32)).astype(BF16)
    e = expand_hi
    return (jnp.dot(hi, e, preferred_element_type=F32) + jnp.dot(mid, e, preferred_element_type=F32)
            + jnp.dot(lo, e, preferred_element_type=F32))


def _ssd_body(z_ref, xbc_ref, hs_ref, cw_ref, cb_ref, par_ref, dskip_ref, nw_ref, o_ref,
              tail_ref, xbuf_ref, xs_ref, bc_ref, yd_ref, hin_ref, state_ref):
    ts = z_ref.shape[0]
    nc = ts // CHUNK
    gw = SSM_INNER // SSM_GROUPS

    @pl.when(pl.program_id(1) == 0)
    def _():
        tail_ref[...] = jnp.zeros_like(tail_ref)
        state_ref[...] = jnp.zeros_like(state_ref)

    taps = _shift_taps(nc)
    for slot, dst in enumerate((xs_ref, bc_ref)):
        cols = slice(slot * HB, (slot + 1) * HB)
        dst[...] = _conv_silu(xbc_ref[:, cols], tail_ref, slot, xbuf_ref, cw_ref[:, cols], cb_ref[:, cols], taps)

    hs = hs_ref[...]
    dt = _softplus(hs + par_ref[1:2, :])
    a_all = -jnp.exp(par_ref[0:1, :]) * dt
    r, c = _chunk_iota()
    tril = _mask(r >= c)
    lincl = jnp.broadcast_to(tril.astype(BF16), (nc, CHUNK, CHUNK))
    a3 = a_all.reshape(nc, CHUNK, LANES)
    ahi = a3.astype(BF16)
    ar = a3 - ahi.astype(F32)
    amid = ar.astype(BF16)
    alo = (ar - amid.astype(F32)).astype(BF16)
    acum = _bdot(lincl, ahi) + _bdot(lincl, amid) + _bdot(lincl, alo)
    a_last = acum[:, CHUNK - 1:CHUNK, :]
    lane = lax.broadcasted_iota(jnp.int32, (LANES, SSM_INNER), 0)
    chan = lax.broadcasted_iota(jnp.int32, (LANES, SSM_INNER), 1)
    expand = jnp.where(lane - SSM_HEADS == chan // SSM_HEADDIM, 1.0, 0.0).astype(BF16)
    dt_x = _expand_heads(dt, expand)
    ea_x = _expand_heads(jnp.exp(acum).reshape(ts, LANES), expand)
    din_x = _expand_heads(jnp.exp(a_last - acum).reshape(ts, LANES), expand)
    cd_x = _expand_heads(jnp.broadcast_to(jnp.exp(a_last), (nc, SUBLANES, LANES)).reshape(nc * SUBLANES, LANES),
                         expand).reshape(nc, SUBLANES, SSM_INNER)

    xs = xs_ref[...]
    xdt = xs * dt_x
    xw = xdt * din_x
    for g in range(SSM_GROUPS):
        gcols = slice(g * gw, (g + 1) * gw)
        b3 = bc_ref[:, g * SSM_STATE:(g + 1) * SSM_STATE].reshape(nc, CHUNK, SSM_STATE)
        c3 = bc_ref[:, gw + g * SSM_STATE:gw + (g + 1) * SSM_STATE].reshape(nc, CHUNK, SSM_STATE)
        cb = _bdot_nt(c3, b3)
        hpg = SSM_HEADS // SSM_GROUPS
        heads = [g * hpg + hr for hr in range(hpg)]
        ah3 = jnp.concatenate([a_all[:, SSM_HEADS + hd:SSM_HEADS + hd + 1].reshape(nc, CHUNK, 1) for hd in heads],
                              axis=0)
        x4 = jnp.concatenate([xdt[:, hd * SSM_HEADDIM:(hd + 1) * SSM_HEADDIM].reshape(nc, CHUNK, SSM_HEADDIM)
                              for hd in heads], axis=0)
        seg = jnp.exp(_segment_logdecay(ah3, r, c)) * tril
        yd = _bdot(seg * jnp.concatenate([cb] * hpg, axis=0), x4)
        for hr, hd in enumerate(heads):
            yd_ref[:, hd * SSM_HEADDIM:(hd + 1) * SSM_HEADDIM] = yd[hr * nc:(hr + 1) * nc].reshape(ts, SSM_HEADDIM)
        bt = jnp.swapaxes(b3, 1, 2)
        st = _bdot(bt, xw[:, gcols].reshape(nc, CHUNK, gw))
        hcur = state_ref[g]
        for ci in range(nc):
            hin_ref[ci] = hcur
            hcur = hcur * cd_x[ci, 0:1, gcols] + st[ci]
        state_ref[g] = hcur
        y_off = _bdot(c3, hin_ref[...]).reshape(ts, gw) * ea_x[:, gcols]
        y = yd_ref[:, gcols] + y_off + dskip_ref[:, gcols] * xs[:, gcols]
        gated = y * _silu(z_ref[:, gcols].astype(F32))
        o_ref[:, gcols] = _rms(gated, nw_ref[:, gcols]).astype(o_ref.dtype)


def _ssd(h, hs, conv_w, conv_b, par, dskip, norm_w, bsz, seq):
    ts = min(SEQ_TILE, seq)
    ns = seq // ts
    nc = ts // CHUNK
    gw = SSM_INNER // SSM_GROUPS
    full = lambda shape: pl.BlockSpec(shape, lambda b, s: (0,) * len(shape))
    return pl.pallas_call(
        _ssd_body,
        grid=(bsz, ns),
        in_specs=[pl.BlockSpec((ts, HB), lambda b, s: (b * ns + s, COL_SZ)),
                  pl.BlockSpec((ts, 2 * HB), lambda b, s: (b * ns + s, COL_SXBC)),
                  pl.BlockSpec((ts, LANES), lambda b, s: (b * ns + s, 0)),
                  full((CONV_WIDTH, 2 * HB)), full((1, 2 * HB)), full((SUBLANES, LANES)),
                  full((1, SSM_INNER)), full((1, SSM_INNER))],
        out_specs=pl.BlockSpec((ts, HB), lambda b, s: (b * ns + s, 0)),
        out_shape=jax.ShapeDtypeStruct((bsz * seq, HB), BF16),
        scratch_shapes=[
            pltpu.VMEM((2, HALO, HB), BF16),
            pltpu.VMEM((ts + HALO, HB), BF16),
            pltpu.VMEM((ts, HB), F32), pltpu.VMEM((ts, HB), F32), pltpu.VMEM((ts, HB), F32),
            pltpu.VMEM((nc, SSM_STATE, gw), F32),
            pltpu.VMEM((SSM_GROUPS, SSM_STATE, gw), F32),
        ],
        compiler_params=_cparams("parallel", "arbitrary"),
        name="ssd",
    )(h, h, hs, conv_w, conv_b, par, dskip, norm_w)


def _rope(x, cos, sin_signed):
    width = x.shape[-1]
    half = DIFF_DQK // 2
    lane = lax.broadcasted_iota(jnp.int32, x.shape, x.ndim - 1)
    fwd = pltpu.roll(x, width - half, axis=x.ndim - 1)
    bwd = pltpu.roll(x, half, axis=x.ndim - 1)
    partner = jnp.where((lane & half) == 0, fwd, bwd)
    return x * cos + partner * sin_signed


def _diff_body(q_ref, k_ref, v_ref, cos_ref, sin_ref, lam_ref, nw_ref, o_ref, kr_ref, *, lambda_init, nq):
    qi = pl.program_id(2)
    tq = q_ref.shape[0]
    q0 = pl.multiple_of(qi * tq, tq)
    cos = cos_ref[...]
    sin = sin_ref[...]
    kr_ref[pl.ds(q0, tq), :] = _rope(k_ref[...].astype(F32), cos, sin).astype(BF16)
    qr = _rope(q_ref[...].astype(F32), cos, sin) * (DIFF_DQK ** -0.5)

    lp = lam_ref[...]
    prod01 = jnp.sum(lp[0:1, :] * lp[1:2, :], axis=-1, keepdims=True)
    prod23 = jnp.sum(lp[2:3, :] * lp[3:4, :], axis=-1, keepdims=True)
    lam = jnp.exp(prod01) - jnp.exp(prod23) + lambda_init

    lane = lax.broadcasted_iota(jnp.int32, (tq, 2 * DIFF_DQK), 1)
    qmaps = (jnp.where(lane < DIFF_DQK, qr, 0.0).astype(BF16), jnp.where(lane >= DIFF_DQK, qr, 0.0).astype(BF16))
    row = lax.broadcasted_iota(jnp.int32, (tq, tq), 0)
    col = lax.broadcasted_iota(jnp.int32, (tq, tq), 1)
    diag_ok = (row // CHUNK) >= (col // CHUNK)

    for i in range(nq):

        @pl.when(qi == i)
        def _(i=i):
            lo = i * tq
            outs = []
            for qm in qmaps:
                s_diag = jnp.where(diag_ok, _dot_nt(qm, kr_ref[lo:lo + tq, :]), -jnp.inf)
                mx = jnp.max(s_diag, axis=-1, keepdims=True)
                if i > 0:
                    s_off = _dot_nt(qm, kr_ref[0:lo, :])
                    mx = jnp.maximum(mx, jnp.max(s_off, axis=-1, keepdims=True))
                p_diag = jnp.exp(s_diag - mx)
                denom = jnp.sum(p_diag, axis=-1, keepdims=True)
                acc = _dot(p_diag, v_ref[lo:lo + tq, :])
                if i > 0:
                    p_off = jnp.exp(s_off - mx)
                    denom = denom + jnp.sum(p_off, axis=-1, keepdims=True)
                    acc = acc + _dot(p_off, v_ref[0:lo, :])
                outs.append(acc / denom)
            o = outs[0] - lam * outs[1]
            o_ref[...] = (_rms(o, nw_ref[...]) * (1.0 - lambda_init)).astype(o_ref.dtype)


def _diff_attention(h, cos, sin, lam_params, norm_w, lambda_init, bsz, seq):
    tq = min(SEQ_TILE, seq)
    nq = seq // tq
    per_head = HB // DIFF_DV
    full = lambda shape: pl.BlockSpec(shape, lambda b, hd, q: (0,) * len(shape))
    tok = lambda col: pl.BlockSpec((tq, DIFF_DV), lambda b, hd, q: (b * nq + q, col * per_head + hd))
    return pl.pallas_call(
        functools.partial(_diff_body, lambda_init=lambda_init, nq=nq),
        grid=(bsz, DIFF_HEADS, nq),
        in_specs=[tok(COL_DQ), tok(COL_DK),
                  pl.BlockSpec((seq, DIFF_DV), lambda b, hd, q: (b, COL_DV * per_head + hd)),
                  pl.BlockSpec((tq, DIFF_DV), lambda b, hd, q: (q, 0)),
                  pl.BlockSpec((tq, DIFF_DV), lambda b, hd, q: (q, 0)),
                  full((4, DIFF_DQK)), full((1, DIFF_DV))],
        out_specs=pl.BlockSpec((tq, DIFF_DV), lambda b, hd, q: (b * nq + q, hd)),
        out_shape=jax.ShapeDtypeStruct((bsz * seq, HB), BF16),
        scratch_shapes=[pltpu.VMEM((seq, DIFF_DV), BF16)],
        compiler_params=_cparams("parallel", "parallel", "arbitrary"),
        name="diff_attn",
    )(h, h, h, cos, sin, lam_params, norm_w)


def _mem_body(q_ref, kv_ref, o_ref):
    width = MEM_HEADS * MEM_HEADDIM
    for h in range(MEM_HEADS):
        cols = slice(h * MEM_HEADDIM, (h + 1) * MEM_HEADDIM)
        q = q_ref[:, cols].astype(F32) * (MEM_HEADDIM ** -0.5)
        s = _dot_nt(q, kv_ref[:, cols])
        m = jnp.max(s, axis=-1, keepdims=True)
        p = jnp.exp(s - m)
        p = p / jnp.sum(p, axis=-1, keepdims=True)
        o_ref[:, cols] = _dot(p, kv_ref[:, width + h * MEM_HEADDIM:width + (h + 1) * MEM_HEADDIM]).astype(o_ref.dtype)


def _mem_attention(h, kv, bsz, seq, mem_len):
    tm = min(ROW_TILE, seq)
    ns = seq // tm
    return pl.pallas_call(
        _mem_body,
        grid=(bsz, ns),
        in_specs=[pl.BlockSpec((tm, HB), lambda b, s: (b * ns + s, COL_MQ)),
                  pl.BlockSpec((mem_len, 2 * HB), lambda b, s: (b, 0))],
        out_specs=pl.BlockSpec((tm, HB), lambda b, s: (b * ns + s, 0)),
        out_shape=jax.ShapeDtypeStruct((bsz * seq, HB), BF16),
        compiler_params=_cparams("parallel", "parallel"),
        name="mem_attn",
    )(h, kv)


def _pack_halves(x):
    n = x.shape[1] // 2
    lo = pltpu.bitcast(x[:, :n].astype(BF16).astype(F32), U32)
    hi = pltpu.bitcast(x[:, n:].astype(BF16).astype(F32), U32)
    return (lo >> 16) | hi


def _unpack_halves(w):
    lo = pltpu.bitcast(w << 16, F32)
    hi = pltpu.bitcast(w & jnp.uint32(0xFFFF0000), F32)
    return lo.astype(BF16), hi.astype(BF16)


def _merge_body(g_ref, y0_ref, y1_ref, y2_ref, y3_ref, x_ref, wb_ref, wo_ref, lng_ref, lnb_ref,
                rwh_ref, rwl_ref, rb_ref, xo_ref, pa_ref, pb_ref, route_ref, cnt_ref):
    tm = x_ref.shape[0]
    merged = jnp.zeros((tm, D_MODEL), F32)
    for i, y_ref in enumerate((y0_ref, y1_ref, y2_ref, y3_ref)):
        gate = _sigmoid(g_ref[:, i * D_MODEL:(i + 1) * D_MODEL].astype(F32))
        merged = merged + gate * jnp.dot(y_ref[...], wb_ref[i], preferred_element_type=F32)
    mix = _dot(merged, wo_ref[...])
    xn = _layer_norm(DEEPNORM_ALPHA * x_ref[...] + mix, lng_ref[...], lnb_ref[...])
    xo_ref[...] = xn
    pa_ref[...] = _pack_halves(xn[:, :D_MODEL // 2])
    pb_ref[...] = _pack_halves(xn[:, D_MODEL // 2:])

    xh, xl = _split2(xn)
    logits = (jnp.dot(xh, rwh_ref[...], preferred_element_type=F32)
              + (jnp.dot(xh, rwl_ref[...], preferred_element_type=F32)
                 + jnp.dot(xl, rwh_ref[...], preferred_element_type=F32))) + rb_ref[...]
    lane = lax.broadcasted_iota(jnp.int32, (tm, LANES), 1)
    work = jnp.where(lane < N_EXPERTS, logits, -jnp.inf)
    vals, ids, sels = [], [], []
    for _ in range(TOP_K):
        mx = jnp.max(work, axis=-1, keepdims=True)
        idx = jnp.min(jnp.where(work == mx, lane, LANES), axis=-1, keepdims=True)
        sel = lane == idx
        vals.append(mx)
        ids.append(idx)
        sels.append(sel)
        work = jnp.where(sel, -jnp.inf, work)
    exps = [jnp.exp(v - vals[0]) for v in vals]
    denom = exps[0] + exps[1] + exps[2] + exps[3]
    chosen = jnp.where(sels[0] | sels[1] | sels[2] | sels[3], 1.0, 0.0)
    r = lax.broadcasted_iota(jnp.int32, (tm, tm), 0)
    c = lax.broadcasted_iota(jnp.int32, (tm, tm), 1)
    ltri = jnp.where(r >= c, 1.0, 0.0).astype(BF16)
    cum = jnp.dot(ltri, chosen.astype(BF16), preferred_element_type=F32)
    out = jnp.zeros((tm, LANES), F32)
    for k in range(TOP_K):
        rank = jnp.sum(jnp.where(sels[k], cum, 0.0), axis=-1, keepdims=True) - 1.0
        out = jnp.where(lane == k, exps[k] / denom, out)
        out = jnp.where(lane == TOP_K + k, ids[k].astype(F32), out)
        out = jnp.where(lane == 2 * TOP_K + k, rank, out)
    route_ref[...] = out
    cnt_ref[0] = jnp.broadcast_to(cum[tm - 1:tm, :], (SUBLANES, LANES))


def _merge(h, ys, x, wb, wo, lng, lnb, rwh, rwl, rb):
    t = x.shape[0]
    tm = min(ROW_TILE, t)
    nt = t // tm
    tok = lambda width: pl.BlockSpec((tm, width), lambda i: (i, 0))
    full = lambda shape: pl.BlockSpec(shape, lambda i: (0,) * len(shape))
    return pl.pallas_call(
        _merge_body,
        grid=(nt,),
        in_specs=[tok(N_BRANCH * D_MODEL), tok(HB), tok(HB), tok(HB), tok(HB), tok(D_MODEL),
                  full((N_BRANCH, HB, D_MODEL)), full((D_MODEL, D_MODEL)), full((1, D_MODEL)), full((1, D_MODEL)),
                  full((D_MODEL, LANES)), full((D_MODEL, LANES)), full((1, LANES))],
        out_specs=[tok(D_MODEL), tok(SC_WORDS), tok(SC_WORDS), tok(LANES),
                   pl.BlockSpec((1, SUBLANES, LANES), lambda i: (i, 0, 0))],
        out_shape=[jax.ShapeDtypeStruct((t, D_MODEL), F32), jax.ShapeDtypeStruct((t, SC_WORDS), U32),
                   jax.ShapeDtypeStruct((t, SC_WORDS), U32), jax.ShapeDtypeStruct((t, LANES), F32),
                   jax.ShapeDtypeStruct((nt, SUBLANES, LANES), F32)],
        compiler_params=_cparams("parallel"),
        name="merge_ln_router",
    )(h, *ys, x, wb, wo, lng, lnb, rwh, rwl, rb)


def _sc_mesh():
    return plsc.VectorSubcoreMesh(core_axis_name="core", subcore_axis_name="subcore")


def _sc_dispatch(xa, xb, pos, n_out):
    t, words = xa.shape
    k = pos.shape[0]
    out = jax.ShapeDtypeStruct((n_out, words), xa.dtype)

    @functools.partial(pl.kernel, out_type=(out, out), mesh=_sc_mesh())
    def scatter_rows(xa_hbm, xb_hbm, i_hbm, oa_hbm, ob_hbm):
        for x_hbm, o_hbm in ((xa_hbm, oa_hbm), (xb_hbm, ob_hbm)):
            def body(x_vmem, i_vmem, o_hbm=o_hbm):
                for kk in range(k):
                    pltpu.sync_copy(x_vmem, o_hbm.at[i_vmem.at[kk]])

            pltpu.emit_pipeline(
                body, grid=(t // SC_WINDOW,),
                in_specs=[pl.BlockSpec((SC_WINDOW, words), lambda i: (i, 0)),
                          pl.BlockSpec((k, SC_WINDOW), lambda i: (0, i))],
                out_specs=[],
                core_axis_name=("core", "subcore"),
                dimension_semantics=(pltpu.PARALLEL,),
            )(x_hbm, i_hbm)

    return scatter_rows(xa, xb, pos)


def _sc_collect(ya, yb, pos_flat):
    n = pos_flat.shape[1]
    words = ya.shape[1]
    out = jax.ShapeDtypeStruct((n, words), ya.dtype)

    @functools.partial(pl.kernel, out_type=(out, out), mesh=_sc_mesh())
    def gather_rows(ya_hbm, yb_hbm, i_hbm, oa_hbm, ob_hbm):
        for y_hbm, o_hbm in ((ya_hbm, oa_hbm), (yb_hbm, ob_hbm)):
            def body(i_vmem, o_vmem, y_hbm=y_hbm):
                pltpu.sync_copy(y_hbm.at[i_vmem.at[0]], o_vmem)

            pltpu.emit_pipeline(
                body, grid=(n // SC_WINDOW,),
                in_specs=[pl.BlockSpec((1, SC_WINDOW), lambda i: (0, i))],
                out_specs=[pl.BlockSpec((SC_WINDOW, words), lambda i: (i, 0))],
                core_axis_name=("core", "subcore"),
                dimension_semantics=(pltpu.PARALLEL,),
            )(i_hbm, o_hbm)

    return gather_rows(ya, yb, pos_flat)


def _expert_body(te_ref, xa_ref, xb_ref, wgl_ref, bgl_ref, wd_ref, bd_ref, ya_ref, yb_ref):
    pl.when(pl.program_id(0) < te_ref[pl.num_programs(0)])(
        functools.partial(_expert_tile, xa_ref, xb_ref, wgl_ref, bgl_ref, wd_ref, bd_ref, ya_ref, yb_ref))


def _expert_tile(xa_ref, xb_ref, wgl_ref, bgl_ref, wd_ref, bd_ref, ya_ref, yb_ref):
    x0, x1 = _unpack_halves(xa_ref[...])
    x2, x3 = _unpack_halves(xb_ref[...])
    x = jnp.concatenate([x0, x1, x2, x3], axis=1)
    gl = jnp.dot(x, wgl_ref[0], preferred_element_type=F32) + bgl_ref[0]
    nblk = 2 * D_FF // DEINT_BLOCK
    half = DEINT_BLOCK // 2
    glu = jnp.concatenate([gl[:, b * DEINT_BLOCK:b * DEINT_BLOCK + half] for b in range(nblk)], axis=1)
    lin = jnp.concatenate([gl[:, b * DEINT_BLOCK + half:(b + 1) * DEINT_BLOCK] for b in range(nblk)], axis=1)
    glu = jnp.minimum(glu, SWIGLU_LIMIT)
    lin = jnp.clip(lin, -SWIGLU_LIMIT, SWIGLU_LIMIT)
    act = (lin + 1.0) * glu * _sigmoid(SWIGLU_ALPHA * glu)
    y = _dot(act, wd_ref[0]) + bd_ref[0]
    ya_ref[...] = _pack_halves(y[:, :D_MODEL // 2])
    yb_ref[...] = _pack_halves(y[:, D_MODEL // 2:])


def _expert_mlp(tile_expert, xa, xb, wgl, bgl, wd, bd):
    n = xa.shape[0]
    tr = EXPERT_TILE
    tok = pl.BlockSpec((tr, SC_WORDS), lambda i, te: (i, 0))
    per_expert = lambda shape: pl.BlockSpec((1,) + shape, lambda i, te: (te[i],) + (0,) * len(shape))
    out = jax.ShapeDtypeStruct((n, SC_WORDS), U32)
    return pl.pallas_call(
        _expert_body,
        grid_spec=pltpu.PrefetchScalarGridSpec(
            num_scalar_prefetch=1,
            grid=(n // tr,),
            in_specs=[tok, tok, per_expert((D_MODEL, 2 * D_FF)), per_expert((1, 2 * D_FF)),
                      per_expert((D_FF, D_MODEL)), per_expert((1, D_MODEL))],
            out_specs=[tok, tok],
        ),
        out_shape=[out, out],
        compiler_params=_cparams("arbitrary"),
        name="expert_mlp",
    )(tile_expert, xa, xb, wgl, bgl, wd, bd)


def _deinterleave_body(w_ref, p_ref, o_ref):
    for b in range(w_ref.shape[1] // DEINT_BLOCK):
        cols = slice(b * DEINT_BLOCK, (b + 1) * DEINT_BLOCK)
        o_ref[:, cols] = jnp.dot(w_ref[:, cols].astype(BF16), p_ref[...], preferred_element_type=F32).astype(o_ref.dtype)


def _deinterleave_perm():
    src = lax.broadcasted_iota(jnp.int32, (DEINT_BLOCK, DEINT_BLOCK), 0)
    dst = lax.broadcasted_iota(jnp.int32, (DEINT_BLOCK, DEINT_BLOCK), 1)
    half = DEINT_BLOCK // 2
    return jnp.where(src == jnp.where(dst < half, 2 * dst, 2 * (dst - half) + 1), 1.0, 0.0)


def _deinterleave(w):
    rows, n2 = w.shape
    tm = min(ROW_TILE, rows)
    return pl.pallas_call(
        _deinterleave_body,
        grid=(rows // tm,),
        in_specs=[pl.BlockSpec((tm, n2), lambda i: (i, 0)),
                  pl.BlockSpec((DEINT_BLOCK, DEINT_BLOCK), lambda i: (0, 0))],
        out_specs=pl.BlockSpec((tm, n2), lambda i: (i, 0)),
        out_shape=jax.ShapeDtypeStruct((rows, n2), BF16),
        compiler_params=_cparams("parallel"),
        name="deinterleave",
    )(w, _deinterleave_perm().astype(BF16))


def _combine_body(ya_ref, yb_ref, route_ref, x_ref, lng_ref, lnb_ref, xo_ref, xb_ref):
    x = x_ref[...]
    route = route_ref[...]
    half = D_MODEL // 2
    q = D_MODEL // 4
    ffn_a = jnp.zeros((x.shape[0], half), F32)
    ffn_b = jnp.zeros((x.shape[0], half), F32)
    for k in range(TOP_K):
        wk = route[:, k:k + 1]
        a0, a1 = _unpack_halves(ya_ref[k])
        b0, b1 = _unpack_halves(yb_ref[k])
        ffn_a = ffn_a + wk * jnp.concatenate([a0.astype(F32), a1.astype(F32)], axis=-1)
        ffn_b = ffn_b + wk * jnp.concatenate([b0.astype(F32), b1.astype(F32)], axis=-1)
    del q
    ffn = jnp.concatenate([ffn_a, ffn_b], axis=-1)
    xn = _layer_norm(DEEPNORM_ALPHA * x + ffn, lng_ref[...], lnb_ref[...])
    xo_ref[...] = xn
    xb_ref[...] = xn.astype(BF16)


def _combine(ya, yb, route, x, lng, lnb):
    t = x.shape[0]
    tm = min(ROW_TILE, t)
    tok = lambda width: pl.BlockSpec((tm, width), lambda i: (i, 0))
    gathered = pl.BlockSpec((TOP_K, tm, SC_WORDS), lambda i: (0, i, 0))
    full = lambda shape: pl.BlockSpec(shape, lambda i: (0,) * len(shape))
    return pl.pallas_call(
        _combine_body,
        grid=(t // tm,),
        in_specs=[gathered, gathered, tok(LANES), tok(D_MODEL), full((1, D_MODEL)), full((1, D_MODEL))],
        out_specs=[tok(D_MODEL), tok(D_MODEL)],
        out_shape=[jax.ShapeDtypeStruct((t, D_MODEL), F32), jax.ShapeDtypeStruct((t, D_MODEL), BF16)],
        compiler_params=_cparams("parallel"),
        name="combine_ln",
    )(ya, yb, route, x, lng, lnb)


def _route_positions(route, counts):
    t = route.shape[0]
    nt = counts.shape[0]
    tm = t // nt
    ids = route[:, TOP_K:2 * TOP_K].astype(jnp.int32).reshape(nt, tm, TOP_K)
    rank = route[:, 2 * TOP_K:3 * TOP_K].astype(jnp.int32).reshape(nt, tm, TOP_K)
    cnt = counts[:, 0, :N_EXPERTS].astype(jnp.int32)
    before = jnp.cumsum(cnt, axis=0) - cnt
    total = jnp.sum(cnt, axis=0)
    padded = ((total + EXPERT_TILE - 1) // EXPERT_TILE) * EXPERT_TILE
    ends = jnp.cumsum(padded)
    offs = (ends - padded)[None, :] + before
    experts = jnp.arange(N_EXPERTS, dtype=jnp.int32)
    pos = rank + jnp.sum(jnp.where(ids[..., None] == experts, offs[:, None, None, :], 0), axis=-1)
    n_rows = TOP_K * t + N_EXPERTS * EXPERT_TILE
    tile_start = jnp.arange(n_rows // EXPERT_TILE, dtype=jnp.int32) * EXPERT_TILE
    tile_expert = jnp.minimum(jnp.sum((tile_start[:, None] >= ends[None, :]).astype(jnp.int32), axis=1), N_EXPERTS - 1)
    tiles_used = ends[N_EXPERTS - 1:] // EXPERT_TILE
    return pos.reshape(t, TOP_K).T, jnp.concatenate([tile_expert, tiles_used]), n_rows


def _rope_tables(seq):
    half = DIFF_DQK // 2
    inv_freq = ROPE_THETA ** (-jnp.arange(half, dtype=F32) / half)
    ang = jnp.arange(seq, dtype=F32)[:, None] * inv_freq[None, :]
    cos = jnp.cos(ang)
    sin = jnp.sin(ang)
    reps = DIFF_DV // DIFF_DQK
    cos_t = jnp.tile(jnp.concatenate([cos, cos], axis=-1), (1, reps))
    sin_t = jnp.tile(jnp.concatenate([-sin, sin], axis=-1), (1, reps))
    return cos_t, sin_t


def _lane_row(vec, offset):
    return jnp.zeros((LANES,), F32).at[offset:offset + vec.shape[0]].set(vec.astype(F32))


def _layer_weights(l, p):
    w_in = p['w_in'][l]
    w_big = jnp.concatenate([w_in[:, 5648:9744], w_in[:, 0:2048], w_in[:, 2056:5128], w_in[:, 5136:5648]],
                            axis=1).astype(BF16)
    w_small = jnp.concatenate([w_in[:, 2048:2056], w_in[:, 5128:5136],
                               jnp.zeros((D_MODEL, LANES - 16), F32)], axis=1).astype(BF16)
    par = jnp.zeros((SUBLANES, LANES), F32)
    gdn_par = par.at[0].set(_lane_row(p['gdn_a_log'][l], GDN_HEADS)).at[1].set(_lane_row(p['gdn_dt_bias'][l], GDN_HEADS))
    ssm_par = par.at[0].set(_lane_row(p['ssm_a_log'][l], SSM_HEADS)).at[1].set(_lane_row(p['ssm_dt_bias'][l], SSM_HEADS))
    rw = jnp.concatenate([p['router_w'][l], jnp.zeros((D_MODEL, LANES - N_EXPERTS), F32)], axis=1)
    rwh = rw.astype(BF16)
    bgu = p['b_gate_up'][l]
    return dict(
        w_big=w_big, w_small=w_small, gdn_par=gdn_par, ssm_par=ssm_par,
        gdn_conv_w=p['gdn_conv_w'][l], gdn_norm_w=p['gdn_norm_w'][l][None, :],
        lambda_init=0.8 - 0.6 * math.exp(-0.3 * l), diff_lambda=p['diff_lambda'][l],
        diff_norm_w=p['diff_norm_w'][l][None, :],
        ssm_conv_w=p['ssm_conv_w'][l], ssm_conv_b=p['ssm_conv_b'][l][None, :],
        dskip=jnp.repeat(p['ssm_d'][l], SSM_HEADDIM)[None, :], ssm_norm_w=p['ssm_norm_w'][l][None, :],
        w_mem=p['w_mem'][l].astype(BF16), w_branch=p['w_branch'][l].astype(BF16), w_out=p['w_out'][l].astype(BF16),
        ln1_g=p['ln1_g'][l][None, :], ln1_b=p['ln1_b'][l][None, :],
        rwh=rwh, rwl=(rw - rwh.astype(F32)).astype(BF16), rb=_lane_row(p['router_b'][l], 0)[None, :],
        w_gate_lin=p['w_gate_lin'][l],
        bgl=bgu.reshape(-1, 2 * D_FF // DEINT_BLOCK, DEINT_BLOCK // 2, 2).transpose(0, 1, 3, 2).reshape(-1, 1, 2 * D_FF),
        w_down=p['w_down'][l].astype(BF16), b_down=p['b_down'][l][:, None, :],
        ln2_g=p['ln2_g'][l][None, :], ln2_b=p['ln2_b'][l][None, :])


def _mix_and_route(w, x, xbf, kv_src, bsz, seq, cos, sin):
    t = bsz * seq
    h, hs = _in_proj(xbf, w['w_big'], w['w_small'], min(1024, t), H_WIDTH // 4)
    y_gdn = _gdn(h, hs, w['gdn_conv_w'], w['gdn_par'], w['gdn_norm_w'], bsz, seq)
    y_diff = _diff_attention(h, cos, sin, w['diff_lambda'], w['diff_norm_w'], w['lambda_init'], bsz, seq)
    y_ssm = _ssd(h, hs, w['ssm_conv_w'], w['ssm_conv_b'], w['ssm_par'], w['dskip'], w['ssm_norm_w'], bsz, seq)
    kv = _matmul(kv_src, w['w_mem'], BF16, min(1024, kv_src.shape[0]), HB, "mem_kv")
    y_mem = _mem_attention(h, kv, bsz, seq, kv_src.shape[0] // bsz)
    x1, xa, xb, route, counts = _merge(h, (y_gdn, y_diff, y_ssm, y_mem), x, w['w_branch'], w['w_out'],
                                       w['ln1_g'], w['ln1_b'], w['rwh'], w['rwl'], w['rb'])
    pos, tile_expert, n_rows = _route_positions(route, counts)
    xsa, xsb = _sc_dispatch(xa, xb, pos, n_rows)
    return dict(x1=x1, route=route, pos=pos, tile_expert=tile_expert, xsa=xsa, xsb=xsb)


def _experts(w, st):
    ysa, ysb = _expert_mlp(st['tile_expert'], st['xsa'], st['xsb'], w['w_gate_lin'], w['bgl'], w['w_down'],
                           w['b_down'])
    t = st['x1'].shape[0]
    ga, gb = _sc_collect(ysa, ysb, st['pos'].reshape(1, TOP_K * t))
    return dict(st, ga=ga.reshape(TOP_K, t, SC_WORDS), gb=gb.reshape(TOP_K, t, SC_WORDS))


def _finish(w, st):
    return _combine(st['ga'], st['gb'], st['route'], st['x1'], w['ln2_g'], w['ln2_b'])


def kernel(x, mem, w_in, gdn_conv_w, gdn_a_log, gdn_dt_bias, gdn_norm_w, diff_lambda, diff_norm_w, ssm_conv_w, ssm_conv_b, ssm_a_log, ssm_dt_bias, ssm_d, ssm_norm_w, w_mem, w_branch, w_out, ln1_g, ln1_b, router_w, router_b, w_gate_up, b_gate_up, w_down, b_down, ln2_g, ln2_b):
    p = dict(w_in=w_in, gdn_conv_w=gdn_conv_w, gdn_a_log=gdn_a_log, gdn_dt_bias=gdn_dt_bias, gdn_norm_w=gdn_norm_w,
             diff_lambda=diff_lambda, diff_norm_w=diff_norm_w, ssm_conv_w=ssm_conv_w, ssm_conv_b=ssm_conv_b,
             ssm_a_log=ssm_a_log, ssm_dt_bias=ssm_dt_bias, ssm_d=ssm_d, ssm_norm_w=ssm_norm_w, w_mem=w_mem,
             w_branch=w_branch, w_out=w_out, ln1_g=ln1_g, ln1_b=ln1_b, router_w=router_w, router_b=router_b,
             w_gate_up=w_gate_up, b_gate_up=b_gate_up, w_down=w_down, b_down=b_down, ln2_g=ln2_g, ln2_b=ln2_b)
    bsz, seq, d = x.shape
    depth, n_exp = w_gate_up.shape[0], w_gate_up.shape[1]
    p['w_gate_lin'] = _deinterleave(w_gate_up.reshape(depth * n_exp * d, 2 * D_FF)).reshape(depth, n_exp, d, 2 * D_FF)
    cos, sin = _rope_tables(seq)
    groups = TOKEN_GROUPS if bsz % TOKEN_GROUPS == 0 else 1
    gb = bsz // groups
    xs = [x[g * gb:(g + 1) * gb].reshape(gb * seq, d) for g in range(groups)]
    xbfs = [v.astype(BF16) for v in xs]
    kvs = [mem[g * gb:(g + 1) * gb].reshape(gb * mem.shape[1], d).astype(BF16) for g in range(groups)]
    for l in range(depth):
        w = _layer_weights(l, p)
        sts = [_mix_and_route(w, xs[g], xbfs[g], kvs[g], gb, seq, cos, sin) for g in range(groups)]
        sts = [_experts(w, st) for st in sts]
        outs = [_finish(w, st) for st in sts]
        xs = [o[0] for o in outs]
        xbfs = [o[1] for o in outs]
    return jnp.concatenate([v.reshape(gb, seq, d) for v in xs], axis=0)
```

```python
import functools
import math

import jax
import jax.numpy as jnp
from jax import lax
from jax.experimental import pallas as pl
from jax.experimental.pallas import tpu as pltpu
from jax.experimental.pallas import tpu_sc as plsc

F32 = jnp.float32
BF16 = jnp.bfloat16
U32 = jnp.uint32

D_MODEL = 1024
DEPTH = 4
CHUNK = 64
Q_BLOCK = 128
CONV_WIDTH = 4
ROPE_THETA = 10000.0
NORM_EPS = 1e-6
LN_EPS = 1e-5

GDN_HEADS = 4
GDN_DK = 128
GDN_DV = 128
DIFF_HEADS = 4
DIFF_DQK = 64
DIFF_DV = 128
SSM_HEADS = 8
SSM_HEADDIM = 64
SSM_GROUPS = 2
SSM_STATE = 128
SSM_INNER = 512
MEM_HEADS = 4
MEM_HEADDIM = 128
N_BRANCH = 4
N_EXPERTS = 32
TOP_K = 4
D_FF = 1024
SWIGLU_LIMIT = 7.0
SWIGLU_ALPHA = 1.702
DEEPNORM_ALPHA = (2.0 * DEPTH) ** 0.25
LOG2_E = math.log2(math.e)

LANES = 128
SUBLANES = 8
HALO = 16
VMEM_LIMIT = 56 * 1024 * 1024

HB = 512
COL_GATES = 0
COL_GQ, COL_GK, COL_GV, COL_GZ = 8, 9, 10, 11
COL_DQ, COL_DK, COL_DV = 12, 13, 14
COL_SZ = 15
COL_SXBC = 8
COL_MQ = 18
H_WIDTH = 19 * HB

SEQ_TILE = 512
ROW_TILE = 512
EXPERT_TILE = 512
SC_WINDOW = 128
SC_WORDS = 256
DEINT_BLOCK = 2 * LANES
GDN_TILE = 512
DIFF_ROW_SPLIT = 2
TOKEN_GROUPS = 2


def _cparams(*sem):
    return pltpu.CompilerParams(dimension_semantics=sem, vmem_limit_bytes=VMEM_LIMIT)


def _dot(a, b):
    return jnp.dot(a.astype(BF16), b.astype(BF16), preferred_element_type=F32)


def _dot_nt(a, b):
    return lax.dot_general(a.astype(BF16), b.astype(BF16), (((1,), (1,)), ((), ())), preferred_element_type=F32)


def _bdot(a, b):
    return jnp.einsum('cmk,ckn->cmn', a.astype(BF16), b.astype(BF16), preferred_element_type=F32)


def _bdot_nt(a, b):
    return jnp.einsum('cmk,cnk->cmn', a.astype(BF16), b.astype(BF16), preferred_element_type=F32)


def _split2(a):
    hi = a.astype(BF16)
    lo = (a - hi.astype(F32)).astype(BF16)
    return hi, lo


def _sigmoid(x):
    return 0.5 * jnp.tanh(0.5 * x) + 0.5


def _silu(x):
    return x * _sigmoid(x)


def _softplus(x):
    return jnp.maximum(x, 0.0) + jnp.log1p(jnp.exp(-jnp.abs(x)))


def _rms(x, w):
    return x * lax.rsqrt(jnp.mean(x * x, axis=-1, keepdims=True) + NORM_EPS) * w


def _layer_norm(x, g, b):
    mu = jnp.mean(x, axis=-1, keepdims=True)
    xc = x - mu
    var = jnp.mean(xc * xc, axis=-1, keepdims=True)
    return xc * lax.rsqrt(var + LN_EPS) * g + b


def _mm_body(x_ref, w_ref, o_ref):
    o_ref[...] = jnp.dot(x_ref[...], w_ref[...], preferred_element_type=F32).astype(o_ref.dtype)


def _matmul(x, w, out_dtype, tm, tn, name):
    m, k = x.shape
    n = w.shape[1]
    return pl.pallas_call(
        _mm_body,
        grid=(m // tm, n // tn),
        in_specs=[pl.BlockSpec((tm, k), lambda i, j: (i, 0)), pl.BlockSpec((k, tn), lambda i, j: (0, j))],
        out_specs=pl.BlockSpec((tm, tn), lambda i, j: (i, j)),
        out_shape=jax.ShapeDtypeStruct((m, n), out_dtype),
        compiler_params=_cparams("parallel", "parallel"),
        name=name,
    )(x, w)


def _in_proj_body(x_ref, w_ref, ws_ref, o_ref, os_ref):
    x = x_ref[...]
    o_ref[...] = jnp.dot(x, w_ref[...], preferred_element_type=F32).astype(o_ref.dtype)

    @pl.when(pl.program_id(1) == 0)
    def _():
        os_ref[...] = jnp.dot(x, ws_ref[...], preferred_element_type=F32)


def _in_proj(x, w, w_small, tm, tn):
    m, k = x.shape
    n = w.shape[1]
    return pl.pallas_call(
        _in_proj_body,
        grid=(m // tm, n // tn),
        in_specs=[pl.BlockSpec((tm, k), lambda i, j: (i, 0)), pl.BlockSpec((k, tn), lambda i, j: (0, j)),
                  pl.BlockSpec((k, LANES), lambda i, j: (0, 0))],
        out_specs=[pl.BlockSpec((tm, tn), lambda i, j: (i, j)), pl.BlockSpec((tm, LANES), lambda i, j: (i, 0))],
        out_shape=[jax.ShapeDtypeStruct((m, n), BF16), jax.ShapeDtypeStruct((m, LANES), F32)],
        compiler_params=_cparams("parallel", "arbitrary"),
        name="in_proj",
    )(x, w, w_small)


def _shift_taps(nc):
    m = lax.broadcasted_iota(jnp.int32, (1, (CONV_WIDTH - 1) * CHUNK, CHUNK + HALO), 1)
    k = lax.broadcasted_iota(jnp.int32, (1, (CONV_WIDTH - 1) * CHUNK, CHUNK + HALO), 2)
    src = (m % CHUNK) + HALO - (CONV_WIDTH - 1 - m // CHUNK)
    return jnp.broadcast_to(_mask(k == src).astype(BF16), (nc, (CONV_WIDTH - 1) * CHUNK, CHUNK + HALO))


def _conv_silu(x, tail_ref, slot, xbuf_ref, w, bias, taps):
    rows, width = x.shape
    nc = rows // CHUNK
    xbuf_ref[0:HALO, :] = tail_ref[slot]
    xbuf_ref[HALO:HALO + rows, :] = x
    tail_ref[slot] = x[rows - HALO:rows, :]
    xext = jnp.concatenate([xbuf_ref[i * CHUNK:(i + 1) * CHUNK + HALO, :][None] for i in range(nc)], axis=0)
    shifted = _bdot(taps, xext)
    y = x.astype(F32).reshape(nc, CHUNK, width) * w[CONV_WIDTH - 1:CONV_WIDTH, :]
    for j in range(CONV_WIDTH - 1):
        y = y + shifted[:, j * CHUNK:(j + 1) * CHUNK, :] * w[j:j + 1, :]
    y = y.reshape(rows, width)
    if bias is not None:
        y = y + bias
    return _silu(y)


def _chunk_iota():
    r = lax.broadcasted_iota(jnp.int32, (1, CHUNK, CHUNK), 1)
    c = lax.broadcasted_iota(jnp.int32, (1, CHUNK, CHUNK), 2)
    return r, c


def _mask(cond):
    return jnp.where(cond, 1.0, 0.0)


def _segment_logdecay(g3, r, c):
    nb = g3.shape[0]
    lincl = jnp.broadcast_to(_mask(r >= c).astype(BF16), (nb, CHUNK, CHUNK))
    upper = _mask(r > c)
    ghi = g3.astype(BF16).astype(F32)
    glo = g3 - ghi
    return _bdot(lincl, ghi * upper) + _bdot(lincl, glo * upper)


def _inv_unit_lower(a, r, c):
    d = _mask(r == c) - a * _mask(((r >> 1) == (c >> 1)) & (r > c))
    for lb in range(1, 6):
        lower_left = _mask(((r >> (lb + 1)) == (c >> (lb + 1))) & ((r >> lb) > (c >> lb)))
        d = d - _bdot(_bdot(d, a * lower_left), d)
    return d


def _gdn_body(q_ref, k_ref, v_ref, z_ref, hs_ref, cw_ref, par_ref, nw_ref, o_ref,
              tail_ref, xbuf_ref, qs_ref, ks_ref, vs_ref, pt_ref, bt_ref, qwt_ref, qut_ref, ot_ref, gl_ref,
              state_ref):
    ts = q_ref.shape[0]
    nc = ts // CHUNK

    @pl.when(pl.program_id(1) == 0)
    def _():
        tail_ref[...] = jnp.zeros_like(tail_ref)
        state_ref[...] = jnp.zeros_like(state_ref)

    taps = _shift_taps(nc)
    for slot, (src, dst) in enumerate(((q_ref, qs_ref), (k_ref, ks_ref), (v_ref, vs_ref))):
        w = cw_ref[:, slot * HB:(slot + 1) * HB]
        dst[...] = _conv_silu(src[...], tail_ref, slot, xbuf_ref, w, None, taps)

    hs = hs_ref[...]
    beta_all = _sigmoid(hs)
    g_all = -jnp.exp(par_ref[0:1, :]) * _softplus(hs + par_ref[1:2, :])

    def stack(per_head):
        return jnp.concatenate([per_head(h) for h in range(GDN_HEADS)], axis=0)

    def l2(ref, h):
        x = ref[:, h * GDN_DK:(h + 1) * GDN_DK]
        return (x * lax.rsqrt(jnp.sum(x * x, axis=-1, keepdims=True) + NORM_EPS)).reshape(nc, CHUNK, GDN_DK)

    q3 = stack(lambda h: l2(qs_ref, h)) * (GDN_DK ** -0.5)
    k3 = stack(lambda h: l2(ks_ref, h))
    v3 = stack(lambda h: vs_ref[:, h * GDN_DV:(h + 1) * GDN_DV].reshape(nc, CHUNK, GDN_DV))
    b3 = stack(lambda h: beta_all[:, h:h + 1].reshape(nc, CHUNK, 1))
    g3 = stack(lambda h: g_all[:, GDN_HEADS + h:GDN_HEADS + h + 1].reshape(nc, CHUNK, 1))
    r, c = _chunk_iota()
    m = _segment_logdecay(g3, r, c)
    em = jnp.exp(m)
    a = _bdot_nt(k3, k3) * b3 * (em * _mask(r > c))
    t = _inv_unit_lower(a, r, c)
    gc = m[:, :, 0:1] + g3[:, 0:1, :]
    eg = jnp.exp(gc)
    g_last = gc[:, CHUNK - 1:CHUNK, :]
    sol = _bdot(t, jnp.concatenate([v3 * b3, k3 * (b3 * eg)], axis=-1))
    u = sol[:, :, :GDN_DV]
    w = sol[:, :, GDN_DV:]
    kd = k3 * jnp.exp(g_last - gc)
    qk = _bdot_nt(q3, k3) * (em * _mask(r >= c))
    pt = _bdot(jnp.swapaxes(w, 1, 2), kd).astype(BF16)
    bt = _bdot(jnp.swapaxes(u, 1, 2), kd)
    qwt = jnp.swapaxes(q3 * eg - _bdot(qk, w), 1, 2).astype(BF16)
    qut = jnp.swapaxes(_bdot(qk, u), 1, 2)
    gl = jnp.broadcast_to(jnp.exp(g_last), (GDN_HEADS * nc, SUBLANES, LANES))
    for h in range(GDN_HEADS):
        hb = slice(h * nc, (h + 1) * nc)
        pt_ref[h] = pt[hb]
        bt_ref[h] = bt[hb]
        qwt_ref[h] = qwt[hb]
        qut_ref[h] = qut[hb]
        gl_ref[h] = gl[hb]

    def chunk_step(ci, carry):
        for h in range(GDN_HEADS):
            st = state_ref[h]
            stb = st.astype(BF16)
            ot_ref[h, ci] = jnp.dot(stb, qwt_ref[h, ci], preferred_element_type=F32) + qut_ref[h, ci]
            state_ref[h] = (st * gl_ref[h, ci][0:1, :] - jnp.dot(stb, pt_ref[h, ci], preferred_element_type=F32)
                            + bt_ref[h, ci])
        return carry

    lax.fori_loop(0, nc, chunk_step, 0)

    for h in range(GDN_HEADS):
        sl = slice(h * GDN_DV, (h + 1) * GDN_DV)
        o = jnp.swapaxes(ot_ref[h], 1, 2).reshape(ts, GDN_DV)
        o_ref[:, sl] = (_rms(o, nw_ref[...]) * _silu(z_ref[:, sl].astype(F32))).astype(o_ref.dtype)


def _gdn(h, hs, conv_w, par, norm_w, bsz, seq):
    ts = min(GDN_TILE, seq)
    ns = seq // ts
    tok = lambda col: pl.BlockSpec((ts, HB), lambda b, s: (b * ns + s, col))
    full = lambda shape: pl.BlockSpec(shape, lambda b, s: (0,) * len(shape))
    nc = ts // CHUNK
    return pl.pallas_call(
        _gdn_body,
        grid=(bsz, ns),
        in_specs=[tok(COL_GQ), tok(COL_GK), tok(COL_GV), tok(COL_GZ),
                  pl.BlockSpec((ts, LANES), lambda b, s: (b * ns + s, 0)),
                  full((CONV_WIDTH, 3 * HB)), full((SUBLANES, LANES)), full((1, GDN_DV))],
        out_specs=pl.BlockSpec((ts, HB), lambda b, s: (b * ns + s, 0)),
        out_shape=jax.ShapeDtypeStruct((bsz * seq, HB), BF16),
        scratch_shapes=[
            pltpu.VMEM((3, HALO, HB), BF16),
            pltpu.VMEM((ts + HALO, HB), BF16),
            pltpu.VMEM((ts, HB), F32), pltpu.VMEM((ts, HB), F32), pltpu.VMEM((ts, HB), F32),
            pltpu.VMEM((GDN_HEADS, nc, GDN_DK, GDN_DK), BF16), pltpu.VMEM((GDN_HEADS, nc, GDN_DV, GDN_DK), F32),
            pltpu.VMEM((GDN_HEADS, nc, GDN_DK, CHUNK), BF16), pltpu.VMEM((GDN_HEADS, nc, GDN_DV, CHUNK), F32),
            pltpu.VMEM((GDN_HEADS, nc, GDN_DV, CHUNK), F32),
            pltpu.VMEM((GDN_HEADS, nc, SUBLANES, LANES), F32),
            pltpu.VMEM((GDN_HEADS, GDN_DV, GDN_DK), F32),
        ],
        compiler_params=_cparams("parallel", "arbitrary"),
        name="gdn",
    )(h, h, h, h, hs, conv_w, par, norm_w)


def _expand_heads(x, expand_hi):
    hi = x.astype(BF16)
    r1 = x - hi.astype(F32)
    mid = r1.astype(BF16)
    lo = (r1 - mid.astype(F32)).astype(BF16)
    e = expand_hi
    return (jnp.dot(hi, e, preferred_element_type=F32) + jnp.dot(mid, e, preferred_element_type=F32)
            + jnp.dot(lo, e, preferred_element_type=F32))


def _ssd_body(z_ref, xbc_ref, hs_ref, cw_ref, cb_ref, par_ref, dskip_ref, nw_ref, o_ref,
              tail_ref, xbuf_ref, xs_ref, bc_ref, yd_ref, hin_ref, state_ref):
    ts = z_ref.shape[0]
    nc = ts // CHUNK
    gw = SSM_INNER // SSM_GROUPS

    @pl.when(pl.program_id(1) == 0)
    def _():
        tail_ref[...] = jnp.zeros_like(tail_ref)
        state_ref[...] = jnp.zeros_like(state_ref)

    taps = _shift_taps(nc)
    for slot, dst in enumerate((xs_ref, bc_ref)):
        cols = slice(slot * HB, (slot + 1) * HB)
        dst[...] = _conv_silu(xbc_ref[:, cols], tail_ref, slot, xbuf_ref, cw_ref[:, cols], cb_ref[:, cols], taps)

    hs = hs_ref[...]
    dt = _softplus(hs + par_ref[1:2, :])
    a_all = -jnp.exp(par_ref[0:1, :]) * dt
    r, c = _chunk_iota()
    tril = _mask(r >= c)
    lincl = jnp.broadcast_to(tril.astype(BF16), (nc, CHUNK, CHUNK))
    a3 = a_all.reshape(nc, CHUNK, LANES)
    ahi = a3.astype(BF16)
    ar = a3 - ahi.astype(F32)
    amid = ar.astype(BF16)
    alo = (ar - amid.astype(F32)).astype(BF16)
    acum = _bdot(lincl, ahi) + _bdot(lincl, amid) + _bdot(lincl, alo)
    lane = lax.broadcasted_iota(jnp.int32, (LANES, SSM_INNER), 0)
    chan = lax.broadcasted_iota(jnp.int32, (LANES, SSM_INNER), 1)
    expand = jnp.where(lane - SSM_HEADS == chan // SSM_HEADDIM, 1.0, 0.0).astype(BF16)
    dt_x = _expand_heads(dt, expand)
    acum_x = _expand_heads(acum.reshape(ts, LANES), expand).reshape(nc, CHUNK, SSM_INNER)
    a_last_x = acum_x[:, CHUNK - 1:CHUNK, :]
    ea_x = jnp.exp(acum_x).reshape(ts, SSM_INNER)
    din_x = jnp.exp(a_last_x - acum_x).reshape(ts, SSM_INNER)
    cd_x = jnp.exp(a_last_x)

    xs = xs_ref[...]
    xdt = xs * dt_x
    xw = xdt * din_x
    for g in range(SSM_GROUPS):
        gcols = slice(g * gw, (g + 1) * gw)
        b3 = bc_ref[:, g * SSM_STATE:(g + 1) * SSM_STATE].reshape(nc, CHUNK, SSM_STATE)
        c3 = bc_ref[:, gw + g * SSM_STATE:gw + (g + 1) * SSM_STATE].reshape(nc, CHUNK, SSM_STATE)
        cb = _bdot_nt(c3, b3)
        hpg = SSM_HEADS // SSM_GROUPS
        heads = [g * hpg + hr for hr in range(hpg)]
        ah3 = jnp.concatenate([a_all[:, SSM_HEADS + hd:SSM_HEADS + hd + 1].reshape(nc, CHUNK, 1) for hd in heads],
                              axis=0)
        x4 = jnp.concatenate([xdt[:, hd * SSM_HEADDIM:(hd + 1) * SSM_HEADDIM].reshape(nc, CHUNK, SSM_HEADDIM)
                              for hd in heads], axis=0)
        seg = jnp.exp(_segment_logdecay(ah3, r, c)) * tril
        yd = _bdot(seg * jnp.concatenate([cb] * hpg, axis=0), x4)
        for hr, hd in enumerate(heads):
            yd_ref[:, hd * SSM_HEADDIM:(hd + 1) * SSM_HEADDIM] = yd[hr * nc:(hr + 1) * nc].reshape(ts, SSM_HEADDIM)
        bt = jnp.swapaxes(b3, 1, 2)
        st = _bdot(bt, xw[:, gcols].reshape(nc, CHUNK, gw))
        hcur = state_ref[g]
        for ci in range(nc):
            hin_ref[ci] = hcur
            hcur = hcur * cd_x[ci, 0:1, gcols] + st[ci]
        state_ref[g] = hcur
        y_off = _bdot(c3, hin_ref[...]).reshape(ts, gw) * ea_x[:, gcols]
        y = yd_ref[:, gcols] + y_off + dskip_ref[:, gcols] * xs[:, gcols]
        gated = y * _silu(z_ref[:, gcols].astype(F32))
        o_ref[:, gcols] = _rms(gated, nw_ref[:, gcols]).astype(o_ref.dtype)


def _ssd(h, hs, conv_w, conv_b, par, dskip, norm_w, bsz, seq):
    ts = min(SEQ_TILE, seq)
    ns = seq // ts
    nc = ts // CHUNK
    gw = SSM_INNER // SSM_GROUPS
    full = lambda shape: pl.BlockSpec(shape, lambda b, s: (0,) * len(shape))
    return pl.pallas_call(
        _ssd_body,
        grid=(bsz, ns),
        in_specs=[pl.BlockSpec((ts, HB), lambda b, s: (b * ns + s, COL_SZ)),
                  pl.BlockSpec((ts, 2 * HB), lambda b, s: (b * ns + s, COL_SXBC)),
                  pl.BlockSpec((ts, LANES), lambda b, s: (b * ns + s, 0)),
                  full((CONV_WIDTH, 2 * HB)), full((1, 2 * HB)), full((SUBLANES, LANES)),
                  full((1, SSM_INNER)), full((1, SSM_INNER))],
        out_specs=pl.BlockSpec((ts, HB), lambda b, s: (b * ns + s, 0)),
        out_shape=jax.ShapeDtypeStruct((bsz * seq, HB), BF16),
        scratch_shapes=[
            pltpu.VMEM((2, HALO, HB), BF16),
            pltpu.VMEM((ts + HALO, HB), BF16),
            pltpu.VMEM((ts, HB), F32), pltpu.VMEM((ts, HB), F32), pltpu.VMEM((ts, HB), F32),
            pltpu.VMEM((nc, SSM_STATE, gw), F32),
            pltpu.VMEM((SSM_GROUPS, SSM_STATE, gw), F32),
        ],
        compiler_params=_cparams("parallel", "arbitrary"),
        name="ssd",
    )(h, h, hs, conv_w, conv_b, par, dskip, norm_w)


def _rope(x, cos, sin_signed):
    width = x.shape[-1]
    half = DIFF_DQK // 2
    lane = lax.broadcasted_iota(jnp.int32, x.shape, x.ndim - 1)
    fwd = pltpu.roll(x, width - half, axis=x.ndim - 1)
    bwd = pltpu.roll(x, half, axis=x.ndim - 1)
    partner = jnp.where((lane & half) == 0, fwd, bwd)
    return x * cos + partner * sin_signed


def _diff_body(q_ref, k_ref, v_ref, cos_ref, sin_ref, lam_ref, nw_ref, o_ref, kr_ref, *, lambda_init, nq):
    qi = pl.program_id(2)
    tq = q_ref.shape[0]
    q0 = pl.multiple_of(qi * tq, tq)
    cos = cos_ref[...]
    sin = sin_ref[...]
    kr_ref[pl.ds(q0, tq), :] = _rope(k_ref[...].astype(F32), cos, sin).astype(BF16)
    qr = _rope(q_ref[...].astype(F32), cos, sin) * (DIFF_DQK ** -0.5 * LOG2_E)

    lp = lam_ref[...]
    prod01 = jnp.sum(lp[0:1, :] * lp[1:2, :], axis=-1, keepdims=True)
    prod23 = jnp.sum(lp[2:3, :] * lp[3:4, :], axis=-1, keepdims=True)
    lam = jnp.exp(prod01) - jnp.exp(prod23) + lambda_init

    lane = lax.broadcasted_iota(jnp.int32, (tq, 2 * DIFF_DQK), 1)
    qmaps = (jnp.where(lane < DIFF_DQK, qr, 0.0).astype(BF16), jnp.where(lane >= DIFF_DQK, qr, 0.0).astype(BF16))
    th = tq // DIFF_ROW_SPLIT

    for i in range(nq):

        @pl.when(qi == i)
        def _(i=i):
            lo = i * tq
            for rh in range(DIFF_ROW_SPLIT):
                rows = slice(rh * th, (rh + 1) * th)
                own = (rh + 1) * th
                row = lax.broadcasted_iota(jnp.int32, (th, own), 0) + rh * th
                col = lax.broadcasted_iota(jnp.int32, (th, own), 1)
                diag_ok = (row // CHUNK) >= (col // CHUNK)
                outs = []
                for qmap in qmaps:
                    qm = qmap[rows, :]
                    s_diag = jnp.where(diag_ok, _dot_nt(qm, kr_ref[lo:lo + own, :]), -jnp.inf)
                    mx = jnp.max(s_diag, axis=-1, keepdims=True)
                    if i > 0:
                        s_off = _dot_nt(qm, kr_ref[0:lo, :])
                        mx = jnp.maximum(mx, jnp.max(s_off, axis=-1, keepdims=True))
                    p_diag = jnp.exp2(s_diag - mx)
                    denom = jnp.sum(p_diag, axis=-1, keepdims=True)
                    acc = _dot(p_diag, v_ref[lo:lo + own, :])
                    if i > 0:
                        p_off = jnp.exp2(s_off - mx)
                        denom = denom + jnp.sum(p_off, axis=-1, keepdims=True)
                        acc = acc + _dot(p_off, v_ref[0:lo, :])
                    outs.append(acc / denom)
                o = outs[0] - lam * outs[1]
                o_ref[rows, :] = (_rms(o, nw_ref[...]) * (1.0 - lambda_init)).astype(o_ref.dtype)


def _diff_attention(h, cos, sin, lam_params, norm_w, lambda_init, bsz, seq):
    tq = min(SEQ_TILE, seq)
    nq = seq // tq
    per_head = HB // DIFF_DV
    full = lambda shape: pl.BlockSpec(shape, lambda b, hd, q: (0,) * len(shape))
    tok = lambda col: pl.BlockSpec((tq, DIFF_DV), lambda b, hd, q: (b * nq + q, col * per_head + hd))
    return pl.pallas_call(
        functools.partial(_diff_body, lambda_init=lambda_init, nq=nq),
        grid=(bsz, DIFF_HEADS, nq),
        in_specs=[tok(COL_DQ), tok(COL_DK),
                  pl.BlockSpec((seq, DIFF_DV), lambda b, hd, q: (b, COL_DV * per_head + hd)),
                  pl.BlockSpec((tq, DIFF_DV), lambda b, hd, q: (q, 0)),
                  pl.BlockSpec((tq, DIFF_DV), lambda b, hd, q: (q, 0)),
                  full((4, DIFF_DQK)), full((1, DIFF_DV))],
        out_specs=pl.BlockSpec((tq, DIFF_DV), lambda b, hd, q: (b * nq + q, hd)),
        out_shape=jax.ShapeDtypeStruct((bsz * seq, HB), BF16),
        scratch_shapes=[pltpu.VMEM((seq, DIFF_DV), BF16)],
        compiler_params=_cparams("parallel", "parallel", "arbitrary"),
        name="diff_attn",
    )(h, h, h, cos, sin, lam_params, norm_w)


def _mem_body(q_ref, kv_ref, o_ref):
    width = MEM_HEADS * MEM_HEADDIM
    for h in range(MEM_HEADS):
        cols = slice(h * MEM_HEADDIM, (h + 1) * MEM_HEADDIM)
        q = q_ref[:, cols].astype(F32) * (MEM_HEADDIM ** -0.5)
        s = _dot_nt(q, kv_ref[:, cols])
        m = jnp.max(s, axis=-1, keepdims=True)
        p = jnp.exp(s - m)
        p = p / jnp.sum(p, axis=-1, keepdims=True)
        o_ref[:, cols] = _dot(p, kv_ref[:, width + h * MEM_HEADDIM:width + (h + 1) * MEM_HEADDIM]).astype(o_ref.dtype)


def _mem_attention(h, kv, bsz, seq, mem_len):
    tm = min(ROW_TILE, seq)
    ns = seq // tm
    return pl.pallas_call(
        _mem_body,
        grid=(bsz, ns),
        in_specs=[pl.BlockSpec((tm, HB), lambda b, s: (b * ns + s, COL_MQ)),
                  pl.BlockSpec((mem_len, 2 * HB), lambda b, s: (b, 0))],
        out_specs=pl.BlockSpec((tm, HB), lambda b, s: (b * ns + s, 0)),
        out_shape=jax.ShapeDtypeStruct((bsz * seq, HB), BF16),
        compiler_params=_cparams("parallel", "parallel"),
        name="mem_attn",
    )(h, kv)


def _pack_halves(x):
    n = x.shape[1] // 2
    lo = pltpu.bitcast(x[:, :n].astype(BF16).astype(F32), U32)
    hi = pltpu.bitcast(x[:, n:].astype(BF16).astype(F32), U32)
    return (lo >> 16) | hi


def _unpack_halves(w):
    lo = pltpu.bitcast(w << 16, F32)
    hi = pltpu.bitcast(w & jnp.uint32(0xFFFF0000), F32)
    return lo.astype(BF16), hi.astype(BF16)


def _merge_body(g_ref, y0_ref, y1_ref, y2_ref, y3_ref, x_ref, wb_ref, wo_ref, lng_ref, lnb_ref,
                rwh_ref, rwl_ref, rb_ref, xo_ref, pa_ref, pb_ref, route_ref, cnt_ref):
    tm = x_ref.shape[0]
    merged = jnp.zeros((tm, D_MODEL), F32)
    for i, y_ref in enumerate((y0_ref, y1_ref, y2_ref, y3_ref)):
        gate = _sigmoid(g_ref[:, i * D_MODEL:(i + 1) * D_MODEL].astype(F32))
        merged = merged + gate * jnp.dot(y_ref[...], wb_ref[i], preferred_element_type=F32)
    mix = _dot(merged, wo_ref[...])
    xn = _layer_norm(DEEPNORM_ALPHA * x_ref[...] + mix, lng_ref[...], lnb_ref[...])
    xo_ref[...] = xn
    pa_ref[...] = _pack_halves(xn[:, :D_MODEL // 2])
    pb_ref[...] = _pack_halves(xn[:, D_MODEL // 2:])

    xh, xl = _split2(xn)
    logits = (jnp.dot(xh, rwh_ref[...], preferred_element_type=F32)
              + (jnp.dot(xh, rwl_ref[...], preferred_element_type=F32)
                 + jnp.dot(xl, rwh_ref[...], preferred_element_type=F32))) + rb_ref[...]
    lane = lax.broadcasted_iota(jnp.int32, (tm, LANES), 1)
    work = jnp.where(lane < N_EXPERTS, logits, -jnp.inf)
    vals, ids, sels = [], [], []
    for _ in range(TOP_K):
        mx = jnp.max(work, axis=-1, keepdims=True)
        idx = jnp.min(jnp.where(work == mx, lane, LANES), axis=-1, keepdims=True)
        sel = lane == idx
        vals.append(mx)
        ids.append(idx)
        sels.append(sel)
        work = jnp.where(sel, -jnp.inf, work)
    exps = [jnp.exp(v - vals[0]) for v in vals]
    denom = exps[0] + exps[1] + exps[2] + exps[3]
    chosen = jnp.where(sels[0] | sels[1] | sels[2] | sels[3], 1.0, 0.0)
    r = lax.broadcasted_iota(jnp.int32, (tm, tm), 0)
    c = lax.broadcasted_iota(jnp.int32, (tm, tm), 1)
    ltri = jnp.where(r >= c, 1.0, 0.0).astype(BF16)
    cum = jnp.dot(ltri, chosen.astype(BF16), preferred_element_type=F32)
    out = jnp.zeros((tm, LANES), F32)
    for k in range(TOP_K):
        rank = jnp.sum(jnp.where(sels[k], cum, 0.0), axis=-1, keepdims=True) - 1.0
        out = jnp.where(lane == k, exps[k] / denom, out)
        out = jnp.where(lane == TOP_K + k, ids[k].astype(F32), out)
        out = jnp.where(lane == 2 * TOP_K + k, rank, out)
    route_ref[...] = out
    cnt_ref[0] = jnp.broadcast_to(cum[tm - 1:tm, :], (SUBLANES, LANES))


def _merge(h, ys, x, wb, wo, lng, lnb, rwh, rwl, rb):
    t = x.shape[0]
    tm = min(ROW_TILE, t)
    nt = t // tm
    tok = lambda width: pl.BlockSpec((tm, width), lambda i: (i, 0))
    full = lambda shape: pl.BlockSpec(shape, lambda i: (0,) * len(shape))
    return pl.pallas_call(
        _merge_body,
        grid=(nt,),
        in_specs=[tok(N_BRANCH * D_MODEL), tok(HB), tok(HB), tok(HB), tok(HB), tok(D_MODEL),
                  full((N_BRANCH, HB, D_MODEL)), full((D_MODEL, D_MODEL)), full((1, D_MODEL)), full((1, D_MODEL)),
                  full((D_MODEL, LANES)), full((D_MODEL, LANES)), full((1, LANES))],
        out_specs=[tok(D_MODEL), tok(SC_WORDS), tok(SC_WORDS), tok(LANES),
                   pl.BlockSpec((1, SUBLANES, LANES), lambda i: (i, 0, 0))],
        out_shape=[jax.ShapeDtypeStruct((t, D_MODEL), F32), jax.ShapeDtypeStruct((t, SC_WORDS), U32),
                   jax.ShapeDtypeStruct((t, SC_WORDS), U32), jax.ShapeDtypeStruct((t, LANES), F32),
                   jax.ShapeDtypeStruct((nt, SUBLANES, LANES), F32)],
        compiler_params=_cparams("parallel"),
        name="merge_ln_router",
    )(h, *ys, x, wb, wo, lng, lnb, rwh, rwl, rb)


def _sc_mesh():
    return plsc.VectorSubcoreMesh(core_axis_name="core", subcore_axis_name="subcore")


def _sc_dispatch(xa, xb, pos, n_out):
    t, words = xa.shape
    k = pos.shape[0]
    out = jax.ShapeDtypeStruct((n_out, words), xa.dtype)

    @functools.partial(pl.kernel, out_type=(out, out), mesh=_sc_mesh())
    def scatter_rows(xa_hbm, xb_hbm, i_hbm, oa_hbm, ob_hbm):
        for x_hbm, o_hbm in ((xa_hbm, oa_hbm), (xb_hbm, ob_hbm)):
            def body(x_vmem, i_vmem, o_hbm=o_hbm):
                for kk in range(k):
                    pltpu.sync_copy(x_vmem, o_hbm.at[i_vmem.at[kk]])

            pltpu.emit_pipeline(
                body, grid=(t // SC_WINDOW,),
                in_specs=[pl.BlockSpec((SC_WINDOW, words), lambda i: (i, 0)),
                          pl.BlockSpec((k, SC_WINDOW), lambda i: (0, i))],
                out_specs=[],
                core_axis_name=("core", "subcore"),
                dimension_semantics=(pltpu.PARALLEL,),
            )(x_hbm, i_hbm)

    return scatter_rows(xa, xb, pos)


def _sc_collect(ya, yb, pos_flat):
    n = pos_flat.shape[1]
    words = ya.shape[1]
    out = jax.ShapeDtypeStruct((n, words), ya.dtype)

    @functools.partial(pl.kernel, out_type=(out, out), mesh=_sc_mesh())
    def gather_rows(ya_hbm, yb_hbm, i_hbm, oa_hbm, ob_hbm):
        for y_hbm, o_hbm in ((ya_hbm, oa_hbm), (yb_hbm, ob_hbm)):
            def body(i_vmem, o_vmem, y_hbm=y_hbm):
                pltpu.sync_copy(y_hbm.at[i_vmem.at[0]], o_vmem)

            pltpu.emit_pipeline(
                body, grid=(n // SC_WINDOW,),
                in_specs=[pl.BlockSpec((1, SC_WINDOW), lambda i: (0, i))],
                out_specs=[pl.BlockSpec((SC_WINDOW, words), lambda i: (i, 0))],
                core_axis_name=("core", "subcore"),
                dimension_semantics=(pltpu.PARALLEL,),
            )(i_hbm, o_hbm)

    return gather_rows(ya, yb, pos_flat)


def _expert_body(te_ref, xa_ref, xb_ref, wgl_ref, bgl_ref, wd_ref, bd_ref, ya_ref, yb_ref):
    pl.when(pl.program_id(0) < te_ref[pl.num_programs(0)])(
        functools.partial(_expert_tile, xa_ref, xb_ref, wgl_ref, bgl_ref, wd_ref, bd_ref, ya_ref, yb_ref))


def _expert_tile(xa_ref, xb_ref, wgl_ref, bgl_ref, wd_ref, bd_ref, ya_ref, yb_ref):
    x0, x1 = _unpack_halves(xa_ref[...])
    x2, x3 = _unpack_halves(xb_ref[...])
    x = jnp.concatenate([x0, x1, x2, x3], axis=1)
    gl = jnp.dot(x, wgl_ref[0], preferred_element_type=F32) + bgl_ref[0]
    nblk = 2 * D_FF // DEINT_BLOCK
    half = DEINT_BLOCK // 2
    glu = jnp.concatenate([gl[:, b * DEINT_BLOCK:b * DEINT_BLOCK + half] for b in range(nblk)], axis=1)
    lin = jnp.concatenate([gl[:, b * DEINT_BLOCK + half:(b + 1) * DEINT_BLOCK] for b in range(nblk)], axis=1)
    glu = jnp.minimum(glu, SWIGLU_LIMIT)
    lin = jnp.clip(lin, -SWIGLU_LIMIT, SWIGLU_LIMIT)
    act = (lin + 1.0) * glu * _sigmoid(SWIGLU_ALPHA * glu)
    y = _dot(act, wd_ref[0]) + bd_ref[0]
    ya_ref[...] = _pack_halves(y[:, :D_MODEL // 2])
    yb_ref[...] = _pack_halves(y[:, D_MODEL // 2:])


def _expert_mlp(tile_expert, xa, xb, wgl, bgl, wd, bd, first):
    n = xa.shape[0]
    tr = EXPERT_TILE
    tok = pl.BlockSpec((tr, SC_WORDS), lambda i, te: (i, 0))
    per_expert = lambda shape: pl.BlockSpec((1,) + shape, lambda i, te: (te[i] + first,) + (0,) * len(shape))
    out = jax.ShapeDtypeStruct((n, SC_WORDS), U32)
    return pl.pallas_call(
        _expert_body,
        grid_spec=pltpu.PrefetchScalarGridSpec(
            num_scalar_prefetch=1,
            grid=(n // tr,),
            in_specs=[tok, tok, per_expert((D_MODEL, 2 * D_FF)), per_expert((1, 2 * D_FF)),
                      per_expert((D_FF, D_MODEL)), per_expert((1, D_MODEL))],
            out_specs=[tok, tok],
        ),
        out_shape=[out, out],
        compiler_params=_cparams("arbitrary"),
        name="expert_mlp",
    )(tile_expert, xa, xb, wgl, bgl, wd, bd)


def _deinterleave_body(w_ref, p_ref, o_ref):
    for b in range(w_ref.shape[1] // DEINT_BLOCK):
        cols = slice(b * DEINT_BLOCK, (b + 1) * DEINT_BLOCK)
        o_ref[:, cols] = jnp.dot(w_ref[:, cols].astype(BF16), p_ref[...], preferred_element_type=F32).astype(o_ref.dtype)


def _deinterleave_perm():
    src = lax.broadcasted_iota(jnp.int32, (DEINT_BLOCK, DEINT_BLOCK), 0)
    dst = lax.broadcasted_iota(jnp.int32, (DEINT_BLOCK, DEINT_BLOCK), 1)
    half = DEINT_BLOCK // 2
    return jnp.where(src == jnp.where(dst < half, 2 * dst, 2 * (dst - half) + 1), 1.0, 0.0)


def _deinterleave(w):
    rows, n2 = w.shape
    tm = min(ROW_TILE, rows)
    return pl.pallas_call(
        _deinterleave_body,
        grid=(rows // tm,),
        in_specs=[pl.BlockSpec((tm, n2), lambda i: (i, 0)),
                  pl.BlockSpec((DEINT_BLOCK, DEINT_BLOCK), lambda i: (0, 0))],
        out_specs=pl.BlockSpec((tm, n2), lambda i: (i, 0)),
        out_shape=jax.ShapeDtypeStruct((rows, n2), BF16),
        compiler_params=_cparams("parallel"),
        name="deinterleave",
    )(w, _deinterleave_perm().astype(BF16))


def _combine_body(ya_ref, yb_ref, route_ref, x_ref, lng_ref, lnb_ref, xo_ref, xb_ref):
    x = x_ref[...]
    route = route_ref[...]
    half = D_MODEL // 2
    q = D_MODEL // 4
    ffn_a = jnp.zeros((x.shape[0], half), F32)
    ffn_b = jnp.zeros((x.shape[0], half), F32)
    for k in range(TOP_K):
        wk = route[:, k:k + 1]
        a0, a1 = _unpack_halves(ya_ref[k])
        b0, b1 = _unpack_halves(yb_ref[k])
        ffn_a = ffn_a + wk * jnp.concatenate([a0.astype(F32), a1.astype(F32)], axis=-1)
        ffn_b = ffn_b + wk * jnp.concatenate([b0.astype(F32), b1.astype(F32)], axis=-1)
    del q
    ffn = jnp.concatenate([ffn_a, ffn_b], axis=-1)
    xn = _layer_norm(DEEPNORM_ALPHA * x + ffn, lng_ref[...], lnb_ref[...])
    xo_ref[...] = xn
    xb_ref[...] = xn.astype(BF16)


def _combine(ya, yb, route, x, lng, lnb):
    t = x.shape[0]
    tm = min(ROW_TILE, t)
    tok = lambda width: pl.BlockSpec((tm, width), lambda i: (i, 0))
    gathered = pl.BlockSpec((TOP_K, tm, SC_WORDS), lambda i: (0, i, 0))
    full = lambda shape: pl.BlockSpec(shape, lambda i: (0,) * len(shape))
    return pl.pallas_call(
        _combine_body,
        grid=(t // tm,),
        in_specs=[gathered, gathered, tok(LANES), tok(D_MODEL), full((1, D_MODEL)), full((1, D_MODEL))],
        out_specs=[tok(D_MODEL), tok(D_MODEL)],
        out_shape=[jax.ShapeDtypeStruct((t, D_MODEL), F32), jax.ShapeDtypeStruct((t, D_MODEL), BF16)],
        compiler_params=_cparams("parallel"),
        name="combine_ln",
    )(ya, yb, route, x, lng, lnb)


def _route_positions(route, counts):
    t = route.shape[0]
    nt = counts.shape[0]
    tm = t // nt
    ids = route[:, TOP_K:2 * TOP_K].astype(jnp.int32).reshape(nt, tm, TOP_K)
    rank = route[:, 2 * TOP_K:3 * TOP_K].astype(jnp.int32).reshape(nt, tm, TOP_K)
    cnt = counts[:, 0, :N_EXPERTS].astype(jnp.int32)
    before = jnp.cumsum(cnt, axis=0) - cnt
    total = jnp.sum(cnt, axis=0)
    padded = ((total + EXPERT_TILE - 1) // EXPERT_TILE) * EXPERT_TILE
    ends = jnp.cumsum(padded)
    offs = (ends - padded)[None, :] + before
    experts = jnp.arange(N_EXPERTS, dtype=jnp.int32)
    pos = rank + jnp.sum(jnp.where(ids[..., None] == experts, offs[:, None, None, :], 0), axis=-1)
    n_rows = TOP_K * t + N_EXPERTS * EXPERT_TILE
    tile_start = jnp.arange(n_rows // EXPERT_TILE, dtype=jnp.int32) * EXPERT_TILE
    tile_expert = jnp.minimum(jnp.sum((tile_start[:, None] >= ends[None, :]).astype(jnp.int32), axis=1), N_EXPERTS - 1)
    tiles_used = ends[N_EXPERTS - 1:] // EXPERT_TILE
    return pos.reshape(t, TOP_K).T, jnp.concatenate([tile_expert, tiles_used]), n_rows


def _rope_tables(seq):
    half = DIFF_DQK // 2
    inv_freq = ROPE_THETA ** (-jnp.arange(half, dtype=F32) / half)
    ang = jnp.arange(seq, dtype=F32)[:, None] * inv_freq[None, :]
    cos = jnp.cos(ang)
    sin = jnp.sin(ang)
    reps = DIFF_DV // DIFF_DQK
    cos_t = jnp.tile(jnp.concatenate([cos, cos], axis=-1), (1, reps))
    sin_t = jnp.tile(jnp.concatenate([-sin, sin], axis=-1), (1, reps))
    return cos_t, sin_t


def _lane_row(vec, offset):
    return jnp.zeros((LANES,), F32).at[offset:offset + vec.shape[0]].set(vec.astype(F32))


def _layer_weights(l, p):
    w_in = p['w_in'][l]
    w_big = jnp.concatenate([w_in[:, 5648:9744], w_in[:, 0:2048], w_in[:, 2056:5128], w_in[:, 5136:5648]],
                            axis=1).astype(BF16)
    w_small = jnp.concatenate([w_in[:, 2048:2056], w_in[:, 5128:5136],
                               jnp.zeros((D_MODEL, LANES - 16), F32)], axis=1).astype(BF16)
    par = jnp.zeros((SUBLANES, LANES), F32)
    gdn_par = par.at[0].set(_lane_row(p['gdn_a_log'][l], GDN_HEADS)).at[1].set(_lane_row(p['gdn_dt_bias'][l], GDN_HEADS))
    ssm_par = par.at[0].set(_lane_row(p['ssm_a_log'][l], SSM_HEADS)).at[1].set(_lane_row(p['ssm_dt_bias'][l], SSM_HEADS))
    rw = jnp.concatenate([p['router_w'][l], jnp.zeros((D_MODEL, LANES - N_EXPERTS), F32)], axis=1)
    rwh = rw.astype(BF16)
    bgu = p['b_gate_up'][l]
    return dict(
        w_big=w_big, w_small=w_small, gdn_par=gdn_par, ssm_par=ssm_par,
        gdn_conv_w=p['gdn_conv_w'][l], gdn_norm_w=p['gdn_norm_w'][l][None, :],
        lambda_init=0.8 - 0.6 * math.exp(-0.3 * l), diff_lambda=p['diff_lambda'][l],
        diff_norm_w=p['diff_norm_w'][l][None, :],
        ssm_conv_w=p['ssm_conv_w'][l], ssm_conv_b=p['ssm_conv_b'][l][None, :],
        dskip=jnp.repeat(p['ssm_d'][l], SSM_HEADDIM)[None, :], ssm_norm_w=p['ssm_norm_w'][l][None, :],
        w_mem=p['w_mem'][l].astype(BF16), w_branch=p['w_branch'][l].astype(BF16), w_out=p['w_out'][l].astype(BF16),
        ln1_g=p['ln1_g'][l][None, :], ln1_b=p['ln1_b'][l][None, :],
        rwh=rwh, rwl=(rw - rwh.astype(F32)).astype(BF16), rb=_lane_row(p['router_b'][l], 0)[None, :],
        first_expert=l * bgu.shape[0],
        ln2_g=p['ln2_g'][l][None, :], ln2_b=p['ln2_b'][l][None, :])


def _mix_and_route(w, x, xbf, kv_src, bsz, seq, cos, sin):
    t = bsz * seq
    h, hs = _in_proj(xbf, w['w_big'], w['w_small'], min(1024, t), H_WIDTH // 4)
    y_gdn = _gdn(h, hs, w['gdn_conv_w'], w['gdn_par'], w['gdn_norm_w'], bsz, seq)
    y_diff = _diff_attention(h, cos, sin, w['diff_lambda'], w['diff_norm_w'], w['lambda_init'], bsz, seq)
    y_ssm = _ssd(h, hs, w['ssm_conv_w'], w['ssm_conv_b'], w['ssm_par'], w['dskip'], w['ssm_norm_w'], bsz, seq)
    kv = _matmul(kv_src, w['w_mem'], BF16, min(1024, kv_src.shape[0]), HB, "mem_kv")
    y_mem = _mem_attention(h, kv, bsz, seq, kv_src.shape[0] // bsz)
    x1, xa, xb, route, counts = _merge(h, (y_gdn, y_diff, y_ssm, y_mem), x, w['w_branch'], w['w_out'],
                                       w['ln1_g'], w['ln1_b'], w['rwh'], w['rwl'], w['rb'])
    pos, tile_expert, n_rows = _route_positions(route, counts)
    xsa, xsb = _sc_dispatch(xa, xb, pos, n_rows)
    return dict(x1=x1, route=route, pos=pos, tile_expert=tile_expert, xsa=xsa, xsb=xsb)


def _experts(w, ew, st):
    ysa, ysb = _expert_mlp(st['tile_expert'], st['xsa'], st['xsb'], ew['w_gate_lin'], ew['b_gate_lin'], ew['w_down'],
                           ew['b_down'], w['first_expert'])
    t = st['x1'].shape[0]
    ga, gb = _sc_collect(ysa, ysb, st['pos'].reshape(1, TOP_K * t))
    return dict(st, ga=ga.reshape(TOP_K, t, SC_WORDS), gb=gb.reshape(TOP_K, t, SC_WORDS))


def _finish(w, st):
    return _combine(st['ga'], st['gb'], st['route'], st['x1'], w['ln2_g'], w['ln2_b'])


def kernel(x, mem, w_in, gdn_conv_w, gdn_a_log, gdn_dt_bias, gdn_norm_w, diff_lambda, diff_norm_w, ssm_conv_w, ssm_conv_b, ssm_a_log, ssm_dt_bias, ssm_d, ssm_norm_w, w_mem, w_branch, w_out, ln1_g, ln1_b, router_w, router_b, w_gate_up, b_gate_up, w_down, b_down, ln2_g, ln2_b):
    p = dict(w_in=w_in, gdn_conv_w=gdn_conv_w, gdn_a_log=gdn_a_log, gdn_dt_bias=gdn_dt_bias, gdn_norm_w=gdn_norm_w,
             diff_lambda=diff_lambda, diff_norm_w=diff_norm_w, ssm_conv_w=ssm_conv_w, ssm_conv_b=ssm_conv_b,
             ssm_a_log=ssm_a_log, ssm_dt_bias=ssm_dt_bias, ssm_d=ssm_d, ssm_norm_w=ssm_norm_w, w_mem=w_mem,
             w_branch=w_branch, w_out=w_out, ln1_g=ln1_g, ln1_b=ln1_b, router_w=router_w, router_b=router_b,
             w_gate_up=w_gate_up, b_gate_up=b_gate_up, w_down=w_down, b_down=b_down, ln2_g=ln2_g, ln2_b=ln2_b)
    bsz, seq, d = x.shape
    depth, n_exp = w_gate_up.shape[0], w_gate_up.shape[1]
    expert_w = dict(
        w_gate_lin=_deinterleave(w_gate_up.reshape(depth * n_exp * d, 2 * D_FF)).reshape(depth * n_exp, d, 2 * D_FF),
        b_gate_lin=b_gate_up.reshape(depth * n_exp, 2 * D_FF // DEINT_BLOCK, DEINT_BLOCK // 2, 2)
        .transpose(0, 1, 3, 2).reshape(depth * n_exp, 1, 2 * D_FF),
        w_down=w_down.astype(BF16).reshape(depth * n_exp, D_FF, d),
        b_down=b_down.reshape(depth * n_exp, 1, d))
    cos, sin = _rope_tables(seq)
    groups = TOKEN_GROUPS if bsz % TOKEN_GROUPS == 0 else 1
    gb = bsz // groups
    xs = [x[g * gb:(g + 1) * gb].reshape(gb * seq, d) for g in range(groups)]
    xbfs = [v.astype(BF16) for v in xs]
    kvs = [mem[g * gb:(g + 1) * gb].reshape(gb * mem.shape[1], d).astype(BF16) for g in range(groups)]
    for l in range(depth):
        w = _layer_weights(l, p)
        sts = [_mix_and_route(w, xs[g], xbfs[g], kvs[g], gb, seq, cos, sin) for g in range(groups)]
        sts = [_experts(w, expert_w, st) for st in sts]
        outs = [_finish(w, st) for st in sts]
        xs = [o[0] for o in outs]
        xbfs = [o[1] for o in outs]
    return jnp.concatenate([v.reshape(gb, seq, d) for v in xs], axis=0)
```

```python
import functools
import math

import jax
import jax.numpy as jnp
from jax import lax
from jax.experimental import pallas as pl
from jax.experimental.pallas import tpu as pltpu
from jax.experimental.pallas import tpu_sc as plsc

F32 = jnp.float32
BF16 = jnp.bfloat16
U32 = jnp.uint32

D_MODEL = 1024
DEPTH = 4
CHUNK = 64
Q_BLOCK = 128
CONV_WIDTH = 4
ROPE_THETA = 10000.0
NORM_EPS = 1e-6
LN_EPS = 1e-5

GDN_HEADS = 4
GDN_DK = 128
GDN_DV = 128
DIFF_HEADS = 4
DIFF_DQK = 64
DIFF_DV = 128
SSM_HEADS = 8
SSM_HEADDIM = 64
SSM_GROUPS = 2
SSM_STATE = 128
SSM_INNER = 512
MEM_HEADS = 4
MEM_HEADDIM = 128
N_BRANCH = 4
N_EXPERTS = 32
TOP_K = 4
D_FF = 1024
SWIGLU_LIMIT = 7.0
SWIGLU_ALPHA = 1.702
DEEPNORM_ALPHA = (2.0 * DEPTH) ** 0.25
LOG2_E = math.log2(math.e)

LANES = 128
SUBLANES = 8
HALO = 16
VMEM_LIMIT = 56 * 1024 * 1024

HB = 512
COL_GATES = 0
COL_GQ, COL_GK, COL_GV, COL_GZ = 8, 9, 10, 11
COL_DQ, COL_DK, COL_DV = 12, 13, 14
COL_SZ = 15
COL_SXBC = 8
COL_MQ = 18
H_WIDTH = 19 * HB

SEQ_TILE = 512
ROW_TILE = 512
EXPERT_TILE = 512
SC_WINDOW = 128
SC_WORDS = 256
ROUTE_ROWS = 16
DEINT_BLOCK = 2 * LANES
GDN_TILE = 512
DIFF_ROW_SPLIT = 2
TOKEN_GROUPS = 2


def _cparams(*sem):
    return pltpu.CompilerParams(dimension_semantics=sem, vmem_limit_bytes=VMEM_LIMIT)


def _dot(a, b):
    return jnp.dot(a.astype(BF16), b.astype(BF16), preferred_element_type=F32)


def _dot_nt(a, b):
    return lax.dot_general(a.astype(BF16), b.astype(BF16), (((1,), (1,)), ((), ())), preferred_element_type=F32)


def _bdot(a, b):
    return jnp.einsum('cmk,ckn->cmn', a.astype(BF16), b.astype(BF16), preferred_element_type=F32)


def _bdot_nt(a, b):
    return jnp.einsum('cmk,cnk->cmn', a.astype(BF16), b.astype(BF16), preferred_element_type=F32)


def _split2(a):
    hi = a.astype(BF16)
    lo = (a - hi.astype(F32)).astype(BF16)
    return hi, lo


def _sigmoid(x):
    return 0.5 * jnp.tanh(0.5 * x) + 0.5


def _silu(x):
    return x * _sigmoid(x)


def _softplus(x):
    return jnp.maximum(x, 0.0) + jnp.log1p(jnp.exp(-jnp.abs(x)))


def _rms(x, w):
    return x * lax.rsqrt(jnp.mean(x * x, axis=-1, keepdims=True) + NORM_EPS) * w


def _layer_norm(x, g, b):
    mu = jnp.mean(x, axis=-1, keepdims=True)
    xc = x - mu
    var = jnp.mean(xc * xc, axis=-1, keepdims=True)
    return xc * lax.rsqrt(var + LN_EPS) * g + b


def _mm_body(x_ref, w_ref, o_ref):
    o_ref[...] = jnp.dot(x_ref[...], w_ref[...], preferred_element_type=F32).astype(o_ref.dtype)


def _matmul(x, w, out_dtype, tm, tn, name):
    m, k = x.shape
    n = w.shape[1]
    return pl.pallas_call(
        _mm_body,
        grid=(m // tm, n // tn),
        in_specs=[pl.BlockSpec((tm, k), lambda i, j: (i, 0)), pl.BlockSpec((k, tn), lambda i, j: (0, j))],
        out_specs=pl.BlockSpec((tm, tn), lambda i, j: (i, j)),
        out_shape=jax.ShapeDtypeStruct((m, n), out_dtype),
        compiler_params=_cparams("parallel", "parallel"),
        name=name,
    )(x, w)


def _in_proj_body(x_ref, w_ref, ws_ref, o_ref, os_ref):
    x = x_ref[...]
    o_ref[...] = jnp.dot(x, w_ref[...], preferred_element_type=F32).astype(o_ref.dtype)

    @pl.when(pl.program_id(1) == 0)
    def _():
        os_ref[...] = jnp.dot(x, ws_ref[...], preferred_element_type=F32)


def _in_proj(x, w, w_small, tm, tn):
    m, k = x.shape
    n = w.shape[1]
    return pl.pallas_call(
        _in_proj_body,
        grid=(m // tm, n // tn),
        in_specs=[pl.BlockSpec((tm, k), lambda i, j: (i, 0)), pl.BlockSpec((k, tn), lambda i, j: (0, j)),
                  pl.BlockSpec((k, LANES), lambda i, j: (0, 0))],
        out_specs=[pl.BlockSpec((tm, tn), lambda i, j: (i, j)), pl.BlockSpec((tm, LANES), lambda i, j: (i, 0))],
        out_shape=[jax.ShapeDtypeStruct((m, n), BF16), jax.ShapeDtypeStruct((m, LANES), F32)],
        compiler_params=_cparams("parallel", "arbitrary"),
        name="in_proj",
    )(x, w, w_small)


def _shift_taps(nc):
    m = lax.broadcasted_iota(jnp.int32, (1, (CONV_WIDTH - 1) * CHUNK, CHUNK + HALO), 1)
    k = lax.broadcasted_iota(jnp.int32, (1, (CONV_WIDTH - 1) * CHUNK, CHUNK + HALO), 2)
    src = (m % CHUNK) + HALO - (CONV_WIDTH - 1 - m // CHUNK)
    return jnp.broadcast_to(_mask(k == src).astype(BF16), (nc, (CONV_WIDTH - 1) * CHUNK, CHUNK + HALO))


def _conv_silu(x, tail_ref, slot, xbuf_ref, w, bias, taps):
    rows, width = x.shape
    nc = rows // CHUNK
    xbuf_ref[0:HALO, :] = tail_ref[slot]
    xbuf_ref[HALO:HALO + rows, :] = x
    tail_ref[slot] = x[rows - HALO:rows, :]
    xext = jnp.concatenate([xbuf_ref[i * CHUNK:(i + 1) * CHUNK + HALO, :][None] for i in range(nc)], axis=0)
    shifted = _bdot(taps, xext)
    y = x.astype(F32).reshape(nc, CHUNK, width) * w[CONV_WIDTH - 1:CONV_WIDTH, :]
    for j in range(CONV_WIDTH - 1):
        y = y + shifted[:, j * CHUNK:(j + 1) * CHUNK, :] * w[j:j + 1, :]
    y = y.reshape(rows, width)
    if bias is not None:
        y = y + bias
    return _silu(y)


def _chunk_iota():
    r = lax.broadcasted_iota(jnp.int32, (1, CHUNK, CHUNK), 1)
    c = lax.broadcasted_iota(jnp.int32, (1, CHUNK, CHUNK), 2)
    return r, c


def _mask(cond):
    return jnp.where(cond, 1.0, 0.0)


def _segment_logdecay(g3, r, c):
    nb = g3.shape[0]
    lincl = jnp.broadcast_to(_mask(r >= c).astype(BF16), (nb, CHUNK, CHUNK))
    upper = _mask(r > c)
    ghi = g3.astype(BF16).astype(F32)
    glo = g3 - ghi
    return _bdot(lincl, ghi * upper) + _bdot(lincl, glo * upper)


def _inv_unit_lower(a, r, c):
    d = _mask(r == c) - a * _mask(((r >> 1) == (c >> 1)) & (r > c))
    for lb in range(1, 6):
        lower_left = _mask(((r >> (lb + 1)) == (c >> (lb + 1))) & ((r >> lb) > (c >> lb)))
        d = d - _bdot(_bdot(d, a * lower_left), d)
    return d


def _gdn_body(q_ref, k_ref, v_ref, z_ref, hs_ref, cw_ref, par_ref, nw_ref, o_ref,
              tail_ref, xbuf_ref, qs_ref, ks_ref, vs_ref, pt_ref, bt_ref, qwt_ref, qut_ref, ot_ref, gl_ref,
              state_ref):
    ts = q_ref.shape[0]
    nc = ts // CHUNK

    @pl.when(pl.program_id(1) == 0)
    def _():
        tail_ref[...] = jnp.zeros_like(tail_ref)
        state_ref[...] = jnp.zeros_like(state_ref)

    taps = _shift_taps(nc)
    for slot, (src, dst) in enumerate(((q_ref, qs_ref), (k_ref, ks_ref), (v_ref, vs_ref))):
        w = cw_ref[:, slot * HB:(slot + 1) * HB]
        dst[...] = _conv_silu(src[...], tail_ref, slot, xbuf_ref, w, None, taps)

    hs = hs_ref[...]
    beta_all = _sigmoid(hs)
    g_all = -jnp.exp(par_ref[0:1, :]) * _softplus(hs + par_ref[1:2, :])

    def stack(per_head):
        return jnp.concatenate([per_head(h) for h in range(GDN_HEADS)], axis=0)

    def l2(ref, h):
        x = ref[:, h * GDN_DK:(h + 1) * GDN_DK]
        return (x * lax.rsqrt(jnp.sum(x * x, axis=-1, keepdims=True) + NORM_EPS)).reshape(nc, CHUNK, GDN_DK)

    q3 = stack(lambda h: l2(qs_ref, h)) * (GDN_DK ** -0.5)
    k3 = stack(lambda h: l2(ks_ref, h))
    v3 = stack(lambda h: vs_ref[:, h * GDN_DV:(h + 1) * GDN_DV].reshape(nc, CHUNK, GDN_DV))
    b3 = stack(lambda h: beta_all[:, h:h + 1].reshape(nc, CHUNK, 1))
    g3 = stack(lambda h: g_all[:, GDN_HEADS + h:GDN_HEADS + h + 1].reshape(nc, CHUNK, 1))
    r, c = _chunk_iota()
    m = _segment_logdecay(g3, r, c)
    em = jnp.exp(m)
    a = _bdot_nt(k3, k3) * b3 * (em * _mask(r > c))
    t = _inv_unit_lower(a, r, c)
    gc = m[:, :, 0:1] + g3[:, 0:1, :]
    eg = jnp.exp(gc)
    g_last = gc[:, CHUNK - 1:CHUNK, :]
    sol = _bdot(t, jnp.concatenate([v3 * b3, k3 * (b3 * eg)], axis=-1))
    u = sol[:, :, :GDN_DV]
    w = sol[:, :, GDN_DV:]
    kd = k3 * jnp.exp(g_last - gc)
    qk = _bdot_nt(q3, k3) * (em * _mask(r >= c))
    pt = _bdot(jnp.swapaxes(w, 1, 2), kd).astype(BF16)
    bt = _bdot(jnp.swapaxes(u, 1, 2), kd)
    qwt = jnp.swapaxes(q3 * eg - _bdot(qk, w), 1, 2).astype(BF16)
    qut = jnp.swapaxes(_bdot(qk, u), 1, 2)
    gl = jnp.broadcast_to(jnp.exp(g_last), (GDN_HEADS * nc, SUBLANES, LANES))
    for h in range(GDN_HEADS):
        hb = slice(h * nc, (h + 1) * nc)
        pt_ref[h] = pt[hb]
        bt_ref[h] = bt[hb]
        qwt_ref[h] = qwt[hb]
        qut_ref[h] = qut[hb]
        gl_ref[h] = gl[hb]

    def chunk_step(ci, carry):
        for h in range(GDN_HEADS):
            st = state_ref[h]
            stb = st.astype(BF16)
            ot_ref[h, ci] = jnp.dot(stb, qwt_ref[h, ci], preferred_element_type=F32) + qut_ref[h, ci]
            state_ref[h] = (st * gl_ref[h, ci][0:1, :] - jnp.dot(stb, pt_ref[h, ci], preferred_element_type=F32)
                            + bt_ref[h, ci])
        return carry

    lax.fori_loop(0, nc, chunk_step, 0)

    for h in range(GDN_HEADS):
        sl = slice(h * GDN_DV, (h + 1) * GDN_DV)
        o = jnp.swapaxes(ot_ref[h], 1, 2).reshape(ts, GDN_DV)
        o_ref[:, sl] = (_rms(o, nw_ref[...]) * _silu(z_ref[:, sl].astype(F32))).astype(o_ref.dtype)


def _gdn(h, hs, conv_w, par, norm_w, bsz, seq):
    ts = min(GDN_TILE, seq)
    ns = seq // ts
    tok = lambda col: pl.BlockSpec((ts, HB), lambda b, s: (b * ns + s, col))
    full = lambda shape: pl.BlockSpec(shape, lambda b, s: (0,) * len(shape))
    nc = ts // CHUNK
    return pl.pallas_call(
        _gdn_body,
        grid=(bsz, ns),
        in_specs=[tok(COL_GQ), tok(COL_GK), tok(COL_GV), tok(COL_GZ),
                  pl.BlockSpec((ts, LANES), lambda b, s: (b * ns + s, 0)),
                  full((CONV_WIDTH, 3 * HB)), full((SUBLANES, LANES)), full((1, GDN_DV))],
        out_specs=pl.BlockSpec((ts, HB), lambda b, s: (b * ns + s, 0)),
        out_shape=jax.ShapeDtypeStruct((bsz * seq, HB), BF16),
        scratch_shapes=[
            pltpu.VMEM((3, HALO, HB), BF16),
            pltpu.VMEM((ts + HALO, HB), BF16),
            pltpu.VMEM((ts, HB), F32), pltpu.VMEM((ts, HB), F32), pltpu.VMEM((ts, HB), F32),
            pltpu.VMEM((GDN_HEADS, nc, GDN_DK, GDN_DK), BF16), pltpu.VMEM((GDN_HEADS, nc, GDN_DV, GDN_DK), F32),
            pltpu.VMEM((GDN_HEADS, nc, GDN_DK, CHUNK), BF16), pltpu.VMEM((GDN_HEADS, nc, GDN_DV, CHUNK), F32),
            pltpu.VMEM((GDN_HEADS, nc, GDN_DV, CHUNK), F32),
            pltpu.VMEM((GDN_HEADS, nc, SUBLANES, LANES), F32),
            pltpu.VMEM((GDN_HEADS, GDN_DV, GDN_DK), F32),
        ],
        compiler_params=_cparams("parallel", "arbitrary"),
        name="gdn",
    )(h, h, h, h, hs, conv_w, par, norm_w)


def _expand_heads(x, expand_hi):
    hi = x.astype(BF16)
    r1 = x - hi.astype(F32)
    mid = r1.astype(BF16)
    lo = (r1 - mid.astype(F32)).astype(BF16)
    e = expand_hi
    return (jnp.dot(hi, e, preferred_element_type=F32) + jnp.dot(mid, e, preferred_element_type=F32)
            + jnp.dot(lo, e, preferred_element_type=F32))


def _ssd_body(z_ref, xbc_ref, hs_ref, cw_ref, cb_ref, par_ref, dskip_ref, nw_ref, o_ref,
              tail_ref, xbuf_ref, xs_ref, bc_ref, yd_ref, hin_ref, state_ref):
    ts = z_ref.shape[0]
    nc = ts // CHUNK
    gw = SSM_INNER // SSM_GROUPS

    @pl.when(pl.program_id(1) == 0)
    def _():
        tail_ref[...] = jnp.zeros_like(tail_ref)
        state_ref[...] = jnp.zeros_like(state_ref)

    taps = _shift_taps(nc)
    for slot, dst in enumerate((xs_ref, bc_ref)):
        cols = slice(slot * HB, (slot + 1) * HB)
        dst[...] = _conv_silu(xbc_ref[:, cols], tail_ref, slot, xbuf_ref, cw_ref[:, cols], cb_ref[:, cols], taps)

    hs = hs_ref[...]
    dt = _softplus(hs + par_ref[1:2, :])
    a_all = -jnp.exp(par_ref[0:1, :]) * dt
    r, c = _chunk_iota()
    tril = _mask(r >= c)
    lincl = jnp.broadcast_to(tril.astype(BF16), (nc, CHUNK, CHUNK))
    a3 = a_all.reshape(nc, CHUNK, LANES)
    ahi = a3.astype(BF16)
    ar = a3 - ahi.astype(F32)
    amid = ar.astype(BF16)
    alo = (ar - amid.astype(F32)).astype(BF16)
    acum = _bdot(lincl, ahi) + _bdot(lincl, amid) + _bdot(lincl, alo)
    lane = lax.broadcasted_iota(jnp.int32, (LANES, SSM_INNER), 0)
    chan = lax.broadcasted_iota(jnp.int32, (LANES, SSM_INNER), 1)
    expand = jnp.where(lane - SSM_HEADS == chan // SSM_HEADDIM, 1.0, 0.0).astype(BF16)
    dt_x = _expand_heads(dt, expand)
    acum_x = _expand_heads(acum.reshape(ts, LANES), expand).reshape(nc, CHUNK, SSM_INNER)
    a_last_x = acum_x[:, CHUNK - 1:CHUNK, :]
    ea_x = jnp.exp(acum_x).reshape(ts, SSM_INNER)
    din_x = jnp.exp(a_last_x - acum_x).reshape(ts, SSM_INNER)
    cd_x = jnp.exp(a_last_x)

    xs = xs_ref[...]
    xdt = xs * dt_x
    xw = xdt * din_x
    for g in range(SSM_GROUPS):
        gcols = slice(g * gw, (g + 1) * gw)
        b3 = bc_ref[:, g * SSM_STATE:(g + 1) * SSM_STATE].reshape(nc, CHUNK, SSM_STATE)
        c3 = bc_ref[:, gw + g * SSM_STATE:gw + (g + 1) * SSM_STATE].reshape(nc, CHUNK, SSM_STATE)
        cb = _bdot_nt(c3, b3)
        hpg = SSM_HEADS // SSM_GROUPS
        heads = [g * hpg + hr for hr in range(hpg)]
        ah3 = jnp.concatenate([a_all[:, SSM_HEADS + hd:SSM_HEADS + hd + 1].reshape(nc, CHUNK, 1) for hd in heads],
                              axis=0)
        x4 = jnp.concatenate([xdt[:, hd * SSM_HEADDIM:(hd + 1) * SSM_HEADDIM].reshape(nc, CHUNK, SSM_HEADDIM)
                              for hd in heads], axis=0)
        seg = jnp.exp(_segment_logdecay(ah3, r, c)) * tril
        yd = _bdot(seg * jnp.concatenate([cb] * hpg, axis=0), x4)
        for hr, hd in enumerate(heads):
            yd_ref[:, hd * SSM_HEADDIM:(hd + 1) * SSM_HEADDIM] = yd[hr * nc:(hr + 1) * nc].reshape(ts, SSM_HEADDIM)
        bt = jnp.swapaxes(b3, 1, 2)
        st = _bdot(bt, xw[:, gcols].reshape(nc, CHUNK, gw))
        hcur = state_ref[g]
        for ci in range(nc):
            hin_ref[ci] = hcur
            hcur = hcur * cd_x[ci, 0:1, gcols] + st[ci]
        state_ref[g] = hcur
        y_off = _bdot(c3, hin_ref[...]).reshape(ts, gw) * ea_x[:, gcols]
        y = yd_ref[:, gcols] + y_off + dskip_ref[:, gcols] * xs[:, gcols]
        gated = y * _silu(z_ref[:, gcols].astype(F32))
        o_ref[:, gcols] = _rms(gated, nw_ref[:, gcols]).astype(o_ref.dtype)


def _ssd(h, hs, conv_w, conv_b, par, dskip, norm_w, bsz, seq):
    ts = min(SEQ_TILE, seq)
    ns = seq // ts
    nc = ts // CHUNK
    gw = SSM_INNER // SSM_GROUPS
    full = lambda shape: pl.BlockSpec(shape, lambda b, s: (0,) * len(shape))
    return pl.pallas_call(
        _ssd_body,
        grid=(bsz, ns),
        in_specs=[pl.BlockSpec((ts, HB), lambda b, s: (b * ns + s, COL_SZ)),
                  pl.BlockSpec((ts, 2 * HB), lambda b, s: (b * ns + s, COL_SXBC)),
                  pl.BlockSpec((ts, LANES), lambda b, s: (b * ns + s, 0)),
                  full((CONV_WIDTH, 2 * HB)), full((1, 2 * HB)), full((SUBLANES, LANES)),
                  full((1, SSM_INNER)), full((1, SSM_INNER))],
        out_specs=pl.BlockSpec((ts, HB), lambda b, s: (b * ns + s, 0)),
        out_shape=jax.ShapeDtypeStruct((bsz * seq, HB), BF16),
        scratch_shapes=[
            pltpu.VMEM((2, HALO, HB), BF16),
            pltpu.VMEM((ts + HALO, HB), BF16),
            pltpu.VMEM((ts, HB), F32), pltpu.VMEM((ts, HB), F32), pltpu.VMEM((ts, HB), F32),
            pltpu.VMEM((nc, SSM_STATE, gw), F32),
            pltpu.VMEM((SSM_GROUPS, SSM_STATE, gw), F32),
        ],
        compiler_params=_cparams("parallel", "arbitrary"),
        name="ssd",
    )(h, h, hs, conv_w, conv_b, par, dskip, norm_w)


def _rope(x, cos, sin_signed):
    width = x.shape[-1]
    half = DIFF_DQK // 2
    lane = lax.broadcasted_iota(jnp.int32, x.shape, x.ndim - 1)
    fwd = pltpu.roll(x, width - half, axis=x.ndim - 1)
    bwd = pltpu.roll(x, half, axis=x.ndim - 1)
    partner = jnp.where((lane & half) == 0, fwd, bwd)
    return x * cos + partner * sin_signed


def _diff_body(q_ref, k_ref, v_ref, cos_ref, sin_ref, lam_ref, nw_ref, o_ref, kr_ref, *, lambda_init, nq):
    qi = pl.program_id(2)
    tq = q_ref.shape[0]
    q0 = pl.multiple_of(qi * tq, tq)
    cos = cos_ref[pl.ds(q0, tq), :]
    sin = sin_ref[pl.ds(q0, tq), :]
    kr_ref[pl.ds(q0, tq), :] = _rope(k_ref[...].astype(F32), cos, sin).astype(BF16)
    qr = _rope(q_ref[...].astype(F32), cos, sin) * (DIFF_DQK ** -0.5 * LOG2_E)

    lp = lam_ref[...]
    prod01 = jnp.sum(lp[0:1, :] * lp[1:2, :], axis=-1, keepdims=True)
    prod23 = jnp.sum(lp[2:3, :] * lp[3:4, :], axis=-1, keepdims=True)
    lam = jnp.exp(prod01) - jnp.exp(prod23) + lambda_init

    lane = lax.broadcasted_iota(jnp.int32, (tq, 2 * DIFF_DQK), 1)
    qmaps = (jnp.where(lane < DIFF_DQK, qr, 0.0).astype(BF16), jnp.where(lane >= DIFF_DQK, qr, 0.0).astype(BF16))
    th = tq // DIFF_ROW_SPLIT

    for i in range(nq):

        @pl.when(qi == i)
        def _(i=i):
            lo = i * tq
            for rh in range(DIFF_ROW_SPLIT):
                rows = slice(rh * th, (rh + 1) * th)
                own = (rh + 1) * th
                row = lax.broadcasted_iota(jnp.int32, (th, own), 0) + rh * th
                col = lax.broadcasted_iota(jnp.int32, (th, own), 1)
                diag_ok = (row // CHUNK) >= (col // CHUNK)
                outs = []
                for qmap in qmaps:
                    qm = qmap[rows, :]
                    s_diag = jnp.where(diag_ok, _dot_nt(qm, kr_ref[lo:lo + own, :]), -jnp.inf)
                    mx = jnp.max(s_diag, axis=-1, keepdims=True)
                    if i > 0:
                        s_off = _dot_nt(qm, kr_ref[0:lo, :])
                        mx = jnp.maximum(mx, jnp.max(s_off, axis=-1, keepdims=True))
                    p_diag = jnp.exp2(s_diag - mx)
                    denom = jnp.sum(p_diag, axis=-1, keepdims=True)
                    acc = _dot(p_diag, v_ref[lo:lo + own, :])
                    if i > 0:
                        p_off = jnp.exp2(s_off - mx)
                        denom = denom + jnp.sum(p_off, axis=-1, keepdims=True)
                        acc = acc + _dot(p_off, v_ref[0:lo, :])
                    outs.append(acc / denom)
                o = outs[0] - lam * outs[1]
                o_ref[rows, :] = (_rms(o, nw_ref[...]) * (1.0 - lambda_init)).astype(o_ref.dtype)


def _diff_attention(h, cos, sin, lam_params, norm_w, lambda_init, bsz, seq):
    tq = min(SEQ_TILE, seq)
    nq = seq // tq
    per_head = HB // DIFF_DV
    full = lambda shape: pl.BlockSpec(shape, lambda b, hd, q: (0,) * len(shape))
    tok = lambda col: pl.BlockSpec((tq, DIFF_DV), lambda b, hd, q: (b * nq + q, col * per_head + hd))
    return pl.pallas_call(
        functools.partial(_diff_body, lambda_init=lambda_init, nq=nq),
        grid=(bsz, DIFF_HEADS, nq),
        in_specs=[tok(COL_DQ), tok(COL_DK),
                  pl.BlockSpec((seq, DIFF_DV), lambda b, hd, q: (b, COL_DV * per_head + hd)),
                  full((seq, DIFF_DV)), full((seq, DIFF_DV)),
                  full((4, DIFF_DQK)), full((1, DIFF_DV))],
        out_specs=pl.BlockSpec((tq, DIFF_DV), lambda b, hd, q: (b * nq + q, hd)),
        out_shape=jax.ShapeDtypeStruct((bsz * seq, HB), BF16),
        scratch_shapes=[pltpu.VMEM((seq, DIFF_DV), BF16)],
        compiler_params=_cparams("parallel", "parallel", "arbitrary"),
        name="diff_attn",
    )(h, h, h, cos, sin, lam_params, norm_w)


def _mem_body(q_ref, kv_ref, o_ref):
    width = MEM_HEADS * MEM_HEADDIM
    for h in range(MEM_HEADS):
        cols = slice(h * MEM_HEADDIM, (h + 1) * MEM_HEADDIM)
        q = q_ref[:, cols].astype(F32) * (MEM_HEADDIM ** -0.5)
        s = _dot_nt(q, kv_ref[:, cols])
        m = jnp.max(s, axis=-1, keepdims=True)
        p = jnp.exp(s - m)
        p = p / jnp.sum(p, axis=-1, keepdims=True)
        o_ref[:, cols] = _dot(p, kv_ref[:, width + h * MEM_HEADDIM:width + (h + 1) * MEM_HEADDIM]).astype(o_ref.dtype)


def _mem_attention(h, kv, bsz, seq, mem_len):
    tm = min(ROW_TILE, seq)
    ns = seq // tm
    return pl.pallas_call(
        _mem_body,
        grid=(bsz, ns),
        in_specs=[pl.BlockSpec((tm, HB), lambda b, s: (b * ns + s, COL_MQ)),
                  pl.BlockSpec((mem_len, 2 * HB), lambda b, s: (b, 0))],
        out_specs=pl.BlockSpec((tm, HB), lambda b, s: (b * ns + s, 0)),
        out_shape=jax.ShapeDtypeStruct((bsz * seq, HB), BF16),
        compiler_params=_cparams("parallel", "parallel"),
        name="mem_attn",
    )(h, kv)


def _pack_halves(x):
    n = x.shape[1] // 2
    lo = pltpu.bitcast(x[:, :n].astype(BF16).astype(F32), U32)
    hi = pltpu.bitcast(x[:, n:].astype(BF16).astype(F32), U32)
    return (lo >> 16) | hi


def _unpack_halves(w):
    lo = pltpu.bitcast(w << 16, F32)
    hi = pltpu.bitcast(w & jnp.uint32(0xFFFF0000), F32)
    return lo.astype(BF16), hi.astype(BF16)


def _merge_body(g_ref, y0_ref, y1_ref, y2_ref, y3_ref, x_ref, wb_ref, wo_ref, lng_ref, lnb_ref,
                rwh_ref, rwl_ref, rb_ref, xo_ref, pa_ref, pb_ref, route_ref, cnt_ref):
    tm = x_ref.shape[0]
    merged = jnp.zeros((tm, D_MODEL), F32)
    for i, y_ref in enumerate((y0_ref, y1_ref, y2_ref, y3_ref)):
        gate = _sigmoid(g_ref[:, i * D_MODEL:(i + 1) * D_MODEL].astype(F32))
        merged = merged + gate * jnp.dot(y_ref[...], wb_ref[i], preferred_element_type=F32)
    mix = _dot(merged, wo_ref[...])
    xn = _layer_norm(DEEPNORM_ALPHA * x_ref[...] + mix, lng_ref[...], lnb_ref[...])
    xo_ref[...] = xn
    pa_ref[...] = _pack_halves(xn[:, :D_MODEL // 2])
    pb_ref[...] = _pack_halves(xn[:, D_MODEL // 2:])

    xh, xl = _split2(xn)
    logits = (_dot_nt(rwh_ref[...], xh) + (_dot_nt(rwl_ref[...], xh) + _dot_nt(rwh_ref[...], xl)))[:N_EXPERTS, :]
    work = logits + rb_ref[0:N_EXPERTS, 0:1]
    expert = lax.broadcasted_iota(jnp.int32, (N_EXPERTS, tm), 0)
    vals, ids, sels = [], [], []
    for _ in range(TOP_K):
        mx = jnp.max(work, axis=0, keepdims=True)
        idx = jnp.min(jnp.where(work == mx, expert, N_EXPERTS), axis=0, keepdims=True)
        sel = expert == idx
        vals.append(mx)
        ids.append(idx)
        sels.append(sel)
        work = jnp.where(sel, -jnp.inf, work)
    exps = [jnp.exp(v - vals[0]) for v in vals]
    denom = exps[0] + exps[1] + exps[2] + exps[3]
    chosen = jnp.where(sels[0] | sels[1] | sels[2] | sels[3], 1.0, 0.0)
    r = lax.broadcasted_iota(jnp.int32, (tm, tm), 0)
    c = lax.broadcasted_iota(jnp.int32, (tm, tm), 1)
    utri = jnp.where(r <= c, 1.0, 0.0).astype(BF16)
    cum = jnp.dot(chosen.astype(BF16), utri, preferred_element_type=F32)
    ranks = [jnp.sum(jnp.where(sels[k], cum, 0.0), axis=0, keepdims=True) - 1.0 for k in range(TOP_K)]
    rows = [e / denom for e in exps] + [i.astype(F32) for i in ids] + ranks
    rows.append(jnp.zeros((ROUTE_ROWS - len(rows), tm), F32))
    route_ref[...] = jnp.concatenate(rows, axis=0)
    cnt_ref[0] = jnp.broadcast_to(cum[:, tm - 1:tm], (N_EXPERTS, LANES))


def _merge(h, ys, x, wb, wo, lng, lnb, rwh, rwl, rb):
    t = x.shape[0]
    tm = min(ROW_TILE, t)
    nt = t // tm
    tok = lambda width: pl.BlockSpec((tm, width), lambda i: (i, 0))
    full = lambda shape: pl.BlockSpec(shape, lambda i: (0,) * len(shape))
    return pl.pallas_call(
        _merge_body,
        grid=(nt,),
        in_specs=[tok(N_BRANCH * D_MODEL), tok(HB), tok(HB), tok(HB), tok(HB), tok(D_MODEL),
                  full((N_BRANCH, HB, D_MODEL)), full((D_MODEL, D_MODEL)), full((1, D_MODEL)), full((1, D_MODEL)),
                  full((LANES, D_MODEL)), full((LANES, D_MODEL)), full((LANES, 1))],
        out_specs=[tok(D_MODEL), tok(SC_WORDS), tok(SC_WORDS), pl.BlockSpec((ROUTE_ROWS, tm), lambda i: (0, i)),
                   pl.BlockSpec((1, N_EXPERTS, LANES), lambda i: (i, 0, 0))],
        out_shape=[jax.ShapeDtypeStruct((t, D_MODEL), F32), jax.ShapeDtypeStruct((t, SC_WORDS), U32),
                   jax.ShapeDtypeStruct((t, SC_WORDS), U32), jax.ShapeDtypeStruct((ROUTE_ROWS, t), F32),
                   jax.ShapeDtypeStruct((nt, N_EXPERTS, LANES), F32)],
        compiler_params=_cparams("parallel"),
        name="merge_ln_router",
    )(h, *ys, x, wb, wo, lng, lnb, rwh, rwl, rb)


def _sc_mesh():
    return plsc.VectorSubcoreMesh(core_axis_name="core", subcore_axis_name="subcore")


def _sc_dispatch(xa, xb, pos, n_out):
    t, words = xa.shape
    k = pos.shape[0]
    out = jax.ShapeDtypeStruct((n_out, words), xa.dtype)

    @functools.partial(pl.kernel, out_type=(out, out), mesh=_sc_mesh())
    def scatter_rows(xa_hbm, xb_hbm, i_hbm, oa_hbm, ob_hbm):
        for x_hbm, o_hbm in ((xa_hbm, oa_hbm), (xb_hbm, ob_hbm)):
            def body(x_vmem, i_vmem, o_hbm=o_hbm):
                for kk in range(k):
                    pltpu.sync_copy(x_vmem, o_hbm.at[i_vmem.at[kk]])

            pltpu.emit_pipeline(
                body, grid=(t // SC_WINDOW,),
                in_specs=[pl.BlockSpec((SC_WINDOW, words), lambda i: (i, 0)),
                          pl.BlockSpec((k, SC_WINDOW), lambda i: (0, i))],
                out_specs=[],
                core_axis_name=("core", "subcore"),
                dimension_semantics=(pltpu.PARALLEL,),
            )(x_hbm, i_hbm)

    return scatter_rows(xa, xb, pos)


def _sc_collect(ya, yb, pos_flat):
    n = pos_flat.shape[1]
    words = ya.shape[1]
    out = jax.ShapeDtypeStruct((n, words), ya.dtype)

    @functools.partial(pl.kernel, out_type=(out, out), mesh=_sc_mesh())
    def gather_rows(ya_hbm, yb_hbm, i_hbm, oa_hbm, ob_hbm):
        for y_hbm, o_hbm in ((ya_hbm, oa_hbm), (yb_hbm, ob_hbm)):
            def body(i_vmem, o_vmem, y_hbm=y_hbm):
                pltpu.sync_copy(y_hbm.at[i_vmem.at[0]], o_vmem)

            pltpu.emit_pipeline(
                body, grid=(n // SC_WINDOW,),
                in_specs=[pl.BlockSpec((1, SC_WINDOW), lambda i: (0, i))],
                out_specs=[pl.BlockSpec((SC_WINDOW, words), lambda i: (i, 0))],
                core_axis_name=("core", "subcore"),
                dimension_semantics=(pltpu.PARALLEL,),
            )(i_hbm, o_hbm)

    return gather_rows(ya, yb, pos_flat)


def _expert_body(te_ref, xa_ref, xb_ref, wgl_ref, bgl_ref, wd_ref, bd_ref, ya_ref, yb_ref):
    pl.when(pl.program_id(0) < te_ref[pl.num_programs(0)])(
        functools.partial(_expert_tile, xa_ref, xb_ref, wgl_ref, bgl_ref, wd_ref, bd_ref, ya_ref, yb_ref))


def _expert_tile(xa_ref, xb_ref, wgl_ref, bgl_ref, wd_ref, bd_ref, ya_ref, yb_ref):
    x0, x1 = _unpack_halves(xa_ref[...])
    x2, x3 = _unpack_halves(xb_ref[...])
    x = jnp.concatenate([x0, x1, x2, x3], axis=1)
    gl = jnp.dot(x, wgl_ref[0], preferred_element_type=F32) + bgl_ref[0]
    nblk = 2 * D_FF // DEINT_BLOCK
    half = DEINT_BLOCK // 2
    glu = jnp.concatenate([gl[:, b * DEINT_BLOCK:b * DEINT_BLOCK + half] for b in range(nblk)], axis=1)
    lin = jnp.concatenate([gl[:, b * DEINT_BLOCK + half:(b + 1) * DEINT_BLOCK] for b in range(nblk)], axis=1)
    glu = jnp.minimum(glu, SWIGLU_LIMIT)
    lin = jnp.clip(lin, -SWIGLU_LIMIT, SWIGLU_LIMIT)
    act = (lin + 1.0) * glu * _sigmoid(SWIGLU_ALPHA * glu)
    y = _dot(act, wd_ref[0]) + bd_ref[0]
    ya_ref[...] = _pack_halves(y[:, :D_MODEL // 2])
    yb_ref[...] = _pack_halves(y[:, D_MODEL // 2:])


def _expert_mlp(tile_expert, xa, xb, wgl, bgl, wd, bd, first):
    n = xa.shape[0]
    tr = EXPERT_TILE
    tok = pl.BlockSpec((tr, SC_WORDS), lambda i, te: (i, 0))
    per_expert = lambda shape: pl.BlockSpec((1,) + shape, lambda i, te: (te[i] + first,) + (0,) * len(shape))
    out = jax.ShapeDtypeStruct((n, SC_WORDS), U32)
    return pl.pallas_call(
        _expert_body,
        grid_spec=pltpu.PrefetchScalarGridSpec(
            num_scalar_prefetch=1,
            grid=(n // tr,),
            in_specs=[tok, tok, per_expert((D_MODEL, 2 * D_FF)), per_expert((1, 2 * D_FF)),
                      per_expert((D_FF, D_MODEL)), per_expert((1, D_MODEL))],
            out_specs=[tok, tok],
        ),
        out_shape=[out, out],
        compiler_params=_cparams("arbitrary"),
        name="expert_mlp",
    )(tile_expert, xa, xb, wgl, bgl, wd, bd)


def _deinterleave_body(w_ref, p_ref, o_ref):
    for b in range(w_ref.shape[1] // DEINT_BLOCK):
        cols = slice(b * DEINT_BLOCK, (b + 1) * DEINT_BLOCK)
        o_ref[:, cols] = jnp.dot(w_ref[:, cols].astype(BF16), p_ref[...], preferred_element_type=F32).astype(o_ref.dtype)


def _deinterleave_perm():
    src = lax.broadcasted_iota(jnp.int32, (DEINT_BLOCK, DEINT_BLOCK), 0)
    dst = lax.broadcasted_iota(jnp.int32, (DEINT_BLOCK, DEINT_BLOCK), 1)
    half = DEINT_BLOCK // 2
    return jnp.where(src == jnp.where(dst < half, 2 * dst, 2 * (dst - half) + 1), 1.0, 0.0)


def _deinterleave(w):
    rows, n2 = w.shape
    tm = min(ROW_TILE, rows)
    return pl.pallas_call(
        _deinterleave_body,
        grid=(rows // tm,),
        in_specs=[pl.BlockSpec((tm, n2), lambda i: (i, 0)),
                  pl.BlockSpec((DEINT_BLOCK, DEINT_BLOCK), lambda i: (0, 0))],
        out_specs=pl.BlockSpec((tm, n2), lambda i: (i, 0)),
        out_shape=jax.ShapeDtypeStruct((rows, n2), BF16),
        compiler_params=_cparams("parallel"),
        name="deinterleave",
    )(w, _deinterleave_perm().astype(BF16))


def _combine_body(ya_ref, yb_ref, route_ref, x_ref, lng_ref, lnb_ref, xo_ref, xb_ref):
    x = x_ref[...]
    route = route_ref[...]
    half = D_MODEL // 2
    q = D_MODEL // 4
    ffn_a = jnp.zeros((x.shape[0], half), F32)
    ffn_b = jnp.zeros((x.shape[0], half), F32)
    for k in range(TOP_K):
        wk = route[:, k:k + 1]
        a0, a1 = _unpack_halves(ya_ref[k])
        b0, b1 = _unpack_halves(yb_ref[k])
        ffn_a = ffn_a + wk * jnp.concatenate([a0.astype(F32), a1.astype(F32)], axis=-1)
        ffn_b = ffn_b + wk * jnp.concatenate([b0.astype(F32), b1.astype(F32)], axis=-1)
    del q
    ffn = jnp.concatenate([ffn_a, ffn_b], axis=-1)
    xn = _layer_norm(DEEPNORM_ALPHA * x + ffn, lng_ref[...], lnb_ref[...])
    xo_ref[...] = xn
    xb_ref[...] = xn.astype(BF16)


def _combine(ya, yb, route, x, lng, lnb):
    t = x.shape[0]
    tm = min(ROW_TILE, t)
    tok = lambda width: pl.BlockSpec((tm, width), lambda i: (i, 0))
    gathered = pl.BlockSpec((TOP_K, tm, SC_WORDS), lambda i: (0, i, 0))
    full = lambda shape: pl.BlockSpec(shape, lambda i: (0,) * len(shape))
    return pl.pallas_call(
        _combine_body,
        grid=(t // tm,),
        in_specs=[gathered, gathered, tok(LANES), tok(D_MODEL), full((1, D_MODEL)), full((1, D_MODEL))],
        out_specs=[tok(D_MODEL), tok(D_MODEL)],
        out_shape=[jax.ShapeDtypeStruct((t, D_MODEL), F32), jax.ShapeDtypeStruct((t, D_MODEL), BF16)],
        compiler_params=_cparams("parallel"),
        name="combine_ln",
    )(ya, yb, route, x, lng, lnb)


def _route_positions(route, counts):
    t = route.shape[1]
    nt = counts.shape[0]
    tm = t // nt
    ids = route[TOP_K:2 * TOP_K].astype(jnp.int32).reshape(TOP_K, nt, tm)
    rank = route[2 * TOP_K:3 * TOP_K].astype(jnp.int32).reshape(TOP_K, nt, tm)
    cnt = counts[:, :, 0].astype(jnp.int32)
    before = jnp.cumsum(cnt, axis=0) - cnt
    total = jnp.sum(cnt, axis=0)
    padded = ((total + EXPERT_TILE - 1) // EXPERT_TILE) * EXPERT_TILE
    ends = jnp.cumsum(padded)
    offs = (ends - padded)[None, :] + before
    experts = jnp.arange(N_EXPERTS, dtype=jnp.int32)
    pos = rank + jnp.sum(jnp.where(ids[..., None] == experts, offs[None, :, None, :], 0), axis=-1)
    n_rows = TOP_K * t + N_EXPERTS * EXPERT_TILE
    tile_start = jnp.arange(n_rows // EXPERT_TILE, dtype=jnp.int32) * EXPERT_TILE
    tile_expert = jnp.minimum(jnp.sum((tile_start[:, None] >= ends[None, :]).astype(jnp.int32), axis=1), N_EXPERTS - 1)
    tiles_used = ends[N_EXPERTS - 1:] // EXPERT_TILE
    return pos.reshape(TOP_K, t), jnp.concatenate([tile_expert, tiles_used]), n_rows


def _rope_tables(seq):
    half = DIFF_DQK // 2
    inv_freq = ROPE_THETA ** (-jnp.arange(half, dtype=F32) / half)
    ang = jnp.arange(seq, dtype=F32)[:, None] * inv_freq[None, :]
    cos = jnp.cos(ang)
    sin = jnp.sin(ang)
    reps = DIFF_DV // DIFF_DQK
    cos_t = jnp.tile(jnp.concatenate([cos, cos], axis=-1), (1, reps))
    sin_t = jnp.tile(jnp.concatenate([-sin, sin], axis=-1), (1, reps))
    return cos_t, sin_t


def _lane_row(vec, offset):
    return jnp.zeros((LANES,), F32).at[offset:offset + vec.shape[0]].set(vec.astype(F32))


def _layer_weights(l, p):
    w_in = p['w_in'][l]
    w_big = jnp.concatenate([w_in[:, 5648:9744], w_in[:, 0:2048], w_in[:, 2056:5128], w_in[:, 5136:5648]],
                            axis=1).astype(BF16)
    w_small = jnp.concatenate([w_in[:, 2048:2056], w_in[:, 5128:5136],
                               jnp.zeros((D_MODEL, LANES - 16), F32)], axis=1).astype(BF16)
    par = jnp.zeros((SUBLANES, LANES), F32)
    gdn_par = par.at[0].set(_lane_row(p['gdn_a_log'][l], GDN_HEADS)).at[1].set(_lane_row(p['gdn_dt_bias'][l], GDN_HEADS))
    ssm_par = par.at[0].set(_lane_row(p['ssm_a_log'][l], SSM_HEADS)).at[1].set(_lane_row(p['ssm_dt_bias'][l], SSM_HEADS))
    rw = jnp.concatenate([p['router_w'][l].T, jnp.zeros((LANES - N_EXPERTS, D_MODEL), F32)], axis=0)
    rwh = rw.astype(BF16)
    bgu = p['b_gate_up'][l]
    return dict(
        w_big=w_big, w_small=w_small, gdn_par=gdn_par, ssm_par=ssm_par,
        gdn_conv_w=p['gdn_conv_w'][l], gdn_norm_w=p['gdn_norm_w'][l][None, :],
        lambda_init=0.8 - 0.6 * math.exp(-0.3 * l), diff_lambda=p['diff_lambda'][l],
        diff_norm_w=p['diff_norm_w'][l][None, :],
        ssm_conv_w=p['ssm_conv_w'][l], ssm_conv_b=p['ssm_conv_b'][l][None, :],
        dskip=jnp.repeat(p['ssm_d'][l], SSM_HEADDIM)[None, :], ssm_norm_w=p['ssm_norm_w'][l][None, :],
        w_mem=p['w_mem'][l].astype(BF16), w_branch=p['w_branch'][l].astype(BF16), w_out=p['w_out'][l].astype(BF16),
        ln1_g=p['ln1_g'][l][None, :], ln1_b=p['ln1_b'][l][None, :],
        rwh=rwh, rwl=(rw - rwh.astype(F32)).astype(BF16), rb=_lane_row(p['router_b'][l], 0)[:, None],
        first_expert=l * bgu.shape[0],
        ln2_g=p['ln2_g'][l][None, :], ln2_b=p['ln2_b'][l][None, :])


def _mix_and_route(w, x, xbf, kv_src, bsz, seq, cos, sin):
    t = bsz * seq
    h, hs = _in_proj(xbf, w['w_big'], w['w_small'], min(1024, t), H_WIDTH // 4)
    y_gdn = _gdn(h, hs, w['gdn_conv_w'], w['gdn_par'], w['gdn_norm_w'], bsz, seq)
    y_diff = _diff_attention(h, cos, sin, w['diff_lambda'], w['diff_norm_w'], w['lambda_init'], bsz, seq)
    y_ssm = _ssd(h, hs, w['ssm_conv_w'], w['ssm_conv_b'], w['ssm_par'], w['dskip'], w['ssm_norm_w'], bsz, seq)
    kv = _matmul(kv_src, w['w_mem'], BF16, min(1024, kv_src.shape[0]), HB, "mem_kv")
    y_mem = _mem_attention(h, kv, bsz, seq, kv_src.shape[0] // bsz)
    x1, xa, xb, route, counts = _merge(h, (y_gdn, y_diff, y_ssm, y_mem), x, w['w_branch'], w['w_out'],
                                       w['ln1_g'], w['ln1_b'], w['rwh'], w['rwl'], w['rb'])
    pos, tile_expert, n_rows = _route_positions(route, counts)
    xsa, xsb = _sc_dispatch(xa, xb, pos, n_rows)
    cw = jnp.zeros((t, LANES), F32).at[:, :TOP_K].set(route[:TOP_K].T)
    return dict(x1=x1, route=cw, pos=pos, tile_expert=tile_expert, xsa=xsa, xsb=xsb)


def _experts(w, ew, st):
    ysa, ysb = _expert_mlp(st['tile_expert'], st['xsa'], st['xsb'], ew['w_gate_lin'], ew['b_gate_lin'], ew['w_down'],
                           ew['b_down'], w['first_expert'])
    t = st['x1'].shape[0]
    ga, gb = _sc_collect(ysa, ysb, st['pos'].reshape(1, TOP_K * t))
    return dict(st, ga=ga.reshape(TOP_K, t, SC_WORDS), gb=gb.reshape(TOP_K, t, SC_WORDS))


def _finish(w, st):
    return _combine(st['ga'], st['gb'], st['route'], st['x1'], w['ln2_g'], w['ln2_b'])


def kernel(x, mem, w_in, gdn_conv_w, gdn_a_log, gdn_dt_bias, gdn_norm_w, diff_lambda, diff_norm_w, ssm_conv_w, ssm_conv_b, ssm_a_log, ssm_dt_bias, ssm_d, ssm_norm_w, w_mem, w_branch, w_out, ln1_g, ln1_b, router_w, router_b, w_gate_up, b_gate_up, w_down, b_down, ln2_g, ln2_b):
    p = dict(w_in=w_in, gdn_conv_w=gdn_conv_w, gdn_a_log=gdn_a_log, gdn_dt_bias=gdn_dt_bias, gdn_norm_w=gdn_norm_w,
             diff_lambda=diff_lambda, diff_norm_w=diff_norm_w, ssm_conv_w=ssm_conv_w, ssm_conv_b=ssm_conv_b,
             ssm_a_log=ssm_a_log, ssm_dt_bias=ssm_dt_bias, ssm_d=ssm_d, ssm_norm_w=ssm_norm_w, w_mem=w_mem,
             w_branch=w_branch, w_out=w_out, ln1_g=ln1_g, ln1_b=ln1_b, router_w=router_w, router_b=router_b,
             w_gate_up=w_gate_up, b_gate_up=b_gate_up, w_down=w_down, b_down=b_down, ln2_g=ln2_g, ln2_b=ln2_b)
    bsz, seq, d = x.shape
    depth, n_exp = w_gate_up.shape[0], w_gate_up.shape[1]
    expert_w = dict(
        w_gate_lin=_deinterleave(w_gate_up.reshape(depth * n_exp * d, 2 * D_FF)).reshape(depth * n_exp, d, 2 * D_FF),
        b_gate_lin=b_gate_up.reshape(depth * n_exp, 2 * D_FF // DEINT_BLOCK, DEINT_BLOCK // 2, 2)
        .transpose(0, 1, 3, 2).reshape(depth * n_exp, 1, 2 * D_FF),
        w_down=w_down.astype(BF16).reshape(depth * n_exp, D_FF, d),
        b_down=b_down.reshape(depth * n_exp, 1, d))
    cos, sin = _rope_tables(seq)
    groups = TOKEN_GROUPS if bsz % TOKEN_GROUPS == 0 else 1
    gb = bsz // groups
    xs = [x[g * gb:(g + 1) * gb].reshape(gb * seq, d) for g in range(groups)]
    xbfs = [v.astype(BF16) for v in xs]
    kvs = [mem[g * gb:(g + 1) * gb].reshape(gb * mem.shape[1], d).astype(BF16) for g in range(groups)]
    for l in range(depth):
        w = _layer_weights(l, p)
        sts = [_mix_and_route(w, xs[g], xbfs[g], kvs[g], gb, seq, cos, sin) for g in range(groups)]
        sts = [_experts(w, expert_w, st) for st in sts]
        outs = [_finish(w, st) for st in sts]
        xs = [o[0] for o in outs]
        xbfs = [o[1] for o in outs]
    return jnp.concatenate([v.reshape(gb, seq, d) for v in xs], axis=0)
```

```python
import functools
import math

import jax
import jax.numpy as jnp
from jax import lax
from jax.experimental import pallas as pl
from jax.experimental.pallas import tpu as pltpu
from jax.experimental.pallas import tpu_sc as plsc

F32 = jnp.float32
BF16 = jnp.bfloat16
U32 = jnp.uint32

D_MODEL = 1024
DEPTH = 4
CHUNK = 64
Q_BLOCK = 128
CONV_WIDTH = 4
ROPE_THETA = 10000.0
NORM_EPS = 1e-6
LN_EPS = 1e-5

GDN_HEADS = 4
GDN_DK = 128
GDN_DV = 128
DIFF_HEADS = 4
DIFF_DQK = 64
DIFF_DV = 128
SSM_HEADS = 8
SSM_HEADDIM = 64
SSM_GROUPS = 2
SSM_STATE = 128
SSM_INNER = 512
MEM_HEADS = 4
MEM_HEADDIM = 128
N_BRANCH = 4
N_EXPERTS = 32
TOP_K = 4
D_FF = 1024
SWIGLU_LIMIT = 7.0
SWIGLU_ALPHA = 1.702
DEEPNORM_ALPHA = (2.0 * DEPTH) ** 0.25
LOG2_E = math.log2(math.e)

LANES = 128
SUBLANES = 8
HALO = 16
VMEM_LIMIT = 56 * 1024 * 1024

HB = 512
COL_GATES = 0
COL_GQ, COL_GK, COL_GV, COL_GZ = 8, 9, 10, 11
COL_DQ, COL_DK, COL_DV = 12, 13, 14
COL_SZ = 15
COL_SXBC = 8
COL_MQ = 18
H_WIDTH = 19 * HB

SEQ_TILE = 512
ROW_TILE = 512
EXPERT_TILE = 512
SC_WINDOW = 128
SC_WORDS = 256
ROUTE_ROWS = 16
DEINT_BLOCK = 2 * LANES
GDN_TILE = 512
DIFF_ROW_SPLIT = 2
TOKEN_GROUPS = 2


def _cparams(*sem):
    return pltpu.CompilerParams(dimension_semantics=sem, vmem_limit_bytes=VMEM_LIMIT)


def _dot(a, b):
    return jnp.dot(a.astype(BF16), b.astype(BF16), preferred_element_type=F32)


def _dot_nt(a, b):
    return lax.dot_general(a.astype(BF16), b.astype(BF16), (((1,), (1,)), ((), ())), preferred_element_type=F32)


def _bdot(a, b):
    return jnp.einsum('cmk,ckn->cmn', a.astype(BF16), b.astype(BF16), preferred_element_type=F32)


def _bdot_nt(a, b):
    return jnp.einsum('cmk,cnk->cmn', a.astype(BF16), b.astype(BF16), preferred_element_type=F32)


def _split2(a):
    hi = a.astype(BF16)
    lo = (a - hi.astype(F32)).astype(BF16)
    return hi, lo


def _sigmoid(x):
    return 0.5 * jnp.tanh(0.5 * x) + 0.5


def _silu(x):
    return x * _sigmoid(x)


def _softplus(x):
    return jnp.maximum(x, 0.0) + jnp.log1p(jnp.exp(-jnp.abs(x)))


def _rms(x, w):
    return x * lax.rsqrt(jnp.mean(x * x, axis=-1, keepdims=True) + NORM_EPS) * w


def _layer_norm(x, g, b):
    mu = jnp.mean(x, axis=-1, keepdims=True)
    xc = x - mu
    var = jnp.mean(xc * xc, axis=-1, keepdims=True)
    return xc * lax.rsqrt(var + LN_EPS) * g + b


def _mm_body(x_ref, w_ref, o_ref):
    o_ref[...] = jnp.dot(x_ref[...], w_ref[...], preferred_element_type=F32).astype(o_ref.dtype)


def _matmul(x, w, out_dtype, tm, tn, name):
    m, k = x.shape
    n = w.shape[1]
    return pl.pallas_call(
        _mm_body,
        grid=(m // tm, n // tn),
        in_specs=[pl.BlockSpec((tm, k), lambda i, j: (i, 0)), pl.BlockSpec((k, tn), lambda i, j: (0, j))],
        out_specs=pl.BlockSpec((tm, tn), lambda i, j: (i, j)),
        out_shape=jax.ShapeDtypeStruct((m, n), out_dtype),
        compiler_params=_cparams("parallel", "parallel"),
        name=name,
    )(x, w)


def _in_proj_body(x_ref, w_ref, ws_ref, o_ref, os_ref):
    x = x_ref[...]
    o_ref[...] = jnp.dot(x, w_ref[...], preferred_element_type=F32).astype(o_ref.dtype)

    @pl.when(pl.program_id(1) == 0)
    def _():
        os_ref[...] = jnp.dot(x, ws_ref[...], preferred_element_type=F32)


def _in_proj(x, w, w_small, tm, tn):
    m, k = x.shape
    n = w.shape[1]
    return pl.pallas_call(
        _in_proj_body,
        grid=(m // tm, n // tn),
        in_specs=[pl.BlockSpec((tm, k), lambda i, j: (i, 0)), pl.BlockSpec((k, tn), lambda i, j: (0, j)),
                  pl.BlockSpec((k, LANES), lambda i, j: (0, 0))],
        out_specs=[pl.BlockSpec((tm, tn), lambda i, j: (i, j)), pl.BlockSpec((tm, LANES), lambda i, j: (i, 0))],
        out_shape=[jax.ShapeDtypeStruct((m, n), BF16), jax.ShapeDtypeStruct((m, LANES), F32)],
        compiler_params=_cparams("parallel", "arbitrary"),
        name="in_proj",
    )(x, w, w_small)


def _shift_taps(nc):
    m = lax.broadcasted_iota(jnp.int32, (1, (CONV_WIDTH - 1) * CHUNK, CHUNK + HALO), 1)
    k = lax.broadcasted_iota(jnp.int32, (1, (CONV_WIDTH - 1) * CHUNK, CHUNK + HALO), 2)
    src = (m % CHUNK) + HALO - (CONV_WIDTH - 1 - m // CHUNK)
    return jnp.broadcast_to(_mask(k == src).astype(BF16), (nc, (CONV_WIDTH - 1) * CHUNK, CHUNK + HALO))


def _conv_silu(x, tail_ref, slot, xbuf_ref, w, bias, taps):
    rows, width = x.shape
    nc = rows // CHUNK
    xbuf_ref[0:HALO, :] = tail_ref[slot]
    xbuf_ref[HALO:HALO + rows, :] = x
    tail_ref[slot] = x[rows - HALO:rows, :]
    xext = jnp.concatenate([xbuf_ref[i * CHUNK:(i + 1) * CHUNK + HALO, :][None] for i in range(nc)], axis=0)
    shifted = _bdot(taps, xext)
    y = x.astype(F32).reshape(nc, CHUNK, width) * w[CONV_WIDTH - 1:CONV_WIDTH, :]
    for j in range(CONV_WIDTH - 1):
        y = y + shifted[:, j * CHUNK:(j + 1) * CHUNK, :] * w[j:j + 1, :]
    y = y.reshape(rows, width)
    if bias is not None:
        y = y + bias
    return _silu(y)


def _chunk_iota():
    r = lax.broadcasted_iota(jnp.int32, (1, CHUNK, CHUNK), 1)
    c = lax.broadcasted_iota(jnp.int32, (1, CHUNK, CHUNK), 2)
    return r, c


def _mask(cond):
    return jnp.where(cond, 1.0, 0.0)


def _segment_logdecay(g3, r, c):
    nb = g3.shape[0]
    lincl = jnp.broadcast_to(_mask(r >= c).astype(BF16), (nb, CHUNK, CHUNK))
    upper = _mask(r > c)
    ghi = g3.astype(BF16).astype(F32)
    glo = g3 - ghi
    return _bdot(lincl, ghi * upper) + _bdot(lincl, glo * upper)


def _inv_unit_lower(a, r, c):
    d = _mask(r == c) - a * _mask(((r >> 1) == (c >> 1)) & (r > c))
    for lb in range(1, 6):
        lower_left = _mask(((r >> (lb + 1)) == (c >> (lb + 1))) & ((r >> lb) > (c >> lb)))
        d = d - _bdot(_bdot(d, a * lower_left), d)
    return d


def _gdn_body(q_ref, k_ref, v_ref, z_ref, hs_ref, cw_ref, par_ref, nw_ref, o_ref,
              tail_ref, xbuf_ref, qs_ref, ks_ref, vs_ref, pt_ref, bt_ref, qwt_ref, qut_ref, ot_ref, gl_ref,
              state_ref):
    ts = q_ref.shape[0]
    nc = ts // CHUNK

    @pl.when(pl.program_id(1) == 0)
    def _():
        tail_ref[...] = jnp.zeros_like(tail_ref)
        state_ref[...] = jnp.zeros_like(state_ref)

    taps = _shift_taps(nc)
    for slot, (src, dst) in enumerate(((q_ref, qs_ref), (k_ref, ks_ref), (v_ref, vs_ref))):
        w = cw_ref[:, slot * HB:(slot + 1) * HB]
        dst[...] = _conv_silu(src[...], tail_ref, slot, xbuf_ref, w, None, taps)

    hs = hs_ref[...]
    beta_all = _sigmoid(hs)
    g_all = -jnp.exp(par_ref[0:1, :]) * _softplus(hs + par_ref[1:2, :])

    def stack(per_head):
        return jnp.concatenate([per_head(h) for h in range(GDN_HEADS)], axis=0)

    def l2(ref, h):
        x = ref[:, h * GDN_DK:(h + 1) * GDN_DK]
        return (x * lax.rsqrt(jnp.sum(x * x, axis=-1, keepdims=True) + NORM_EPS)).reshape(nc, CHUNK, GDN_DK)

    q3 = stack(lambda h: l2(qs_ref, h)) * (GDN_DK ** -0.5)
    k3 = stack(lambda h: l2(ks_ref, h))
    v3 = stack(lambda h: vs_ref[:, h * GDN_DV:(h + 1) * GDN_DV].reshape(nc, CHUNK, GDN_DV))
    b3 = stack(lambda h: beta_all[:, h:h + 1].reshape(nc, CHUNK, 1))
    g3 = stack(lambda h: g_all[:, GDN_HEADS + h:GDN_HEADS + h + 1].reshape(nc, CHUNK, 1))
    r, c = _chunk_iota()
    m = _segment_logdecay(g3, r, c)
    em = jnp.exp(m)
    a = _bdot_nt(k3, k3) * b3 * (em * _mask(r > c))
    t = _inv_unit_lower(a, r, c)
    gc = m[:, :, 0:1] + g3[:, 0:1, :]
    eg = jnp.exp(gc)
    g_last = gc[:, CHUNK - 1:CHUNK, :]
    sol = _bdot(t, jnp.concatenate([v3 * b3, k3 * (b3 * eg)], axis=-1))
    u = sol[:, :, :GDN_DV]
    w = sol[:, :, GDN_DV:]
    kd = k3 * jnp.exp(g_last - gc)
    qk = _bdot_nt(q3, k3) * (em * _mask(r >= c))
    pt = _bdot(jnp.swapaxes(w, 1, 2), kd).astype(BF16)
    bt = _bdot(jnp.swapaxes(u, 1, 2), kd)
    qwt = jnp.swapaxes(q3 * eg - _bdot(qk, w), 1, 2).astype(BF16)
    qut = jnp.swapaxes(_bdot(qk, u), 1, 2)
    gl = jnp.broadcast_to(jnp.exp(g_last), (GDN_HEADS * nc, SUBLANES, LANES))
    for h in range(GDN_HEADS):
        hb = slice(h * nc, (h + 1) * nc)
        pt_ref[h] = pt[hb]
        bt_ref[h] = bt[hb]
        qwt_ref[h] = qwt[hb]
        qut_ref[h] = qut[hb]
        gl_ref[h] = gl[hb]

    def chunk_step(ci, carry):
        for h in range(GDN_HEADS):
            st = state_ref[h]
            stb = st.astype(BF16)
            ot_ref[h, ci] = jnp.dot(stb, qwt_ref[h, ci], preferred_element_type=F32) + qut_ref[h, ci]
            state_ref[h] = (st * gl_ref[h, ci][0:1, :] - jnp.dot(stb, pt_ref[h, ci], preferred_element_type=F32)
                            + bt_ref[h, ci])
        return carry

    lax.fori_loop(0, nc, chunk_step, 0)

    for h in range(GDN_HEADS):
        sl = slice(h * GDN_DV, (h + 1) * GDN_DV)
        o = jnp.swapaxes(ot_ref[h], 1, 2).reshape(ts, GDN_DV)
        o_ref[:, sl] = (_rms(o, nw_ref[...]) * _silu(z_ref[:, sl].astype(F32))).astype(o_ref.dtype)


def _gdn(h, hs, conv_w, par, norm_w, bsz, seq):
    ts = min(GDN_TILE, seq)
    ns = seq // ts
    tok = lambda col: pl.BlockSpec((ts, HB), lambda b, s: (b * ns + s, col))
    full = lambda shape: pl.BlockSpec(shape, lambda b, s: (0,) * len(shape))
    nc = ts // CHUNK
    return pl.pallas_call(
        _gdn_body,
        grid=(bsz, ns),
        in_specs=[tok(COL_GQ), tok(COL_GK), tok(COL_GV), tok(COL_GZ),
                  pl.BlockSpec((ts, LANES), lambda b, s: (b * ns + s, 0)),
                  full((CONV_WIDTH, 3 * HB)), full((SUBLANES, LANES)), full((1, GDN_DV))],
        out_specs=pl.BlockSpec((ts, HB), lambda b, s: (b * ns + s, 0)),
        out_shape=jax.ShapeDtypeStruct((bsz * seq, HB), BF16),
        scratch_shapes=[
            pltpu.VMEM((3, HALO, HB), BF16),
            pltpu.VMEM((ts + HALO, HB), BF16),
            pltpu.VMEM((ts, HB), F32), pltpu.VMEM((ts, HB), F32), pltpu.VMEM((ts, HB), F32),
            pltpu.VMEM((GDN_HEADS, nc, GDN_DK, GDN_DK), BF16), pltpu.VMEM((GDN_HEADS, nc, GDN_DV, GDN_DK), F32),
            pltpu.VMEM((GDN_HEADS, nc, GDN_DK, CHUNK), BF16), pltpu.VMEM((GDN_HEADS, nc, GDN_DV, CHUNK), F32),
            pltpu.VMEM((GDN_HEADS, nc, GDN_DV, CHUNK), F32),
            pltpu.VMEM((GDN_HEADS, nc, SUBLANES, LANES), F32),
            pltpu.VMEM((GDN_HEADS, GDN_DV, GDN_DK), F32),
        ],
        compiler_params=_cparams("parallel", "arbitrary"),
        name="gdn",
    )(h, h, h, h, hs, conv_w, par, norm_w)


def _expand_heads(x, expand_hi):
    hi = x.astype(BF16)
    r1 = x - hi.astype(F32)
    mid = r1.astype(BF16)
    lo = (r1 - mid.astype(F32)).astype(BF16)
    e = expand_hi
    return (jnp.dot(hi, e, preferred_element_type=F32) + jnp.dot(mid, e, preferred_element_type=F32)
            + jnp.dot(lo, e, preferred_element_type=F32))


def _ssd_body(z_ref, xbc_ref, hs_ref, cw_ref, cb_ref, par_ref, dskip_ref, nw_ref, o_ref,
              tail_ref, xbuf_ref, xs_ref, bc_ref, yd_ref, hin_ref, state_ref):
    ts = z_ref.shape[0]
    nc = ts // CHUNK
    gw = SSM_INNER // SSM_GROUPS

    @pl.when(pl.program_id(1) == 0)
    def _():
        tail_ref[...] = jnp.zeros_like(tail_ref)
        state_ref[...] = jnp.zeros_like(state_ref)

    taps = _shift_taps(nc)
    for slot, dst in enumerate((xs_ref, bc_ref)):
        cols = slice(slot * HB, (slot + 1) * HB)
        dst[...] = _conv_silu(xbc_ref[:, cols], tail_ref, slot, xbuf_ref, cw_ref[:, cols], cb_ref[:, cols], taps)

    hs = hs_ref[...]
    dt = _softplus(hs + par_ref[1:2, :])
    a_all = -jnp.exp(par_ref[0:1, :]) * dt
    r, c = _chunk_iota()
    tril = _mask(r >= c)
    lincl = jnp.broadcast_to(tril.astype(BF16), (nc, CHUNK, CHUNK))
    a3 = a_all.reshape(nc, CHUNK, LANES)
    ahi = a3.astype(BF16)
    ar = a3 - ahi.astype(F32)
    amid = ar.astype(BF16)
    alo = (ar - amid.astype(F32)).astype(BF16)
    acum = _bdot(lincl, ahi) + _bdot(lincl, amid) + _bdot(lincl, alo)
    lane = lax.broadcasted_iota(jnp.int32, (LANES, SSM_INNER), 0)
    chan = lax.broadcasted_iota(jnp.int32, (LANES, SSM_INNER), 1)
    expand = jnp.where(lane - SSM_HEADS == chan // SSM_HEADDIM, 1.0, 0.0).astype(BF16)
    dt_x = _expand_heads(dt, expand)
    acum_x = _expand_heads(acum.reshape(ts, LANES), expand).reshape(nc, CHUNK, SSM_INNER)
    a_last_x = acum_x[:, CHUNK - 1:CHUNK, :]
    ea_x = jnp.exp(acum_x).reshape(ts, SSM_INNER)
    din_x = jnp.exp(a_last_x - acum_x).reshape(ts, SSM_INNER)
    cd_x = jnp.exp(a_last_x)

    xs = xs_ref[...]
    xdt = xs * dt_x
    xw = xdt * din_x
    for g in range(SSM_GROUPS):
        gcols = slice(g * gw, (g + 1) * gw)
        b3 = bc_ref[:, g * SSM_STATE:(g + 1) * SSM_STATE].reshape(nc, CHUNK, SSM_STATE)
        c3 = bc_ref[:, gw + g * SSM_STATE:gw + (g + 1) * SSM_STATE].reshape(nc, CHUNK, SSM_STATE)
        cb = _bdot_nt(c3, b3)
        hpg = SSM_HEADS // SSM_GROUPS
        heads = [g * hpg + hr for hr in range(hpg)]
        ah3 = jnp.concatenate([a_all[:, SSM_HEADS + hd:SSM_HEADS + hd + 1].reshape(nc, CHUNK, 1) for hd in heads],
                              axis=0)
        x4 = jnp.concatenate([xdt[:, hd * SSM_HEADDIM:(hd + 1) * SSM_HEADDIM].reshape(nc, CHUNK, SSM_HEADDIM)
                              for hd in heads], axis=0)
        seg = jnp.exp(_segment_logdecay(ah3, r, c)) * tril
        yd = _bdot(seg * jnp.concatenate([cb] * hpg, axis=0), x4)
        for hr, hd in enumerate(heads):
            yd_ref[:, hd * SSM_HEADDIM:(hd + 1) * SSM_HEADDIM] = yd[hr * nc:(hr + 1) * nc].reshape(ts, SSM_HEADDIM)
        bt = jnp.swapaxes(b3, 1, 2)
        st = _bdot(bt, xw[:, gcols].reshape(nc, CHUNK, gw))
        hcur = state_ref[g]
        for ci in range(nc):
            hin_ref[ci] = hcur
            hcur = hcur * cd_x[ci, 0:1, gcols] + st[ci]
        state_ref[g] = hcur
        y_off = _bdot(c3, hin_ref[...]).reshape(ts, gw) * ea_x[:, gcols]
        y = yd_ref[:, gcols] + y_off + dskip_ref[:, gcols] * xs[:, gcols]
        gated = y * _silu(z_ref[:, gcols].astype(F32))
        o_ref[:, gcols] = _rms(gated, nw_ref[:, gcols]).astype(o_ref.dtype)


def _ssd(h, hs, conv_w, conv_b, par, dskip, norm_w, bsz, seq):
    ts = min(SEQ_TILE, seq)
    ns = seq // ts
    nc = ts // CHUNK
    gw = SSM_INNER // SSM_GROUPS
    full = lambda shape: pl.BlockSpec(shape, lambda b, s: (0,) * len(shape))
    return pl.pallas_call(
        _ssd_body,
        grid=(bsz, ns),
        in_specs=[pl.BlockSpec((ts, HB), lambda b, s: (b * ns + s, COL_SZ)),
                  pl.BlockSpec((ts, 2 * HB), lambda b, s: (b * ns + s, COL_SXBC)),
                  pl.BlockSpec((ts, LANES), lambda b, s: (b * ns + s, 0)),
                  full((CONV_WIDTH, 2 * HB)), full((1, 2 * HB)), full((SUBLANES, LANES)),
                  full((1, SSM_INNER)), full((1, SSM_INNER))],
        out_specs=pl.BlockSpec((ts, HB), lambda b, s: (b * ns + s, 0)),
        out_shape=jax.ShapeDtypeStruct((bsz * seq, HB), BF16),
        scratch_shapes=[
            pltpu.VMEM((2, HALO, HB), BF16),
            pltpu.VMEM((ts + HALO, HB), BF16),
            pltpu.VMEM((ts, HB), F32), pltpu.VMEM((ts, HB), F32), pltpu.VMEM((ts, HB), F32),
            pltpu.VMEM((nc, SSM_STATE, gw), F32),
            pltpu.VMEM((SSM_GROUPS, SSM_STATE, gw), F32),
        ],
        compiler_params=_cparams("parallel", "arbitrary"),
        name="ssd",
    )(h, h, hs, conv_w, conv_b, par, dskip, norm_w)


def _rope(x, cos, sin_signed):
    width = x.shape[-1]
    half = DIFF_DQK // 2
    lane = lax.broadcasted_iota(jnp.int32, x.shape, x.ndim - 1)
    fwd = pltpu.roll(x, width - half, axis=x.ndim - 1)
    bwd = pltpu.roll(x, half, axis=x.ndim - 1)
    partner = jnp.where((lane & half) == 0, fwd, bwd)
    return x * cos + partner * sin_signed


def _diff_body(q_ref, k_ref, v_ref, cos_ref, sin_ref, lam_ref, nw_ref, o_ref, kr_ref, *, lambda_init, nq):
    qi = pl.program_id(2)
    tq = q_ref.shape[0]
    q0 = pl.multiple_of(qi * tq, tq)
    cos = cos_ref[pl.ds(q0, tq), :]
    sin = sin_ref[pl.ds(q0, tq), :]
    kr_ref[pl.ds(q0, tq), :] = _rope(k_ref[...].astype(F32), cos, sin).astype(BF16)
    qr = _rope(q_ref[...].astype(F32), cos, sin) * (DIFF_DQK ** -0.5 * LOG2_E)

    lp = lam_ref[...]
    prod01 = jnp.sum(lp[0:1, :] * lp[1:2, :], axis=-1, keepdims=True)
    prod23 = jnp.sum(lp[2:3, :] * lp[3:4, :], axis=-1, keepdims=True)
    lam = jnp.exp(prod01) - jnp.exp(prod23) + lambda_init

    lane = lax.broadcasted_iota(jnp.int32, (tq, 2 * DIFF_DQK), 1)
    qmaps = (jnp.where(lane < DIFF_DQK, qr, 0.0).astype(BF16), jnp.where(lane >= DIFF_DQK, qr, 0.0).astype(BF16))
    th = tq // DIFF_ROW_SPLIT

    for i in range(nq):

        @pl.when(qi == i)
        def _(i=i):
            lo = i * tq
            for rh in range(DIFF_ROW_SPLIT):
                rows = slice(rh * th, (rh + 1) * th)
                own = (rh + 1) * th
                row = lax.broadcasted_iota(jnp.int32, (th, own), 0) + rh * th
                col = lax.broadcasted_iota(jnp.int32, (th, own), 1)
                diag_ok = (row // CHUNK) >= (col // CHUNK)
                outs = []
                for qmap in qmaps:
                    qm = qmap[rows, :]
                    s_diag = jnp.where(diag_ok, _dot_nt(qm, kr_ref[lo:lo + own, :]), -jnp.inf)
                    mx = jnp.max(s_diag, axis=-1, keepdims=True)
                    if i > 0:
                        s_off = _dot_nt(qm, kr_ref[0:lo, :])
                        mx = jnp.maximum(mx, jnp.max(s_off, axis=-1, keepdims=True))
                    p_diag = jnp.exp2(s_diag - mx)
                    denom = jnp.sum(p_diag, axis=-1, keepdims=True)
                    acc = _dot(p_diag, v_ref[lo:lo + own, :])
                    if i > 0:
                        p_off = jnp.exp2(s_off - mx)
                        denom = denom + jnp.sum(p_off, axis=-1, keepdims=True)
                        acc = acc + _dot(p_off, v_ref[0:lo, :])
                    outs.append(acc / denom)
                o = outs[0] - lam * outs[1]
                o_ref[rows, :] = (_rms(o, nw_ref[...]) * (1.0 - lambda_init)).astype(o_ref.dtype)


def _diff_attention(h, cos, sin, lam_params, norm_w, lambda_init, bsz, seq):
    tq = min(SEQ_TILE, seq)
    nq = seq // tq
    per_head = HB // DIFF_DV
    full = lambda shape: pl.BlockSpec(shape, lambda b, hd, q: (0,) * len(shape))
    tok = lambda col: pl.BlockSpec((tq, DIFF_DV), lambda b, hd, q: (b * nq + q, col * per_head + hd))
    return pl.pallas_call(
        functools.partial(_diff_body, lambda_init=lambda_init, nq=nq),
        grid=(bsz, DIFF_HEADS, nq),
        in_specs=[tok(COL_DQ), tok(COL_DK),
                  pl.BlockSpec((seq, DIFF_DV), lambda b, hd, q: (b, COL_DV * per_head + hd)),
                  full((seq, DIFF_DV)), full((seq, DIFF_DV)),
                  full((4, DIFF_DQK)), full((1, DIFF_DV))],
        out_specs=pl.BlockSpec((tq, DIFF_DV), lambda b, hd, q: (b * nq + q, hd)),
        out_shape=jax.ShapeDtypeStruct((bsz * seq, HB), BF16),
        scratch_shapes=[pltpu.VMEM((seq, DIFF_DV), BF16)],
        compiler_params=_cparams("parallel", "parallel", "arbitrary"),
        name="diff_attn",
    )(h, h, h, cos, sin, lam_params, norm_w)


def _mem_body(q_ref, kv_ref, o_ref):
    width = MEM_HEADS * MEM_HEADDIM
    for h in range(MEM_HEADS):
        cols = slice(h * MEM_HEADDIM, (h + 1) * MEM_HEADDIM)
        q = q_ref[:, cols].astype(F32) * (MEM_HEADDIM ** -0.5)
        s = _dot_nt(q, kv_ref[:, cols])
        m = jnp.max(s, axis=-1, keepdims=True)
        p = jnp.exp(s - m)
        p = p / jnp.sum(p, axis=-1, keepdims=True)
        o_ref[:, cols] = _dot(p, kv_ref[:, width + h * MEM_HEADDIM:width + (h + 1) * MEM_HEADDIM]).astype(o_ref.dtype)


def _mem_attention(h, kv, bsz, seq, mem_len):
    tm = min(ROW_TILE, seq)
    ns = seq // tm
    return pl.pallas_call(
        _mem_body,
        grid=(bsz, ns),
        in_specs=[pl.BlockSpec((tm, HB), lambda b, s: (b * ns + s, COL_MQ)),
                  pl.BlockSpec((mem_len, 2 * HB), lambda b, s: (b, 0))],
        out_specs=pl.BlockSpec((tm, HB), lambda b, s: (b * ns + s, 0)),
        out_shape=jax.ShapeDtypeStruct((bsz * seq, HB), BF16),
        compiler_params=_cparams("parallel", "parallel"),
        name="mem_attn",
    )(h, kv)


def _pack_halves(x):
    n = x.shape[1] // 2
    lo = pltpu.bitcast(x[:, :n].astype(BF16).astype(F32), U32)
    hi = pltpu.bitcast(x[:, n:].astype(BF16).astype(F32), U32)
    return (lo >> 16) | hi


def _unpack_halves(w):
    lo = pltpu.bitcast(w << 16, F32)
    hi = pltpu.bitcast(w & jnp.uint32(0xFFFF0000), F32)
    return lo.astype(BF16), hi.astype(BF16)


def _merge_body(g_ref, y0_ref, y1_ref, y2_ref, y3_ref, x_ref, wb_ref, wo_ref, lng_ref, lnb_ref,
                rwh_ref, rwl_ref, rb_ref, xo_ref, pa_ref, pb_ref, route_ref, cnt_ref):
    tm = x_ref.shape[0]
    merged = jnp.zeros((tm, D_MODEL), F32)
    for i, y_ref in enumerate((y0_ref, y1_ref, y2_ref, y3_ref)):
        gate = _sigmoid(g_ref[:, i * D_MODEL:(i + 1) * D_MODEL].astype(F32))
        merged = merged + gate * jnp.dot(y_ref[...], wb_ref[i], preferred_element_type=F32)
    mix = _dot(merged, wo_ref[...])
    xn = _layer_norm(DEEPNORM_ALPHA * x_ref[...] + mix, lng_ref[...], lnb_ref[...])
    xo_ref[...] = xn
    pa_ref[...] = _pack_halves(xn[:, :D_MODEL // 2])
    pb_ref[...] = _pack_halves(xn[:, D_MODEL // 2:])

    xh, xl = _split2(xn)
    logits = (_dot_nt(rwh_ref[...], xh) + (_dot_nt(rwl_ref[...], xh) + _dot_nt(rwh_ref[...], xl)))[:N_EXPERTS, :]
    work = logits + rb_ref[0:N_EXPERTS, 0:1]
    expert = lax.broadcasted_iota(jnp.int32, (N_EXPERTS, tm), 0)
    vals, ids, sels = [], [], []
    for _ in range(TOP_K):
        mx = jnp.max(work, axis=0, keepdims=True)
        idx = jnp.min(jnp.where(work == mx, expert, N_EXPERTS), axis=0, keepdims=True)
        sel = expert == idx
        vals.append(mx)
        ids.append(idx)
        sels.append(sel)
        work = jnp.where(sel, -jnp.inf, work)
    exps = [jnp.exp(v - vals[0]) for v in vals]
    denom = exps[0] + exps[1] + exps[2] + exps[3]
    chosen = jnp.where(sels[0] | sels[1] | sels[2] | sels[3], 1.0, 0.0)
    r = lax.broadcasted_iota(jnp.int32, (tm, tm), 0)
    c = lax.broadcasted_iota(jnp.int32, (tm, tm), 1)
    utri = jnp.where(r <= c, 1.0, 0.0).astype(BF16)
    cum = jnp.dot(chosen.astype(BF16), utri, preferred_element_type=F32)
    ranks = [jnp.sum(jnp.where(sels[k], cum, 0.0), axis=0, keepdims=True) - 1.0 for k in range(TOP_K)]
    rows = [e / denom for e in exps] + [i.astype(F32) for i in ids] + ranks
    rows.append(jnp.zeros((ROUTE_ROWS - len(rows), tm), F32))
    route_ref[...] = jnp.concatenate(rows, axis=0)
    cnt_ref[0] = jnp.broadcast_to(cum[:, tm - 1:tm], (N_EXPERTS, LANES))


def _merge(h, ys, x, wb, wo, lng, lnb, rwh, rwl, rb):
    t = x.shape[0]
    tm = min(ROW_TILE, t)
    nt = t // tm
    tok = lambda width: pl.BlockSpec((tm, width), lambda i: (i, 0))
    full = lambda shape: pl.BlockSpec(shape, lambda i: (0,) * len(shape))
    return pl.pallas_call(
        _merge_body,
        grid=(nt,),
        in_specs=[tok(N_BRANCH * D_MODEL), tok(HB), tok(HB), tok(HB), tok(HB), tok(D_MODEL),
                  full((N_BRANCH, HB, D_MODEL)), full((D_MODEL, D_MODEL)), full((1, D_MODEL)), full((1, D_MODEL)),
                  full((LANES, D_MODEL)), full((LANES, D_MODEL)), full((LANES, 1))],
        out_specs=[tok(D_MODEL), tok(SC_WORDS), tok(SC_WORDS), pl.BlockSpec((ROUTE_ROWS, tm), lambda i: (0, i)),
                   pl.BlockSpec((1, N_EXPERTS, LANES), lambda i: (i, 0, 0))],
        out_shape=[jax.ShapeDtypeStruct((t, D_MODEL), F32), jax.ShapeDtypeStruct((t, SC_WORDS), U32),
                   jax.ShapeDtypeStruct((t, SC_WORDS), U32), jax.ShapeDtypeStruct((ROUTE_ROWS, t), F32),
                   jax.ShapeDtypeStruct((nt, N_EXPERTS, LANES), F32)],
        compiler_params=_cparams("parallel"),
        name="merge_ln_router",
    )(h, *ys, x, wb, wo, lng, lnb, rwh, rwl, rb)


def _sc_mesh():
    return plsc.VectorSubcoreMesh(core_axis_name="core", subcore_axis_name="subcore")


def _sc_dispatch(xa, xb, pos, n_out):
    t, words = xa.shape
    k = pos.shape[0]
    out = jax.ShapeDtypeStruct((n_out, words), xa.dtype)

    @functools.partial(pl.kernel, out_type=(out, out), mesh=_sc_mesh())
    def scatter_rows(xa_hbm, xb_hbm, i_hbm, oa_hbm, ob_hbm):
        for x_hbm, o_hbm in ((xa_hbm, oa_hbm), (xb_hbm, ob_hbm)):
            def body(x_vmem, i_vmem, o_hbm=o_hbm):
                for kk in range(k):
                    pltpu.sync_copy(x_vmem, o_hbm.at[i_vmem.at[kk]])

            pltpu.emit_pipeline(
                body, grid=(t // SC_WINDOW,),
                in_specs=[pl.BlockSpec((SC_WINDOW, words), lambda i: (i, 0)),
                          pl.BlockSpec((k, SC_WINDOW), lambda i: (0, i))],
                out_specs=[],
                core_axis_name=("core", "subcore"),
                dimension_semantics=(pltpu.PARALLEL,),
            )(x_hbm, i_hbm)

    return scatter_rows(xa, xb, pos)


def _sc_collect(ya, yb, pos_flat):
    n = pos_flat.shape[1]
    words = ya.shape[1]
    out = jax.ShapeDtypeStruct((n, words), ya.dtype)

    @functools.partial(pl.kernel, out_type=(out, out), mesh=_sc_mesh())
    def gather_rows(ya_hbm, yb_hbm, i_hbm, oa_hbm, ob_hbm):
        for y_hbm, o_hbm in ((ya_hbm, oa_hbm), (yb_hbm, ob_hbm)):
            def body(i_vmem, o_vmem, y_hbm=y_hbm):
                pltpu.sync_copy(y_hbm.at[i_vmem.at[0]], o_vmem)

            pltpu.emit_pipeline(
                body, grid=(n // SC_WINDOW,),
                in_specs=[pl.BlockSpec((1, SC_WINDOW), lambda i: (0, i))],
                out_specs=[pl.BlockSpec((SC_WINDOW, words), lambda i: (i, 0))],
                core_axis_name=("core", "subcore"),
                dimension_semantics=(pltpu.PARALLEL,),
            )(i_hbm, o_hbm)

    return gather_rows(ya, yb, pos_flat)


def _expert_body(te_ref, xa_ref, xb_ref, wgu_ref, bgl_ref, wd_ref, bd_ref, perm_ref, ya_ref, yb_ref, wgl_s, wd_s):
    i = pl.program_id(0)
    live = i < te_ref[pl.num_programs(0)]
    fresh = jnp.logical_or(i == 0, te_ref[i] != te_ref[jnp.maximum(i - 1, 0)])

    @pl.when(jnp.logical_and(live, fresh))
    def _():
        for b in range(2 * D_FF // DEINT_BLOCK):
            cols = slice(b * DEINT_BLOCK, (b + 1) * DEINT_BLOCK)
            wgl_s[:, cols] = jnp.dot(wgu_ref[0, :, cols].astype(BF16), perm_ref[...],
                                     preferred_element_type=F32).astype(BF16)
        wd_s[...] = wd_ref[0].astype(BF16)

    pl.when(live)(functools.partial(_expert_tile, xa_ref, xb_ref, wgl_s, bgl_ref, wd_s, bd_ref, ya_ref, yb_ref))


def _expert_tile(xa_ref, xb_ref, wgl_s, bgl_ref, wd_s, bd_ref, ya_ref, yb_ref):
    x0, x1 = _unpack_halves(xa_ref[...])
    x2, x3 = _unpack_halves(xb_ref[...])
    x = jnp.concatenate([x0, x1, x2, x3], axis=1)
    gl = jnp.dot(x, wgl_s[...], preferred_element_type=F32) + bgl_ref[0]
    nblk = 2 * D_FF // DEINT_BLOCK
    half = DEINT_BLOCK // 2
    glu = jnp.concatenate([gl[:, b * DEINT_BLOCK:b * DEINT_BLOCK + half] for b in range(nblk)], axis=1)
    lin = jnp.concatenate([gl[:, b * DEINT_BLOCK + half:(b + 1) * DEINT_BLOCK] for b in range(nblk)], axis=1)
    glu = jnp.minimum(glu, SWIGLU_LIMIT)
    lin = jnp.clip(lin, -SWIGLU_LIMIT, SWIGLU_LIMIT)
    act = (lin + 1.0) * glu * _sigmoid(SWIGLU_ALPHA * glu)
    y = _dot(act, wd_s[...]) + bd_ref[0]
    ya_ref[...] = _pack_halves(y[:, :D_MODEL // 2])
    yb_ref[...] = _pack_halves(y[:, D_MODEL // 2:])


def _deinterleave_perm():
    src = lax.broadcasted_iota(jnp.int32, (DEINT_BLOCK, DEINT_BLOCK), 0)
    dst = lax.broadcasted_iota(jnp.int32, (DEINT_BLOCK, DEINT_BLOCK), 1)
    half = DEINT_BLOCK // 2
    return jnp.where(src == jnp.where(dst < half, 2 * dst, 2 * (dst - half) + 1), 1.0, 0.0)


def _expert_mlp(tile_expert, xa, xb, wgu, bgl, wd, bd, first):
    n = xa.shape[0]
    tr = EXPERT_TILE
    tok = pl.BlockSpec((tr, SC_WORDS), lambda i, te: (i, 0))
    per_expert = lambda shape: pl.BlockSpec((1,) + shape, lambda i, te: (te[i] + first,) + (0,) * len(shape))
    out = jax.ShapeDtypeStruct((n, SC_WORDS), U32)
    return pl.pallas_call(
        _expert_body,
        grid_spec=pltpu.PrefetchScalarGridSpec(
            num_scalar_prefetch=1,
            grid=(n // tr,),
            in_specs=[tok, tok, per_expert((D_MODEL, 2 * D_FF)), per_expert((1, 2 * D_FF)),
                      per_expert((D_FF, D_MODEL)), per_expert((1, D_MODEL)),
                      pl.BlockSpec((DEINT_BLOCK, DEINT_BLOCK), lambda i, te: (0, 0))],
            out_specs=[tok, tok],
            scratch_shapes=[pltpu.VMEM((D_MODEL, 2 * D_FF), BF16), pltpu.VMEM((D_FF, D_MODEL), BF16)],
        ),
        out_shape=[out, out],
        compiler_params=_cparams("arbitrary"),
        name="expert_mlp",
    )(tile_expert, xa, xb, wgu, bgl, wd, bd, _deinterleave_perm().astype(BF16))


def _combine_body(ya_ref, yb_ref, route_ref, x_ref, lng_ref, lnb_ref, xo_ref, xb_ref):
    x = x_ref[...]
    route = route_ref[...]
    half = D_MODEL // 2
    q = D_MODEL // 4
    ffn_a = jnp.zeros((x.shape[0], half), F32)
    ffn_b = jnp.zeros((x.shape[0], half), F32)
    for k in range(TOP_K):
        wk = route[:, k:k + 1]
        a0, a1 = _unpack_halves(ya_ref[k])
        b0, b1 = _unpack_halves(yb_ref[k])
        ffn_a = ffn_a + wk * jnp.concatenate([a0.astype(F32), a1.astype(F32)], axis=-1)
        ffn_b = ffn_b + wk * jnp.concatenate([b0.astype(F32), b1.astype(F32)], axis=-1)
    del q
    ffn = jnp.concatenate([ffn_a, ffn_b], axis=-1)
    xn = _layer_norm(DEEPNORM_ALPHA * x + ffn, lng_ref[...], lnb_ref[...])
    xo_ref[...] = xn
    xb_ref[...] = xn.astype(BF16)


def _combine(ya, yb, route, x, lng, lnb):
    t = x.shape[0]
    tm = min(ROW_TILE, t)
    tok = lambda width: pl.BlockSpec((tm, width), lambda i: (i, 0))
    gathered = pl.BlockSpec((TOP_K, tm, SC_WORDS), lambda i: (0, i, 0))
    full = lambda shape: pl.BlockSpec(shape, lambda i: (0,) * len(shape))
    return pl.pallas_call(
        _combine_body,
        grid=(t // tm,),
        in_specs=[gathered, gathered, tok(LANES), tok(D_MODEL), full((1, D_MODEL)), full((1, D_MODEL))],
        out_specs=[tok(D_MODEL), tok(D_MODEL)],
        out_shape=[jax.ShapeDtypeStruct((t, D_MODEL), F32), jax.ShapeDtypeStruct((t, D_MODEL), BF16)],
        compiler_params=_cparams("parallel"),
        name="combine_ln",
    )(ya, yb, route, x, lng, lnb)


def _route_positions(route, counts):
    t = route.shape[1]
    nt = counts.shape[0]
    tm = t // nt
    ids = route[TOP_K:2 * TOP_K].astype(jnp.int32).reshape(TOP_K, nt, tm)
    rank = route[2 * TOP_K:3 * TOP_K].astype(jnp.int32).reshape(TOP_K, nt, tm)
    cnt = counts[:, :, 0].astype(jnp.int32)
    before = jnp.cumsum(cnt, axis=0) - cnt
    total = jnp.sum(cnt, axis=0)
    padded = ((total + EXPERT_TILE - 1) // EXPERT_TILE) * EXPERT_TILE
    ends = jnp.cumsum(padded)
    offs = (ends - padded)[None, :] + before
    experts = jnp.arange(N_EXPERTS, dtype=jnp.int32)
    pos = rank + jnp.sum(jnp.where(ids[..., None] == experts, offs[None, :, None, :], 0), axis=-1)
    n_rows = TOP_K * t + N_EXPERTS * EXPERT_TILE
    tile_start = jnp.arange(n_rows // EXPERT_TILE, dtype=jnp.int32) * EXPERT_TILE
    tile_expert = jnp.minimum(jnp.sum((tile_start[:, None] >= ends[None, :]).astype(jnp.int32), axis=1), N_EXPERTS - 1)
    tiles_used = ends[N_EXPERTS - 1:] // EXPERT_TILE
    return pos.reshape(TOP_K, t), jnp.concatenate([tile_expert, tiles_used]), n_rows


def _rope_tables(seq):
    half = DIFF_DQK // 2
    inv_freq = ROPE_THETA ** (-jnp.arange(half, dtype=F32) / half)
    ang = jnp.arange(seq, dtype=F32)[:, None] * inv_freq[None, :]
    cos = jnp.cos(ang)
    sin = jnp.sin(ang)
    reps = DIFF_DV // DIFF_DQK
    cos_t = jnp.tile(jnp.concatenate([cos, cos], axis=-1), (1, reps))
    sin_t = jnp.tile(jnp.concatenate([-sin, sin], axis=-1), (1, reps))
    return cos_t, sin_t


def _lane_row(vec, offset):
    return jnp.zeros((LANES,), F32).at[offset:offset + vec.shape[0]].set(vec.astype(F32))


def _layer_weights(l, p):
    w_in = p['w_in'][l]
    w_big = jnp.concatenate([w_in[:, 5648:9744], w_in[:, 0:2048], w_in[:, 2056:5128], w_in[:, 5136:5648]],
                            axis=1).astype(BF16)
    w_small = jnp.concatenate([w_in[:, 2048:2056], w_in[:, 5128:5136],
                               jnp.zeros((D_MODEL, LANES - 16), F32)], axis=1).astype(BF16)
    par = jnp.zeros((SUBLANES, LANES), F32)
    gdn_par = par.at[0].set(_lane_row(p['gdn_a_log'][l], GDN_HEADS)).at[1].set(_lane_row(p['gdn_dt_bias'][l], GDN_HEADS))
    ssm_par = par.at[0].set(_lane_row(p['ssm_a_log'][l], SSM_HEADS)).at[1].set(_lane_row(p['ssm_dt_bias'][l], SSM_HEADS))
    rw = jnp.concatenate([p['router_w'][l].T, jnp.zeros((LANES - N_EXPERTS, D_MODEL), F32)], axis=0)
    rwh = rw.astype(BF16)
    bgu = p['b_gate_up'][l]
    return dict(
        w_big=w_big, w_small=w_small, gdn_par=gdn_par, ssm_par=ssm_par,
        gdn_conv_w=p['gdn_conv_w'][l], gdn_norm_w=p['gdn_norm_w'][l][None, :],
        lambda_init=0.8 - 0.6 * math.exp(-0.3 * l), diff_lambda=p['diff_lambda'][l],
        diff_norm_w=p['diff_norm_w'][l][None, :],
        ssm_conv_w=p['ssm_conv_w'][l], ssm_conv_b=p['ssm_conv_b'][l][None, :],
        dskip=jnp.repeat(p['ssm_d'][l], SSM_HEADDIM)[None, :], ssm_norm_w=p['ssm_norm_w'][l][None, :],
        w_mem=p['w_mem'][l].astype(BF16), w_branch=p['w_branch'][l].astype(BF16), w_out=p['w_out'][l].astype(BF16),
        ln1_g=p['ln1_g'][l][None, :], ln1_b=p['ln1_b'][l][None, :],
        rwh=rwh, rwl=(rw - rwh.astype(F32)).astype(BF16), rb=_lane_row(p['router_b'][l], 0)[:, None],
        first_expert=l * bgu.shape[0],
        ln2_g=p['ln2_g'][l][None, :], ln2_b=p['ln2_b'][l][None, :])


def _mix_and_route(w, x, xbf, kv_src, bsz, seq, cos, sin):
    t = bsz * seq
    h, hs = _in_proj(xbf, w['w_big'], w['w_small'], min(1024, t), H_WIDTH // 4)
    y_gdn = _gdn(h, hs, w['gdn_conv_w'], w['gdn_par'], w['gdn_norm_w'], bsz, seq)
    y_diff = _diff_attention(h, cos, sin, w['diff_lambda'], w['diff_norm_w'], w['lambda_init'], bsz, seq)
    y_ssm = _ssd(h, hs, w['ssm_conv_w'], w['ssm_conv_b'], w['ssm_par'], w['dskip'], w['ssm_norm_w'], bsz, seq)
    kv = _matmul(kv_src, w['w_mem'], BF16, min(1024, kv_src.shape[0]), HB, "mem_kv")
    y_mem = _mem_attention(h, kv, bsz, seq, kv_src.shape[0] // bsz)
    x1, xa, xb, route, counts = _merge(h, (y_gdn, y_diff, y_ssm, y_mem), x, w['w_branch'], w['w_out'],
                                       w['ln1_g'], w['ln1_b'], w['rwh'], w['rwl'], w['rb'])
    pos, tile_expert, n_rows = _route_positions(route, counts)
    xsa, xsb = _sc_dispatch(xa, xb, pos, n_rows)
    cw = jnp.zeros((t, LANES), F32).at[:, :TOP_K].set(route[:TOP_K].T)
    return dict(x1=x1, route=cw, pos=pos, tile_expert=tile_expert, xsa=xsa, xsb=xsb)


def _experts(w, ew, st):
    ysa, ysb = _expert_mlp(st['tile_expert'], st['xsa'], st['xsb'], ew['w_gate_up'], ew['b_gate_lin'], ew['w_down'],
                           ew['b_down'], w['first_expert'])
    t = st['x1'].shape[0]
    ga, gb = _sc_collect(ysa, ysb, st['pos'].reshape(1, TOP_K * t))
    return dict(st, ga=ga.reshape(TOP_K, t, SC_WORDS), gb=gb.reshape(TOP_K, t, SC_WORDS))


def _finish(w, st):
    return _combine(st['ga'], st['gb'], st['route'], st['x1'], w['ln2_g'], w['ln2_b'])


def kernel(x, mem, w_in, gdn_conv_w, gdn_a_log, gdn_dt_bias, gdn_norm_w, diff_lambda, diff_norm_w, ssm_conv_w, ssm_conv_b, ssm_a_log, ssm_dt_bias, ssm_d, ssm_norm_w, w_mem, w_branch, w_out, ln1_g, ln1_b, router_w, router_b, w_gate_up, b_gate_up, w_down, b_down, ln2_g, ln2_b):
    p = dict(w_in=w_in, gdn_conv_w=gdn_conv_w, gdn_a_log=gdn_a_log, gdn_dt_bias=gdn_dt_bias, gdn_norm_w=gdn_norm_w,
             diff_lambda=diff_lambda, diff_norm_w=diff_norm_w, ssm_conv_w=ssm_conv_w, ssm_conv_b=ssm_conv_b,
             ssm_a_log=ssm_a_log, ssm_dt_bias=ssm_dt_bias, ssm_d=ssm_d, ssm_norm_w=ssm_norm_w, w_mem=w_mem,
             w_branch=w_branch, w_out=w_out, ln1_g=ln1_g, ln1_b=ln1_b, router_w=router_w, router_b=router_b,
             w_gate_up=w_gate_up, b_gate_up=b_gate_up, w_down=w_down, b_down=b_down, ln2_g=ln2_g, ln2_b=ln2_b)
    bsz, seq, d = x.shape
    depth, n_exp = w_gate_up.shape[0], w_gate_up.shape[1]
    expert_w = dict(
        w_gate_up=w_gate_up.reshape(depth * n_exp, d, 2 * D_FF),
        b_gate_lin=b_gate_up.reshape(depth * n_exp, 2 * D_FF // DEINT_BLOCK, DEINT_BLOCK // 2, 2)
        .transpose(0, 1, 3, 2).reshape(depth * n_exp, 1, 2 * D_FF),
        w_down=w_down.reshape(depth * n_exp, D_FF, d),
        b_down=b_down.reshape(depth * n_exp, 1, d))
    cos, sin = _rope_tables(seq)
    groups = TOKEN_GROUPS if bsz % TOKEN_GROUPS == 0 else 1
    gb = bsz // groups
    xs = [x[g * gb:(g + 1) * gb].reshape(gb * seq, d) for g in range(groups)]
    xbfs = [v.astype(BF16) for v in xs]
    kvs = [mem[g * gb:(g + 1) * gb].reshape(gb * mem.shape[1], d).astype(BF16) for g in range(groups)]
    for l in range(depth):
        w = _layer_weights(l, p)
        sts = [_mix_and_route(w, xs[g], xbfs[g], kvs[g], gb, seq, cos, sin) for g in range(groups)]
        sts = [_experts(w, expert_w, st) for st in sts]
        outs = [_finish(w, st) for st in sts]
        xs = [o[0] for o in outs]
        xbfs = [o[1] for o in outs]
    return jnp.concatenate([v.reshape(gb, seq, d) for v in xs], axis=0)
```

```python
import functools
import math

import jax
import jax.numpy as jnp
from jax import lax
from jax.experimental import pallas as pl
from jax.experimental.pallas import tpu as pltpu
from jax.experimental.pallas import tpu_sc as plsc

F32 = jnp.float32
BF16 = jnp.bfloat16
U32 = jnp.uint32

D_MODEL = 1024
DEPTH = 4
CHUNK = 64
Q_BLOCK = 128
CONV_WIDTH = 4
ROPE_THETA = 10000.0
NORM_EPS = 1e-6
LN_EPS = 1e-5

GDN_HEADS = 4
GDN_DK = 128
GDN_DV = 128
DIFF_HEADS = 4
DIFF_DQK = 64
DIFF_DV = 128
SSM_HEADS = 8
SSM_HEADDIM = 64
SSM_GROUPS = 2
SSM_STATE = 128
SSM_INNER = 512
MEM_HEADS = 4
MEM_HEADDIM = 128
N_BRANCH = 4
N_EXPERTS = 32
TOP_K = 4
D_FF = 1024
SWIGLU_LIMIT = 7.0
SWIGLU_ALPHA = 1.702
DEEPNORM_ALPHA = (2.0 * DEPTH) ** 0.25
LOG2_E = math.log2(math.e)

LANES = 128
SUBLANES = 8
HALO = 16
VMEM_LIMIT = 56 * 1024 * 1024

HB = 512
COL_GATES = 0
COL_GQ, COL_GK, COL_GV, COL_GZ = 8, 9, 10, 11
COL_DQ, COL_DK, COL_DV = 12, 13, 14
COL_SZ = 15
COL_SXBC = 8
COL_MQ = 18
H_WIDTH = 19 * HB

SEQ_TILE = 512
ROW_TILE = 512
EXPERT_TILE = 512
SC_WINDOW = 128
SC_WORDS = 256
ROUTE_ROWS = 16
DEINT_BLOCK = 2 * LANES
GDN_TILE = 512
DIFF_ROW_SPLIT = 2
TOKEN_GROUPS = 2


def _cparams(*sem):
    return pltpu.CompilerParams(dimension_semantics=sem, vmem_limit_bytes=VMEM_LIMIT)


def _dot(a, b):
    return jnp.dot(a.astype(BF16), b.astype(BF16), preferred_element_type=F32)


def _dot_nt(a, b):
    return lax.dot_general(a.astype(BF16), b.astype(BF16), (((1,), (1,)), ((), ())), preferred_element_type=F32)


def _bdot(a, b):
    return jnp.einsum('cmk,ckn->cmn', a.astype(BF16), b.astype(BF16), preferred_element_type=F32)


def _bdot_nt(a, b):
    return jnp.einsum('cmk,cnk->cmn', a.astype(BF16), b.astype(BF16), preferred_element_type=F32)


def _split2(a):
    hi = a.astype(BF16)
    lo = (a - hi.astype(F32)).astype(BF16)
    return hi, lo


def _sigmoid(x):
    return 0.5 * jnp.tanh(0.5 * x) + 0.5


def _silu(x):
    return x * _sigmoid(x)


def _softplus(x):
    return jnp.maximum(x, 0.0) + jnp.log1p(jnp.exp(-jnp.abs(x)))


def _rms(x, w):
    return x * lax.rsqrt(jnp.mean(x * x, axis=-1, keepdims=True) + NORM_EPS) * w


def _layer_norm(x, g, b):
    mu = jnp.mean(x, axis=-1, keepdims=True)
    xc = x - mu
    var = jnp.mean(xc * xc, axis=-1, keepdims=True)
    return xc * lax.rsqrt(var + LN_EPS) * g + b


def _mm_body(x_ref, w_ref, o_ref):
    o_ref[...] = jnp.dot(x_ref[...], w_ref[...], preferred_element_type=F32).astype(o_ref.dtype)


def _matmul(x, w, out_dtype, tm, tn, name):
    m, k = x.shape
    n = w.shape[1]
    return pl.pallas_call(
        _mm_body,
        grid=(m // tm, n // tn),
        in_specs=[pl.BlockSpec((tm, k), lambda i, j: (i, 0)), pl.BlockSpec((k, tn), lambda i, j: (0, j))],
        out_specs=pl.BlockSpec((tm, tn), lambda i, j: (i, j)),
        out_shape=jax.ShapeDtypeStruct((m, n), out_dtype),
        compiler_params=_cparams("parallel", "parallel"),
        name=name,
    )(x, w)


def _in_proj_body(x_ref, w_ref, ws_ref, o_ref, os_ref):
    x = x_ref[...]
    o_ref[...] = jnp.dot(x, w_ref[...], preferred_element_type=F32).astype(o_ref.dtype)

    @pl.when(pl.program_id(1) == 0)
    def _():
        os_ref[...] = jnp.dot(x, ws_ref[...], preferred_element_type=F32)


def _in_proj(x, w, w_small, tm, tn):
    m, k = x.shape
    n = w.shape[1]
    return pl.pallas_call(
        _in_proj_body,
        grid=(m // tm, n // tn),
        in_specs=[pl.BlockSpec((tm, k), lambda i, j: (i, 0)), pl.BlockSpec((k, tn), lambda i, j: (0, j)),
                  pl.BlockSpec((k, LANES), lambda i, j: (0, 0))],
        out_specs=[pl.BlockSpec((tm, tn), lambda i, j: (i, j)), pl.BlockSpec((tm, LANES), lambda i, j: (i, 0))],
        out_shape=[jax.ShapeDtypeStruct((m, n), BF16), jax.ShapeDtypeStruct((m, LANES), F32)],
        compiler_params=_cparams("parallel", "arbitrary"),
        name="in_proj",
    )(x, w, w_small)


def _shift_taps(nc):
    m = lax.broadcasted_iota(jnp.int32, (1, (CONV_WIDTH - 1) * CHUNK, CHUNK + HALO), 1)
    k = lax.broadcasted_iota(jnp.int32, (1, (CONV_WIDTH - 1) * CHUNK, CHUNK + HALO), 2)
    src = (m % CHUNK) + HALO - (CONV_WIDTH - 1 - m // CHUNK)
    return jnp.broadcast_to(_mask(k == src).astype(BF16), (nc, (CONV_WIDTH - 1) * CHUNK, CHUNK + HALO))


def _conv_silu(x, tail_ref, slot, xbuf_ref, w, bias, taps):
    rows, width = x.shape
    nc = rows // CHUNK
    xbuf_ref[0:HALO, :] = tail_ref[slot]
    xbuf_ref[HALO:HALO + rows, :] = x
    tail_ref[slot] = x[rows - HALO:rows, :]
    xext = jnp.concatenate([xbuf_ref[i * CHUNK:(i + 1) * CHUNK + HALO, :][None] for i in range(nc)], axis=0)
    shifted = _bdot(taps, xext)
    y = x.astype(F32).reshape(nc, CHUNK, width) * w[CONV_WIDTH - 1:CONV_WIDTH, :]
    for j in range(CONV_WIDTH - 1):
        y = y + shifted[:, j * CHUNK:(j + 1) * CHUNK, :] * w[j:j + 1, :]
    y = y.reshape(rows, width)
    if bias is not None:
        y = y + bias
    return _silu(y)


def _chunk_iota():
    r = lax.broadcasted_iota(jnp.int32, (1, CHUNK, CHUNK), 1)
    c = lax.broadcasted_iota(jnp.int32, (1, CHUNK, CHUNK), 2)
    return r, c


def _mask(cond):
    return jnp.where(cond, 1.0, 0.0)


def _segment_logdecay(g3, r, c):
    nb = g3.shape[0]
    lincl = jnp.broadcast_to(_mask(r >= c).astype(BF16), (nb, CHUNK, CHUNK))
    upper = _mask(r > c)
    ghi = g3.astype(BF16).astype(F32)
    glo = g3 - ghi
    return _bdot(lincl, ghi * upper) + _bdot(lincl, glo * upper)


def _inv_unit_lower(a, r, c):
    d = _mask(r == c) - a * _mask(((r >> 1) == (c >> 1)) & (r > c))
    for lb in range(1, 6):
        lower_left = _mask(((r >> (lb + 1)) == (c >> (lb + 1))) & ((r >> lb) > (c >> lb)))
        d = d - _bdot(_bdot(d, a * lower_left), d)
    return d


def _gdn_body(q_ref, k_ref, v_ref, z_ref, hs_ref, cw_ref, par_ref, nw_ref, o_ref,
              tail_ref, xbuf_ref, qs_ref, ks_ref, vs_ref, pt_ref, bt_ref, qwt_ref, qut_ref, ot_ref, gl_ref,
              state_ref):
    ts = q_ref.shape[0]
    nc = ts // CHUNK

    @pl.when(pl.program_id(1) == 0)
    def _():
        tail_ref[...] = jnp.zeros_like(tail_ref)
        state_ref[...] = jnp.zeros_like(state_ref)

    taps = _shift_taps(nc)
    for slot, (src, dst) in enumerate(((q_ref, qs_ref), (k_ref, ks_ref), (v_ref, vs_ref))):
        w = cw_ref[:, slot * HB:(slot + 1) * HB]
        dst[...] = _conv_silu(src[...], tail_ref, slot, xbuf_ref, w, None, taps)

    hs = hs_ref[...]
    beta_all = _sigmoid(hs)
    g_all = -jnp.exp(par_ref[0:1, :]) * _softplus(hs + par_ref[1:2, :])

    def stack(per_head):
        return jnp.concatenate([per_head(h) for h in range(GDN_HEADS)], axis=0)

    def l2(ref, h):
        x = ref[:, h * GDN_DK:(h + 1) * GDN_DK]
        return (x * lax.rsqrt(jnp.sum(x * x, axis=-1, keepdims=True) + NORM_EPS)).reshape(nc, CHUNK, GDN_DK)

    q3 = stack(lambda h: l2(qs_ref, h)) * (GDN_DK ** -0.5)
    k3 = stack(lambda h: l2(ks_ref, h))
    v3 = stack(lambda h: vs_ref[:, h * GDN_DV:(h + 1) * GDN_DV].reshape(nc, CHUNK, GDN_DV))
    b3 = stack(lambda h: beta_all[:, h:h + 1].reshape(nc, CHUNK, 1))
    g3 = stack(lambda h: g_all[:, GDN_HEADS + h:GDN_HEADS + h + 1].reshape(nc, CHUNK, 1))
    r, c = _chunk_iota()
    m = _segment_logdecay(g3, r, c)
    em = jnp.exp(m)
    a = _bdot_nt(k3, k3) * b3 * (em * _mask(r > c))
    t = _inv_unit_lower(a, r, c)
    gc = m[:, :, 0:1] + g3[:, 0:1, :]
    eg = jnp.exp(gc)
    g_last = gc[:, CHUNK - 1:CHUNK, :]
    sol = _bdot(t, jnp.concatenate([v3 * b3, k3 * (b3 * eg)], axis=-1))
    u = sol[:, :, :GDN_DV]
    w = sol[:, :, GDN_DV:]
    kd = k3 * jnp.exp(g_last - gc)
    qk = _bdot_nt(q3, k3) * (em * _mask(r >= c))
    pt = _bdot(jnp.swapaxes(w, 1, 2), kd).astype(BF16)
    bt = _bdot(jnp.swapaxes(u, 1, 2), kd)
    qwt = jnp.swapaxes(q3 * eg - _bdot(qk, w), 1, 2).astype(BF16)
    qut = jnp.swapaxes(_bdot(qk, u), 1, 2)
    gl = jnp.broadcast_to(jnp.exp(g_last), (GDN_HEADS * nc, SUBLANES, LANES))
    for h in range(GDN_HEADS):
        hb = slice(h * nc, (h + 1) * nc)
        pt_ref[h] = pt[hb]
        bt_ref[h] = bt[hb]
        qwt_ref[h] = qwt[hb]
        qut_ref[h] = qut[hb]
        gl_ref[h] = gl[hb]

    def chunk_step(ci, carry):
        for h in range(GDN_HEADS):
            st = state_ref[h]
            stb = st.astype(BF16)
            ot_ref[h, ci] = jnp.dot(stb, qwt_ref[h, ci], preferred_element_type=F32) + qut_ref[h, ci]
            state_ref[h] = (st * gl_ref[h, ci][0:1, :] - jnp.dot(stb, pt_ref[h, ci], preferred_element_type=F32)
                            + bt_ref[h, ci])
        return carry

    lax.fori_loop(0, nc, chunk_step, 0)

    for h in range(GDN_HEADS):
        sl = slice(h * GDN_DV, (h + 1) * GDN_DV)
        o = jnp.swapaxes(ot_ref[h], 1, 2).reshape(ts, GDN_DV)
        o_ref[:, sl] = (_rms(o, nw_ref[...]) * _silu(z_ref[:, sl].astype(F32))).astype(o_ref.dtype)


def _gdn(h, hs, conv_w, par, norm_w, bsz, seq):
    ts = min(GDN_TILE, seq)
    ns = seq // ts
    tok = lambda col: pl.BlockSpec((ts, HB), lambda b, s: (b * ns + s, col))
    full = lambda shape: pl.BlockSpec(shape, lambda b, s: (0,) * len(shape))
    nc = ts // CHUNK
    return pl.pallas_call(
        _gdn_body,
        grid=(bsz, ns),
        in_specs=[tok(COL_GQ), tok(COL_GK), tok(COL_GV), tok(COL_GZ),
                  pl.BlockSpec((ts, LANES), lambda b, s: (b * ns + s, 0)),
                  full((CONV_WIDTH, 3 * HB)), full((SUBLANES, LANES)), full((1, GDN_DV))],
        out_specs=pl.BlockSpec((ts, HB), lambda b, s: (b * ns + s, 0)),
        out_shape=jax.ShapeDtypeStruct((bsz * seq, HB), BF16),
        scratch_shapes=[
            pltpu.VMEM((3, HALO, HB), BF16),
            pltpu.VMEM((ts + HALO, HB), BF16),
            pltpu.VMEM((ts, HB), F32), pltpu.VMEM((ts, HB), F32), pltpu.VMEM((ts, HB), F32),
            pltpu.VMEM((GDN_HEADS, nc, GDN_DK, GDN_DK), BF16), pltpu.VMEM((GDN_HEADS, nc, GDN_DV, GDN_DK), F32),
            pltpu.VMEM((GDN_HEADS, nc, GDN_DK, CHUNK), BF16), pltpu.VMEM((GDN_HEADS, nc, GDN_DV, CHUNK), F32),
            pltpu.VMEM((GDN_HEADS, nc, GDN_DV, CHUNK), F32),
            pltpu.VMEM((GDN_HEADS, nc, SUBLANES, LANES), F32),
            pltpu.VMEM((GDN_HEADS, GDN_DV, GDN_DK), F32),
        ],
        compiler_params=_cparams("parallel", "arbitrary"),
        name="gdn",
    )(h, h, h, h, hs, conv_w, par, norm_w)


def _expand_heads(x, expand_hi):
    hi = x.astype(BF16)
    r1 = x - hi.astype(F32)
    mid = r1.astype(BF16)
    lo = (r1 - mid.astype(F32)).astype(BF16)
    e = expand_hi
    return (jnp.dot(hi, e, preferred_element_type=F32) + jnp.dot(mid, e, preferred_element_type=F32)
            + jnp.dot(lo, e, preferred_element_type=F32))


def _ssd_body(z_ref, xbc_ref, hs_ref, cw_ref, cb_ref, par_ref, dskip_ref, nw_ref, o_ref,
              tail_ref, xbuf_ref, xs_ref, bc_ref, yd_ref, hin_ref, state_ref):
    ts = z_ref.shape[0]
    nc = ts // CHUNK
    gw = SSM_INNER // SSM_GROUPS

    @pl.when(pl.program_id(1) == 0)
    def _():
        tail_ref[...] = jnp.zeros_like(tail_ref)
        state_ref[...] = jnp.zeros_like(state_ref)

    taps = _shift_taps(nc)
    for slot, dst in enumerate((xs_ref, bc_ref)):
        cols = slice(slot * HB, (slot + 1) * HB)
        dst[...] = _conv_silu(xbc_ref[:, cols], tail_ref, slot, xbuf_ref, cw_ref[:, cols], cb_ref[:, cols], taps)

    hs = hs_ref[...]
    dt = _softplus(hs + par_ref[1:2, :])
    a_all = -jnp.exp(par_ref[0:1, :]) * dt
    r, c = _chunk_iota()
    tril = _mask(r >= c)
    lincl = jnp.broadcast_to(tril.astype(BF16), (nc, CHUNK, CHUNK))
    a3 = a_all.reshape(nc, CHUNK, LANES)
    ahi = a3.astype(BF16)
    ar = a3 - ahi.astype(F32)
    amid = ar.astype(BF16)
    alo = (ar - amid.astype(F32)).astype(BF16)
    acum = _bdot(lincl, ahi) + _bdot(lincl, amid) + _bdot(lincl, alo)
    lane = lax.broadcasted_iota(jnp.int32, (LANES, SSM_INNER), 0)
    chan = lax.broadcasted_iota(jnp.int32, (LANES, SSM_INNER), 1)
    expand = jnp.where(lane - SSM_HEADS == chan // SSM_HEADDIM, 1.0, 0.0).astype(BF16)
    dt_x = _expand_heads(dt, expand)
    acum_x = _expand_heads(acum.reshape(ts, LANES), expand).reshape(nc, CHUNK, SSM_INNER)
    a_last_x = acum_x[:, CHUNK - 1:CHUNK, :]
    ea_x = jnp.exp(acum_x).reshape(ts, SSM_INNER)
    din_x = jnp.exp(a_last_x - acum_x).reshape(ts, SSM_INNER)
    cd_x = jnp.exp(a_last_x)

    xs = xs_ref[...]
    xdt = xs * dt_x
    xw = xdt * din_x
    for g in range(SSM_GROUPS):
        gcols = slice(g * gw, (g + 1) * gw)
        b3 = bc_ref[:, g * SSM_STATE:(g + 1) * SSM_STATE].reshape(nc, CHUNK, SSM_STATE)
        c3 = bc_ref[:, gw + g * SSM_STATE:gw + (g + 1) * SSM_STATE].reshape(nc, CHUNK, SSM_STATE)
        cb = _bdot_nt(c3, b3)
        hpg = SSM_HEADS // SSM_GROUPS
        heads = [g * hpg + hr for hr in range(hpg)]
        ah3 = jnp.concatenate([a_all[:, SSM_HEADS + hd:SSM_HEADS + hd + 1].reshape(nc, CHUNK, 1) for hd in heads],
                              axis=0)
        x4 = jnp.concatenate([xdt[:, hd * SSM_HEADDIM:(hd + 1) * SSM_HEADDIM].reshape(nc, CHUNK, SSM_HEADDIM)
                              for hd in heads], axis=0)
        seg = jnp.exp(_segment_logdecay(ah3, r, c)) * tril
        yd = _bdot(seg * jnp.concatenate([cb] * hpg, axis=0), x4)
        for hr, hd in enumerate(heads):
            yd_ref[:, hd * SSM_HEADDIM:(hd + 1) * SSM_HEADDIM] = yd[hr * nc:(hr + 1) * nc].reshape(ts, SSM_HEADDIM)
        bt = jnp.swapaxes(b3, 1, 2)
        st = _bdot(bt, xw[:, gcols].reshape(nc, CHUNK, gw))
        hcur = state_ref[g]
        for ci in range(nc):
            hin_ref[ci] = hcur
            hcur = hcur * cd_x[ci, 0:1, gcols] + st[ci]
        state_ref[g] = hcur
        y_off = _bdot(c3, hin_ref[...]).reshape(ts, gw) * ea_x[:, gcols]
        y = yd_ref[:, gcols] + y_off + dskip_ref[:, gcols] * xs[:, gcols]
        gated = y * _silu(z_ref[:, gcols].astype(F32))
        o_ref[:, gcols] = _rms(gated, nw_ref[:, gcols]).astype(o_ref.dtype)


def _ssd(h, hs, conv_w, conv_b, par, dskip, norm_w, bsz, seq):
    ts = min(SEQ_TILE, seq)
    ns = seq // ts
    nc = ts // CHUNK
    gw = SSM_INNER // SSM_GROUPS
    full = lambda shape: pl.BlockSpec(shape, lambda b, s: (0,) * len(shape))
    return pl.pallas_call(
        _ssd_body,
        grid=(bsz, ns),
        in_specs=[pl.BlockSpec((ts, HB), lambda b, s: (b * ns + s, COL_SZ)),
                  pl.BlockSpec((ts, 2 * HB), lambda b, s: (b * ns + s, COL_SXBC)),
                  pl.BlockSpec((ts, LANES), lambda b, s: (b * ns + s, 0)),
                  full((CONV_WIDTH, 2 * HB)), full((1, 2 * HB)), full((SUBLANES, LANES)),
                  full((1, SSM_INNER)), full((1, SSM_INNER))],
        out_specs=pl.BlockSpec((ts, HB), lambda b, s: (b * ns + s, 0)),
        out_shape=jax.ShapeDtypeStruct((bsz * seq, HB), BF16),
        scratch_shapes=[
            pltpu.VMEM((2, HALO, HB), BF16),
            pltpu.VMEM((ts + HALO, HB), BF16),
            pltpu.VMEM((ts, HB), F32), pltpu.VMEM((ts, HB), F32), pltpu.VMEM((ts, HB), F32),
            pltpu.VMEM((nc, SSM_STATE, gw), F32),
            pltpu.VMEM((SSM_GROUPS, SSM_STATE, gw), F32),
        ],
        compiler_params=_cparams("parallel", "arbitrary"),
        name="ssd",
    )(h, h, hs, conv_w, conv_b, par, dskip, norm_w)


def _rope(x, cos, sin_signed):
    width = x.shape[-1]
    half = DIFF_DQK // 2
    lane = lax.broadcasted_iota(jnp.int32, x.shape, x.ndim - 1)
    fwd = pltpu.roll(x, width - half, axis=x.ndim - 1)
    bwd = pltpu.roll(x, half, axis=x.ndim - 1)
    partner = jnp.where((lane & half) == 0, fwd, bwd)
    return x * cos + partner * sin_signed


def _diff_body(q_ref, k_ref, v_ref, cos_ref, sin_ref, lam_ref, nw_ref, o_ref, kr_ref, *, lambda_init, nq):
    qi = pl.program_id(2)
    tq = q_ref.shape[0]
    q0 = pl.multiple_of(qi * tq, tq)
    cos = cos_ref[pl.ds(q0, tq), :]
    sin = sin_ref[pl.ds(q0, tq), :]
    kr_ref[pl.ds(q0, tq), :] = _rope(k_ref[...].astype(F32), cos, sin).astype(BF16)
    qr = _rope(q_ref[...].astype(F32), cos, sin) * (DIFF_DQK ** -0.5 * LOG2_E)

    lp = lam_ref[...]
    prod01 = jnp.sum(lp[0:1, :] * lp[1:2, :], axis=-1, keepdims=True)
    prod23 = jnp.sum(lp[2:3, :] * lp[3:4, :], axis=-1, keepdims=True)
    lam = jnp.exp(prod01) - jnp.exp(prod23) + lambda_init

    lane = lax.broadcasted_iota(jnp.int32, (tq, 2 * DIFF_DQK), 1)
    qmaps = (jnp.where(lane < DIFF_DQK, qr, 0.0).astype(BF16), jnp.where(lane >= DIFF_DQK, qr, 0.0).astype(BF16))
    th = tq // DIFF_ROW_SPLIT

    for i in range(nq):

        @pl.when(qi == i)
        def _(i=i):
            lo = i * tq
            for rh in range(DIFF_ROW_SPLIT):
                rows = slice(rh * th, (rh + 1) * th)
                own = (rh + 1) * th
                row = lax.broadcasted_iota(jnp.int32, (th, own), 0) + rh * th
                col = lax.broadcasted_iota(jnp.int32, (th, own), 1)
                diag_ok = (row // CHUNK) >= (col // CHUNK)
                outs = []
                for qmap in qmaps:
                    qm = qmap[rows, :]
                    s_diag = jnp.where(diag_ok, _dot_nt(qm, kr_ref[lo:lo + own, :]), -jnp.inf)
                    mx = jnp.max(s_diag, axis=-1, keepdims=True)
                    if i > 0:
                        s_off = _dot_nt(qm, kr_ref[0:lo, :])
                        mx = jnp.maximum(mx, jnp.max(s_off, axis=-1, keepdims=True))
                    p_diag = jnp.exp2(s_diag - mx)
                    denom = jnp.sum(p_diag, axis=-1, keepdims=True)
                    acc = _dot(p_diag, v_ref[lo:lo + own, :])
                    if i > 0:
                        p_off = jnp.exp2(s_off - mx)
                        denom = denom + jnp.sum(p_off, axis=-1, keepdims=True)
                        acc = acc + _dot(p_off, v_ref[0:lo, :])
                    outs.append(acc / denom)
                o = outs[0] - lam * outs[1]
                o_ref[rows, :] = (_rms(o, nw_ref[...]) * (1.0 - lambda_init)).astype(o_ref.dtype)


def _diff_attention(h, cos, sin, lam_params, norm_w, lambda_init, bsz, seq):
    tq = min(SEQ_TILE, seq)
    nq = seq // tq
    per_head = HB // DIFF_DV
    full = lambda shape: pl.BlockSpec(shape, lambda b, hd, q: (0,) * len(shape))
    tok = lambda col: pl.BlockSpec((tq, DIFF_DV), lambda b, hd, q: (b * nq + q, col * per_head + hd))
    return pl.pallas_call(
        functools.partial(_diff_body, lambda_init=lambda_init, nq=nq),
        grid=(bsz, DIFF_HEADS, nq),
        in_specs=[tok(COL_DQ), tok(COL_DK),
                  pl.BlockSpec((seq, DIFF_DV), lambda b, hd, q: (b, COL_DV * per_head + hd)),
                  full((seq, DIFF_DV)), full((seq, DIFF_DV)),
                  full((4, DIFF_DQK)), full((1, DIFF_DV))],
        out_specs=pl.BlockSpec((tq, DIFF_DV), lambda b, hd, q: (b * nq + q, hd)),
        out_shape=jax.ShapeDtypeStruct((bsz * seq, HB), BF16),
        scratch_shapes=[pltpu.VMEM((seq, DIFF_DV), BF16)],
        compiler_params=_cparams("parallel", "parallel", "arbitrary"),
        name="diff_attn",
    )(h, h, h, cos, sin, lam_params, norm_w)


def _mem_body(q_ref, kv_ref, o_ref):
    width = MEM_HEADS * MEM_HEADDIM
    for h in range(MEM_HEADS):
        cols = slice(h * MEM_HEADDIM, (h + 1) * MEM_HEADDIM)
        q = q_ref[:, cols].astype(F32) * (MEM_HEADDIM ** -0.5)
        s = _dot_nt(q, kv_ref[:, cols])
        m = jnp.max(s, axis=-1, keepdims=True)
        p = jnp.exp(s - m)
        p = p / jnp.sum(p, axis=-1, keepdims=True)
        o_ref[:, cols] = _dot(p, kv_ref[:, width + h * MEM_HEADDIM:width + (h + 1) * MEM_HEADDIM]).astype(o_ref.dtype)


def _mem_attention(h, kv, bsz, seq, mem_len):
    tm = min(ROW_TILE, seq)
    ns = seq // tm
    return pl.pallas_call(
        _mem_body,
        grid=(bsz, ns),
        in_specs=[pl.BlockSpec((tm, HB), lambda b, s: (b * ns + s, COL_MQ)),
                  pl.BlockSpec((mem_len, 2 * HB), lambda b, s: (b, 0))],
        out_specs=pl.BlockSpec((tm, HB), lambda b, s: (b * ns + s, 0)),
        out_shape=jax.ShapeDtypeStruct((bsz * seq, HB), BF16),
        compiler_params=_cparams("parallel", "parallel"),
        name="mem_attn",
    )(h, kv)


def _pack_halves(x):
    n = x.shape[1] // 2
    lo = pltpu.bitcast(x[:, :n].astype(BF16).astype(F32), U32)
    hi = pltpu.bitcast(x[:, n:].astype(BF16).astype(F32), U32)
    return (lo >> 16) | hi


def _unpack_halves(w):
    lo = pltpu.bitcast(w << 16, F32)
    hi = pltpu.bitcast(w & jnp.uint32(0xFFFF0000), F32)
    return lo.astype(BF16), hi.astype(BF16)


def _merge_body(g_ref, y0_ref, y1_ref, y2_ref, y3_ref, x_ref, wb_ref, wo_ref, lng_ref, lnb_ref,
                rwh_ref, rwl_ref, rb_ref, xo_ref, pa_ref, pb_ref, route_ref, cnt_ref):
    tm = x_ref.shape[0]
    merged = jnp.zeros((tm, D_MODEL), F32)
    for i, y_ref in enumerate((y0_ref, y1_ref, y2_ref, y3_ref)):
        gate = _sigmoid(g_ref[:, i * D_MODEL:(i + 1) * D_MODEL].astype(F32))
        merged = merged + gate * jnp.dot(y_ref[...], wb_ref[i], preferred_element_type=F32)
    mix = _dot(merged, wo_ref[...])
    xn = _layer_norm(DEEPNORM_ALPHA * x_ref[...] + mix, lng_ref[...], lnb_ref[...])
    xo_ref[...] = xn
    pa_ref[...] = _pack_halves(xn[:, :D_MODEL // 2])
    pb_ref[...] = _pack_halves(xn[:, D_MODEL // 2:])

    xh, xl = _split2(xn)
    logits = (_dot_nt(rwh_ref[...], xh) + (_dot_nt(rwl_ref[...], xh) + _dot_nt(rwh_ref[...], xl)))[:N_EXPERTS, :]
    work = logits + rb_ref[0:N_EXPERTS, 0:1]
    expert = lax.broadcasted_iota(jnp.int32, (N_EXPERTS, tm), 0)
    vals, ids, sels = [], [], []
    for _ in range(TOP_K):
        mx = jnp.max(work, axis=0, keepdims=True)
        idx = jnp.min(jnp.where(work == mx, expert, N_EXPERTS), axis=0, keepdims=True)
        sel = expert == idx
        vals.append(mx)
        ids.append(idx)
        sels.append(sel)
        work = jnp.where(sel, -jnp.inf, work)
    exps = [jnp.exp(v - vals[0]) for v in vals]
    denom = exps[0] + exps[1] + exps[2] + exps[3]
    chosen = jnp.where(sels[0] | sels[1] | sels[2] | sels[3], 1.0, 0.0)
    r = lax.broadcasted_iota(jnp.int32, (tm, tm), 0)
    c = lax.broadcasted_iota(jnp.int32, (tm, tm), 1)
    utri = jnp.where(r <= c, 1.0, 0.0).astype(BF16)
    cum = jnp.dot(chosen.astype(BF16), utri, preferred_element_type=F32)
    ranks = [jnp.sum(jnp.where(sels[k], cum, 0.0), axis=0, keepdims=True) - 1.0 for k in range(TOP_K)]
    rows = [e / denom for e in exps] + [i.astype(F32) for i in ids] + ranks
    rows.append(jnp.zeros((ROUTE_ROWS - len(rows), tm), F32))
    route_ref[...] = jnp.concatenate(rows, axis=0)
    cnt_ref[0] = jnp.broadcast_to(cum[:, tm - 1:tm], (N_EXPERTS, LANES))


def _merge(h, ys, x, wb, wo, lng, lnb, rwh, rwl, rb):
    t = x.shape[0]
    tm = min(ROW_TILE, t)
    nt = t // tm
    tok = lambda width: pl.BlockSpec((tm, width), lambda i: (i, 0))
    full = lambda shape: pl.BlockSpec(shape, lambda i: (0,) * len(shape))
    return pl.pallas_call(
        _merge_body,
        grid=(nt,),
        in_specs=[tok(N_BRANCH * D_MODEL), tok(HB), tok(HB), tok(HB), tok(HB), tok(D_MODEL),
                  full((N_BRANCH, HB, D_MODEL)), full((D_MODEL, D_MODEL)), full((1, D_MODEL)), full((1, D_MODEL)),
                  full((LANES, D_MODEL)), full((LANES, D_MODEL)), full((LANES, 1))],
        out_specs=[tok(D_MODEL), tok(SC_WORDS), tok(SC_WORDS), pl.BlockSpec((ROUTE_ROWS, tm), lambda i: (0, i)),
                   pl.BlockSpec((1, N_EXPERTS, LANES), lambda i: (i, 0, 0))],
        out_shape=[jax.ShapeDtypeStruct((t, D_MODEL), F32), jax.ShapeDtypeStruct((t, SC_WORDS), U32),
                   jax.ShapeDtypeStruct((t, SC_WORDS), U32), jax.ShapeDtypeStruct((ROUTE_ROWS, t), F32),
                   jax.ShapeDtypeStruct((nt, N_EXPERTS, LANES), F32)],
        compiler_params=_cparams("parallel"),
        name="merge_ln_router",
    )(h, *ys, x, wb, wo, lng, lnb, rwh, rwl, rb)


def _sc_mesh():
    return plsc.VectorSubcoreMesh(core_axis_name="core", subcore_axis_name="subcore")


def _sc_dispatch(xa, xb, pos, n_out):
    t, words = xa.shape
    k = pos.shape[0]
    out = jax.ShapeDtypeStruct((n_out, words), xa.dtype)

    @functools.partial(pl.kernel, out_type=(out, out), mesh=_sc_mesh())
    def scatter_rows(xa_hbm, xb_hbm, i_hbm, oa_hbm, ob_hbm):
        for x_hbm, o_hbm in ((xa_hbm, oa_hbm), (xb_hbm, ob_hbm)):
            def body(x_vmem, i_vmem, o_hbm=o_hbm):
                for kk in range(k):
                    pltpu.sync_copy(x_vmem, o_hbm.at[i_vmem.at[kk]])

            pltpu.emit_pipeline(
                body, grid=(t // SC_WINDOW,),
                in_specs=[pl.BlockSpec((SC_WINDOW, words), lambda i: (i, 0)),
                          pl.BlockSpec((k, SC_WINDOW), lambda i: (0, i))],
                out_specs=[],
                core_axis_name=("core", "subcore"),
                dimension_semantics=(pltpu.PARALLEL,),
            )(x_hbm, i_hbm)

    return scatter_rows(xa, xb, pos)


def _sc_collect(ya, yb, pos_flat):
    n = pos_flat.shape[1]
    words = ya.shape[1]
    out = jax.ShapeDtypeStruct((n, words), ya.dtype)

    @functools.partial(pl.kernel, out_type=(out, out), mesh=_sc_mesh())
    def gather_rows(ya_hbm, yb_hbm, i_hbm, oa_hbm, ob_hbm):
        for y_hbm, o_hbm in ((ya_hbm, oa_hbm), (yb_hbm, ob_hbm)):
            def body(i_vmem, o_vmem, y_hbm=y_hbm):
                pltpu.sync_copy(y_hbm.at[i_vmem.at[0]], o_vmem)

            pltpu.emit_pipeline(
                body, grid=(n // SC_WINDOW,),
                in_specs=[pl.BlockSpec((1, SC_WINDOW), lambda i: (0, i))],
                out_specs=[pl.BlockSpec((SC_WINDOW, words), lambda i: (i, 0))],
                core_axis_name=("core", "subcore"),
                dimension_semantics=(pltpu.PARALLEL,),
            )(i_hbm, o_hbm)

    return gather_rows(ya, yb, pos_flat)


def _expert_body(te_ref, xa_ref, xb_ref, wgl_ref, bgl_ref, wd_ref, bd_ref, ya_ref, yb_ref):
    pl.when(pl.program_id(0) < te_ref[pl.num_programs(0)])(
        functools.partial(_expert_tile, xa_ref, xb_ref, wgl_ref, bgl_ref, wd_ref, bd_ref, ya_ref, yb_ref))


def _expert_tile(xa_ref, xb_ref, wgl_ref, bgl_ref, wd_ref, bd_ref, ya_ref, yb_ref):
    x0, x1 = _unpack_halves(xa_ref[...])
    x2, x3 = _unpack_halves(xb_ref[...])
    x = jnp.concatenate([x0, x1, x2, x3], axis=1)
    gl = jnp.dot(x, wgl_ref[0], preferred_element_type=F32) + bgl_ref[0]
    nblk = 2 * D_FF // DEINT_BLOCK
    half = DEINT_BLOCK // 2
    glu = jnp.concatenate([gl[:, b * DEINT_BLOCK:b * DEINT_BLOCK + half] for b in range(nblk)], axis=1)
    lin = jnp.concatenate([gl[:, b * DEINT_BLOCK + half:(b + 1) * DEINT_BLOCK] for b in range(nblk)], axis=1)
    glu = jnp.minimum(glu, SWIGLU_LIMIT)
    lin = jnp.clip(lin, -SWIGLU_LIMIT, SWIGLU_LIMIT)
    act = (lin + 1.0) * glu * _sigmoid(SWIGLU_ALPHA * glu)
    y = _dot(act, wd_ref[0]) + bd_ref[0]
    ya_ref[...] = _pack_halves(y[:, :D_MODEL // 2])
    yb_ref[...] = _pack_halves(y[:, D_MODEL // 2:])


def _expert_mlp(tile_expert, xa, xb, wgl, bgl, wd, bd, first):
    n = xa.shape[0]
    tr = EXPERT_TILE
    tok = pl.BlockSpec((tr, SC_WORDS), lambda i, te: (i, 0))
    per_expert = lambda shape: pl.BlockSpec((1,) + shape, lambda i, te: (te[i] + first,) + (0,) * len(shape))
    out = jax.ShapeDtypeStruct((n, SC_WORDS), U32)
    return pl.pallas_call(
        _expert_body,
        grid_spec=pltpu.PrefetchScalarGridSpec(
            num_scalar_prefetch=1,
            grid=(n // tr,),
            in_specs=[tok, tok, per_expert((D_MODEL, 2 * D_FF)), per_expert((1, 2 * D_FF)),
                      per_expert((D_FF, D_MODEL)), per_expert((1, D_MODEL))],
            out_specs=[tok, tok],
        ),
        out_shape=[out, out],
        compiler_params=_cparams("arbitrary"),
        name="expert_mlp",
    )(tile_expert, xa, xb, wgl, bgl, wd, bd)


def _deinterleave_body(w_ref, p_ref, o_ref):
    for b in range(w_ref.shape[1] // DEINT_BLOCK):
        cols = slice(b * DEINT_BLOCK, (b + 1) * DEINT_BLOCK)
        o_ref[:, cols] = jnp.dot(w_ref[:, cols].astype(BF16), p_ref[...], preferred_element_type=F32).astype(o_ref.dtype)


def _deinterleave_perm():
    src = lax.broadcasted_iota(jnp.int32, (DEINT_BLOCK, DEINT_BLOCK), 0)
    dst = lax.broadcasted_iota(jnp.int32, (DEINT_BLOCK, DEINT_BLOCK), 1)
    half = DEINT_BLOCK // 2
    return jnp.where(src == jnp.where(dst < half, 2 * dst, 2 * (dst - half) + 1), 1.0, 0.0)


def _deinterleave(w):
    rows, n2 = w.shape
    tm = min(ROW_TILE, rows)
    return pl.pallas_call(
        _deinterleave_body,
        grid=(rows // tm,),
        in_specs=[pl.BlockSpec((tm, n2), lambda i: (i, 0)),
                  pl.BlockSpec((DEINT_BLOCK, DEINT_BLOCK), lambda i: (0, 0))],
        out_specs=pl.BlockSpec((tm, n2), lambda i: (i, 0)),
        out_shape=jax.ShapeDtypeStruct((rows, n2), BF16),
        compiler_params=_cparams("parallel"),
        name="deinterleave",
    )(w, _deinterleave_perm().astype(BF16))


def _combine_body(ya_ref, yb_ref, route_ref, x_ref, lng_ref, lnb_ref, xo_ref, xb_ref):
    x = x_ref[...]
    route = route_ref[...]
    half = D_MODEL // 2
    q = D_MODEL // 4
    ffn_a = jnp.zeros((x.shape[0], half), F32)
    ffn_b = jnp.zeros((x.shape[0], half), F32)
    for k in range(TOP_K):
        wk = route[:, k:k + 1]
        a0, a1 = _unpack_halves(ya_ref[k])
        b0, b1 = _unpack_halves(yb_ref[k])
        ffn_a = ffn_a + wk * jnp.concatenate([a0.astype(F32), a1.astype(F32)], axis=-1)
        ffn_b = ffn_b + wk * jnp.concatenate([b0.astype(F32), b1.astype(F32)], axis=-1)
    del q
    ffn = jnp.concatenate([ffn_a, ffn_b], axis=-1)
    xn = _layer_norm(DEEPNORM_ALPHA * x + ffn, lng_ref[...], lnb_ref[...])
    xo_ref[...] = xn
    xb_ref[...] = xn.astype(BF16)


def _combine(ya, yb, route, x, lng, lnb):
    t = x.shape[0]
    tm = min(ROW_TILE, t)
    tok = lambda width: pl.BlockSpec((tm, width), lambda i: (i, 0))
    gathered = pl.BlockSpec((TOP_K, tm, SC_WORDS), lambda i: (0, i, 0))
    full = lambda shape: pl.BlockSpec(shape, lambda i: (0,) * len(shape))
    return pl.pallas_call(
        _combine_body,
        grid=(t // tm,),
        in_specs=[gathered, gathered, tok(LANES), tok(D_MODEL), full((1, D_MODEL)), full((1, D_MODEL))],
        out_specs=[tok(D_MODEL), tok(D_MODEL)],
        out_shape=[jax.ShapeDtypeStruct((t, D_MODEL), F32), jax.ShapeDtypeStruct((t, D_MODEL), BF16)],
        compiler_params=_cparams("parallel"),
        name="combine_ln",
    )(ya, yb, route, x, lng, lnb)


def _route_positions(route, counts):
    t = route.shape[1]
    nt = counts.shape[0]
    tm = t // nt
    ids = route[TOP_K:2 * TOP_K].astype(jnp.int32).reshape(TOP_K, nt, tm)
    rank = route[2 * TOP_K:3 * TOP_K].astype(jnp.int32).reshape(TOP_K, nt, tm)
    cnt = counts[:, :, 0].astype(jnp.int32)
    before = jnp.cumsum(cnt, axis=0) - cnt
    total = jnp.sum(cnt, axis=0)
    padded = ((total + EXPERT_TILE - 1) // EXPERT_TILE) * EXPERT_TILE
    ends = jnp.cumsum(padded)
    offs = (ends - padded)[None, :] + before
    experts = jnp.arange(N_EXPERTS, dtype=jnp.int32)
    pos = rank + jnp.sum(jnp.where(ids[..., None] == experts, offs[None, :, None, :], 0), axis=-1)
    n_rows = TOP_K * t + N_EXPERTS * EXPERT_TILE
    tile_start = jnp.arange(n_rows // EXPERT_TILE, dtype=jnp.int32) * EXPERT_TILE
    tile_expert = jnp.minimum(jnp.sum((tile_start[:, None] >= ends[None, :]).astype(jnp.int32), axis=1), N_EXPERTS - 1)
    tiles_used = ends[N_EXPERTS - 1:] // EXPERT_TILE
    return pos.reshape(TOP_K, t), jnp.concatenate([tile_expert, tiles_used]), n_rows


def _rope_tables(seq):
    half = DIFF_DQK // 2
    inv_freq = ROPE_THETA ** (-jnp.arange(half, dtype=F32) / half)
    ang = jnp.arange(seq, dtype=F32)[:, None] * inv_freq[None, :]
    cos = jnp.cos(ang)
    sin = jnp.sin(ang)
    reps = DIFF_DV // DIFF_DQK
    cos_t = jnp.tile(jnp.concatenate([cos, cos], axis=-1), (1, reps))
    sin_t = jnp.tile(jnp.concatenate([-sin, sin], axis=-1), (1, reps))
    return cos_t, sin_t


def _lane_row(vec, offset):
    return jnp.zeros((LANES,), F32).at[offset:offset + vec.shape[0]].set(vec.astype(F32))


def _layer_weights(l, p):
    w_in = p['w_in'][l]
    w_big = jnp.concatenate([w_in[:, 5648:9744], w_in[:, 0:2048], w_in[:, 2056:5128], w_in[:, 5136:5648]],
                            axis=1).astype(BF16)
    w_small = jnp.concatenate([w_in[:, 2048:2056], w_in[:, 5128:5136],
                               jnp.zeros((D_MODEL, LANES - 16), F32)], axis=1).astype(BF16)
    par = jnp.zeros((SUBLANES, LANES), F32)
    gdn_par = par.at[0].set(_lane_row(p['gdn_a_log'][l], GDN_HEADS)).at[1].set(_lane_row(p['gdn_dt_bias'][l], GDN_HEADS))
    ssm_par = par.at[0].set(_lane_row(p['ssm_a_log'][l], SSM_HEADS)).at[1].set(_lane_row(p['ssm_dt_bias'][l], SSM_HEADS))
    rw = jnp.concatenate([p['router_w'][l].T, jnp.zeros((LANES - N_EXPERTS, D_MODEL), F32)], axis=0)
    rwh = rw.astype(BF16)
    bgu = p['b_gate_up'][l]
    return dict(
        w_big=w_big, w_small=w_small, gdn_par=gdn_par, ssm_par=ssm_par,
        gdn_conv_w=p['gdn_conv_w'][l], gdn_norm_w=p['gdn_norm_w'][l][None, :],
        lambda_init=0.8 - 0.6 * math.exp(-0.3 * l), diff_lambda=p['diff_lambda'][l],
        diff_norm_w=p['diff_norm_w'][l][None, :],
        ssm_conv_w=p['ssm_conv_w'][l], ssm_conv_b=p['ssm_conv_b'][l][None, :],
        dskip=jnp.repeat(p['ssm_d'][l], SSM_HEADDIM)[None, :], ssm_norm_w=p['ssm_norm_w'][l][None, :],
        w_mem=p['w_mem'][l].astype(BF16), w_branch=p['w_branch'][l].astype(BF16), w_out=p['w_out'][l].astype(BF16),
        ln1_g=p['ln1_g'][l][None, :], ln1_b=p['ln1_b'][l][None, :],
        rwh=rwh, rwl=(rw - rwh.astype(F32)).astype(BF16), rb=_lane_row(p['router_b'][l], 0)[:, None],
        first_expert=l * bgu.shape[0],
        ln2_g=p['ln2_g'][l][None, :], ln2_b=p['ln2_b'][l][None, :])


def _mix_and_route(w, x, xbf, kv_src, bsz, seq, cos, sin):
    t = bsz * seq
    h, hs = _in_proj(xbf, w['w_big'], w['w_small'], min(1024, t), H_WIDTH // 4)
    y_gdn = _gdn(h, hs, w['gdn_conv_w'], w['gdn_par'], w['gdn_norm_w'], bsz, seq)
    y_diff = _diff_attention(h, cos, sin, w['diff_lambda'], w['diff_norm_w'], w['lambda_init'], bsz, seq)
    y_ssm = _ssd(h, hs, w['ssm_conv_w'], w['ssm_conv_b'], w['ssm_par'], w['dskip'], w['ssm_norm_w'], bsz, seq)
    kv = _matmul(kv_src, w['w_mem'], BF16, min(1024, kv_src.shape[0]), HB, "mem_kv")
    y_mem = _mem_attention(h, kv, bsz, seq, kv_src.shape[0] // bsz)
    x1, xa, xb, route, counts = _merge(h, (y_gdn, y_diff, y_ssm, y_mem), x, w['w_branch'], w['w_out'],
                                       w['ln1_g'], w['ln1_b'], w['rwh'], w['rwl'], w['rb'])
    pos, tile_expert, n_rows = _route_positions(route, counts)
    xsa, xsb = _sc_dispatch(xa, xb, pos, n_rows)
    cw = jnp.zeros((t, LANES), F32).at[:, :TOP_K].set(route[:TOP_K].T)
    return dict(x1=x1, route=cw, pos=pos, tile_expert=tile_expert, xsa=xsa, xsb=xsb)


def _experts(w, ew, st):
    ysa, ysb = _expert_mlp(st['tile_expert'], st['xsa'], st['xsb'], ew['w_gate_lin'], ew['b_gate_lin'], ew['w_down'],
                           ew['b_down'], w['first_expert'])
    t = st['x1'].shape[0]
    ga, gb = _sc_collect(ysa, ysb, st['pos'].reshape(1, TOP_K * t))
    return dict(st, ga=ga.reshape(TOP_K, t, SC_WORDS), gb=gb.reshape(TOP_K, t, SC_WORDS))


def _finish(w, st):
    return _combine(st['ga'], st['gb'], st['route'], st['x1'], w['ln2_g'], w['ln2_b'])


def kernel(x, mem, w_in, gdn_conv_w, gdn_a_log, gdn_dt_bias, gdn_norm_w, diff_lambda, diff_norm_w, ssm_conv_w, ssm_conv_b, ssm_a_log, ssm_dt_bias, ssm_d, ssm_norm_w, w_mem, w_branch, w_out, ln1_g, ln1_b, router_w, router_b, w_gate_up, b_gate_up, w_down, b_down, ln2_g, ln2_b):
    p = dict(w_in=w_in, gdn_conv_w=gdn_conv_w, gdn_a_log=gdn_a_log, gdn_dt_bias=gdn_dt_bias, gdn_norm_w=gdn_norm_w,
             diff_lambda=diff_lambda, diff_norm_w=diff_norm_w, ssm_conv_w=ssm_conv_w, ssm_conv_b=ssm_conv_b,
             ssm_a_log=ssm_a_log, ssm_dt_bias=ssm_dt_bias, ssm_d=ssm_d, ssm_norm_w=ssm_norm_w, w_mem=w_mem,
             w_branch=w_branch, w_out=w_out, ln1_g=ln1_g, ln1_b=ln1_b, router_w=router_w, router_b=router_b,
             w_gate_up=w_gate_up, b_gate_up=b_gate_up, w_down=w_down, b_down=b_down, ln2_g=ln2_g, ln2_b=ln2_b)
    bsz, seq, d = x.shape
    depth, n_exp = w_gate_up.shape[0], w_gate_up.shape[1]
    expert_w = dict(
        w_gate_lin=_deinterleave(w_gate_up.reshape(depth * n_exp * d, 2 * D_FF)).reshape(depth * n_exp, d, 2 * D_FF),
        b_gate_lin=b_gate_up.reshape(depth * n_exp, 2 * D_FF // DEINT_BLOCK, DEINT_BLOCK // 2, 2)
        .transpose(0, 1, 3, 2).reshape(depth * n_exp, 1, 2 * D_FF),
        w_down=w_down.astype(BF16).reshape(depth * n_exp, D_FF, d),
        b_down=b_down.reshape(depth * n_exp, 1, d))
    cos, sin = _rope_tables(seq)
    groups = TOKEN_GROUPS if bsz % TOKEN_GROUPS == 0 else 1
    gb = bsz // groups
    xs = [x[g * gb:(g + 1) * gb].reshape(gb * seq, d) for g in range(groups)]
    xbfs = [v.astype(BF16) for v in xs]
    kvs = [mem[g * gb:(g + 1) * gb].reshape(gb * mem.shape[1], d).astype(BF16) for g in range(groups)]
    ws = [_layer_weights(l, p) for l in range(depth)]
    sts = [_mix_and_route(ws[0], xs[g], xbfs[g], kvs[g], gb, seq, cos, sin) for g in range(groups)]
    for l in range(depth):
        sts = [_experts(ws[l], expert_w, st) for st in sts]
        nxt = []
        for g in range(groups):
            xs[g], xbfs[g] = _finish(ws[l], sts[g])
            if l + 1 < depth:
                nxt.append(_mix_and_route(ws[l + 1], xs[g], xbfs[g], kvs[g], gb, seq, cos, sin))
        sts = nxt
    return jnp.concatenate([v.reshape(gb, seq, d) for v in xs], axis=0)
```

```python
import functools
import math

import jax
import jax.numpy as jnp
from jax import lax
from jax.experimental import pallas as pl
from jax.experimental.pallas import tpu as pltpu
from jax.experimental.pallas import tpu_sc as plsc

F32 = jnp.float32
BF16 = jnp.bfloat16
U32 = jnp.uint32

D_MODEL = 1024
DEPTH = 4
CHUNK = 64
Q_BLOCK = 128
CONV_WIDTH = 4
ROPE_THETA = 10000.0
NORM_EPS = 1e-6
LN_EPS = 1e-5

GDN_HEADS = 4
GDN_DK = 128
GDN_DV = 128
DIFF_HEADS = 4
DIFF_DQK = 64
DIFF_DV = 128
SSM_HEADS = 8
SSM_HEADDIM = 64
SSM_GROUPS = 2
SSM_STATE = 128
SSM_INNER = 512
MEM_HEADS = 4
MEM_HEADDIM = 128
N_BRANCH = 4
N_EXPERTS = 32
TOP_K = 4
D_FF = 1024
SWIGLU_LIMIT = 7.0
SWIGLU_ALPHA = 1.702
DEEPNORM_ALPHA = (2.0 * DEPTH) ** 0.25
LOG2_E = math.log2(math.e)

LANES = 128
SUBLANES = 8
HALO = 16
VMEM_LIMIT = 56 * 1024 * 1024

HB = 512
COL_GATES = 0
COL_GQ, COL_GK, COL_GV, COL_GZ = 8, 9, 10, 11
COL_DQ, COL_DK, COL_DV = 12, 13, 14
COL_SZ = 15
COL_SXBC = 8
COL_MQ = 18
H_WIDTH = 19 * HB

SEQ_TILE = 512
ROW_TILE = 512
EXPERT_TILE = 512
SC_WINDOW = 128
SC_WORDS = 256
ROUTE_ROWS = 16
DEINT_BLOCK = 2 * LANES
GDN_TILE = 512
DIFF_ROW_SPLIT = 2
TOKEN_GROUPS = 2


def _cparams(*sem):
    return pltpu.CompilerParams(dimension_semantics=sem, vmem_limit_bytes=VMEM_LIMIT)


def _dot(a, b):
    return jnp.dot(a.astype(BF16), b.astype(BF16), preferred_element_type=F32)


def _dot_nt(a, b):
    return lax.dot_general(a.astype(BF16), b.astype(BF16), (((1,), (1,)), ((), ())), preferred_element_type=F32)


def _bdot(a, b):
    return jnp.einsum('cmk,ckn->cmn', a.astype(BF16), b.astype(BF16), preferred_element_type=F32)


def _bdot_nt(a, b):
    return jnp.einsum('cmk,cnk->cmn', a.astype(BF16), b.astype(BF16), preferred_element_type=F32)


def _split2(a):
    hi = a.astype(BF16)
    lo = (a - hi.astype(F32)).astype(BF16)
    return hi, lo


def _sigmoid(x):
    return 0.5 * jnp.tanh(0.5 * x) + 0.5


def _silu(x):
    h = 0.5 * x
    return h + h * jnp.tanh(h)


def _softplus(x):
    return jnp.maximum(x, 0.0) + jnp.log1p(jnp.exp(-jnp.abs(x)))


def _rms(x, w):
    return x * lax.rsqrt(jnp.mean(x * x, axis=-1, keepdims=True) + NORM_EPS) * w


def _layer_norm(x, g, b):
    mu = jnp.mean(x, axis=-1, keepdims=True)
    xc = x - mu
    var = jnp.mean(xc * xc, axis=-1, keepdims=True)
    return xc * lax.rsqrt(var + LN_EPS) * g + b


def _mm_body(x_ref, w_ref, o_ref):
    o_ref[...] = jnp.dot(x_ref[...], w_ref[...], preferred_element_type=F32).astype(o_ref.dtype)


def _matmul(x, w, out_dtype, tm, tn, name):
    m, k = x.shape
    n = w.shape[1]
    return pl.pallas_call(
        _mm_body,
        grid=(m // tm, n // tn),
        in_specs=[pl.BlockSpec((tm, k), lambda i, j: (i, 0)), pl.BlockSpec((k, tn), lambda i, j: (0, j))],
        out_specs=pl.BlockSpec((tm, tn), lambda i, j: (i, j)),
        out_shape=jax.ShapeDtypeStruct((m, n), out_dtype),
        compiler_params=_cparams("parallel", "parallel"),
        name=name,
    )(x, w)


def _in_proj_body(x_ref, w_ref, ws_ref, o_ref, os_ref):
    x = x_ref[...]
    o_ref[...] = jnp.dot(x, w_ref[...], preferred_element_type=F32).astype(o_ref.dtype)

    @pl.when(pl.program_id(1) == 0)
    def _():
        os_ref[...] = jnp.dot(x, ws_ref[...], preferred_element_type=F32)


def _in_proj(x, w, w_small, tm, tn):
    m, k = x.shape
    n = w.shape[1]
    return pl.pallas_call(
        _in_proj_body,
        grid=(m // tm, n // tn),
        in_specs=[pl.BlockSpec((tm, k), lambda i, j: (i, 0)), pl.BlockSpec((k, tn), lambda i, j: (0, j)),
                  pl.BlockSpec((k, LANES), lambda i, j: (0, 0))],
        out_specs=[pl.BlockSpec((tm, tn), lambda i, j: (i, j)), pl.BlockSpec((tm, LANES), lambda i, j: (i, 0))],
        out_shape=[jax.ShapeDtypeStruct((m, n), BF16), jax.ShapeDtypeStruct((m, LANES), F32)],
        compiler_params=_cparams("parallel", "arbitrary"),
        name="in_proj",
    )(x, w, w_small)


def _shift_taps(nc):
    m = lax.broadcasted_iota(jnp.int32, (1, (CONV_WIDTH - 1) * CHUNK, CHUNK + HALO), 1)
    k = lax.broadcasted_iota(jnp.int32, (1, (CONV_WIDTH - 1) * CHUNK, CHUNK + HALO), 2)
    src = (m % CHUNK) + HALO - (CONV_WIDTH - 1 - m // CHUNK)
    return jnp.broadcast_to(_mask(k == src).astype(BF16), (nc, (CONV_WIDTH - 1) * CHUNK, CHUNK + HALO))


def _conv_silu(x, tail_ref, slot, xbuf_ref, w, bias, taps):
    rows, width = x.shape
    nc = rows // CHUNK
    xbuf_ref[0:HALO, :] = tail_ref[slot]
    xbuf_ref[HALO:HALO + rows, :] = x
    tail_ref[slot] = x[rows - HALO:rows, :]
    xext = jnp.concatenate([xbuf_ref[i * CHUNK:(i + 1) * CHUNK + HALO, :][None] for i in range(nc)], axis=0)
    shifted = _bdot(taps, xext)
    y = x.astype(F32).reshape(nc, CHUNK, width) * w[CONV_WIDTH - 1:CONV_WIDTH, :]
    for j in range(CONV_WIDTH - 1):
        y = y + shifted[:, j * CHUNK:(j + 1) * CHUNK, :] * w[j:j + 1, :]
    y = y.reshape(rows, width)
    if bias is not None:
        y = y + bias
    return _silu(y)


def _chunk_iota():
    r = lax.broadcasted_iota(jnp.int32, (1, CHUNK, CHUNK), 1)
    c = lax.broadcasted_iota(jnp.int32, (1, CHUNK, CHUNK), 2)
    return r, c


def _mask(cond):
    return jnp.where(cond, 1.0, 0.0)


def _segment_logdecay(g3, r, c):
    nb = g3.shape[0]
    lincl = jnp.broadcast_to(_mask(r >= c).astype(BF16), (nb, CHUNK, CHUNK))
    upper = _mask(r > c)
    ghi = g3.astype(BF16).astype(F32)
    glo = g3 - ghi
    return _bdot(lincl, ghi * upper) + _bdot(lincl, glo * upper)


def _inv_unit_lower(a, r, c):
    d = _mask(r == c) - a * _mask(((r >> 1) == (c >> 1)) & (r > c))
    for lb in range(1, 6):
        lower_left = _mask(((r >> (lb + 1)) == (c >> (lb + 1))) & ((r >> lb) > (c >> lb)))
        d = d - _bdot(_bdot(d, a * lower_left), d)
    return d


def _gdn_body(q_ref, k_ref, v_ref, z_ref, hs_ref, cw_ref, par_ref, nw_ref, o_ref,
              tail_ref, xbuf_ref, qs_ref, ks_ref, vs_ref, pt_ref, bt_ref, qwt_ref, qut_ref, ot_ref, gl_ref,
              state_ref):
    ts = q_ref.shape[0]
    nc = ts // CHUNK

    @pl.when(pl.program_id(1) == 0)
    def _():
        tail_ref[...] = jnp.zeros_like(tail_ref)
        state_ref[...] = jnp.zeros_like(state_ref)

    taps = _shift_taps(nc)
    for slot, (src, dst) in enumerate(((q_ref, qs_ref), (k_ref, ks_ref), (v_ref, vs_ref))):
        w = cw_ref[:, slot * HB:(slot + 1) * HB]
        dst[...] = _conv_silu(src[...], tail_ref, slot, xbuf_ref, w, None, taps)

    hs = hs_ref[...]
    beta_all = _sigmoid(hs)
    g_all = -jnp.exp(par_ref[0:1, :]) * _softplus(hs + par_ref[1:2, :])

    def stack(per_head):
        return jnp.concatenate([per_head(h) for h in range(GDN_HEADS)], axis=0)

    def l2(ref, h):
        x = ref[:, h * GDN_DK:(h + 1) * GDN_DK]
        return (x * lax.rsqrt(jnp.sum(x * x, axis=-1, keepdims=True) + NORM_EPS)).reshape(nc, CHUNK, GDN_DK)

    q3 = stack(lambda h: l2(qs_ref, h)) * (GDN_DK ** -0.5)
    k3 = stack(lambda h: l2(ks_ref, h))
    v3 = stack(lambda h: vs_ref[:, h * GDN_DV:(h + 1) * GDN_DV].reshape(nc, CHUNK, GDN_DV))
    b3 = stack(lambda h: beta_all[:, h:h + 1].reshape(nc, CHUNK, 1))
    g3 = stack(lambda h: g_all[:, GDN_HEADS + h:GDN_HEADS + h + 1].reshape(nc, CHUNK, 1))
    r, c = _chunk_iota()
    m = _segment_logdecay(g3, r, c)
    em = jnp.exp(m)
    a = _bdot_nt(k3, k3) * b3 * (em * _mask(r > c))
    t = _inv_unit_lower(a, r, c)
    gc = m[:, :, 0:1] + g3[:, 0:1, :]
    eg = jnp.exp(gc)
    g_last = gc[:, CHUNK - 1:CHUNK, :]
    sol = _bdot(t, jnp.concatenate([v3 * b3, k3 * (b3 * eg)], axis=-1))
    u = sol[:, :, :GDN_DV]
    w = sol[:, :, GDN_DV:]
    kd = k3 * jnp.exp(g_last - gc)
    qk = _bdot_nt(q3, k3) * (em * _mask(r >= c))
    pt = _bdot(jnp.swapaxes(w, 1, 2), kd).astype(BF16)
    bt = _bdot(jnp.swapaxes(u, 1, 2), kd)
    qwt = jnp.swapaxes(q3 * eg - _bdot(qk, w), 1, 2).astype(BF16)
    qut = jnp.swapaxes(_bdot(qk, u), 1, 2)
    gl = jnp.broadcast_to(jnp.exp(g_last), (GDN_HEADS * nc, SUBLANES, LANES))
    for h in range(GDN_HEADS):
        hb = slice(h * nc, (h + 1) * nc)
        pt_ref[h] = pt[hb]
        bt_ref[h] = bt[hb]
        qwt_ref[h] = qwt[hb]
        qut_ref[h] = qut[hb]
        gl_ref[h] = gl[hb]

    def chunk_step(ci, carry):
        for h in range(GDN_HEADS):
            st = state_ref[h]
            stb = st.astype(BF16)
            ot_ref[h, ci] = jnp.dot(stb, qwt_ref[h, ci], preferred_element_type=F32) + qut_ref[h, ci]
            state_ref[h] = (st * gl_ref[h, ci][0:1, :] - jnp.dot(stb, pt_ref[h, ci], preferred_element_type=F32)
                            + bt_ref[h, ci])
        return carry

    lax.fori_loop(0, nc, chunk_step, 0)

    for h in range(GDN_HEADS):
        sl = slice(h * GDN_DV, (h + 1) * GDN_DV)
        o = jnp.swapaxes(ot_ref[h], 1, 2).reshape(ts, GDN_DV)
        o_ref[:, sl] = (_rms(o, nw_ref[...]) * _silu(z_ref[:, sl].astype(F32))).astype(o_ref.dtype)


def _gdn(h, hs, conv_w, par, norm_w, bsz, seq):
    ts = min(GDN_TILE, seq)
    ns = seq // ts
    tok = lambda col: pl.BlockSpec((ts, HB), lambda b, s: (b * ns + s, col))
    full = lambda shape: pl.BlockSpec(shape, lambda b, s: (0,) * len(shape))
    nc = ts // CHUNK
    return pl.pallas_call(
        _gdn_body,
        grid=(bsz, ns),
        in_specs=[tok(COL_GQ), tok(COL_GK), tok(COL_GV), tok(COL_GZ),
                  pl.BlockSpec((ts, LANES), lambda b, s: (b * ns + s, 0)),
                  full((CONV_WIDTH, 3 * HB)), full((SUBLANES, LANES)), full((1, GDN_DV))],
        out_specs=pl.BlockSpec((ts, HB), lambda b, s: (b * ns + s, 0)),
        out_shape=jax.ShapeDtypeStruct((bsz * seq, HB), BF16),
        scratch_shapes=[
            pltpu.VMEM((3, HALO, HB), BF16),
            pltpu.VMEM((ts + HALO, HB), BF16),
            pltpu.VMEM((ts, HB), F32), pltpu.VMEM((ts, HB), F32), pltpu.VMEM((ts, HB), F32),
            pltpu.VMEM((GDN_HEADS, nc, GDN_DK, GDN_DK), BF16), pltpu.VMEM((GDN_HEADS, nc, GDN_DV, GDN_DK), F32),
            pltpu.VMEM((GDN_HEADS, nc, GDN_DK, CHUNK), BF16), pltpu.VMEM((GDN_HEADS, nc, GDN_DV, CHUNK), F32),
            pltpu.VMEM((GDN_HEADS, nc, GDN_DV, CHUNK), F32),
            pltpu.VMEM((GDN_HEADS, nc, SUBLANES, LANES), F32),
            pltpu.VMEM((GDN_HEADS, GDN_DV, GDN_DK), F32),
        ],
        compiler_params=_cparams("parallel", "arbitrary"),
        name="gdn",
    )(h, h, h, h, hs, conv_w, par, norm_w)


def _expand_heads(x, expand_hi):
    hi = x.astype(BF16)
    r1 = x - hi.astype(F32)
    mid = r1.astype(BF16)
    lo = (r1 - mid.astype(F32)).astype(BF16)
    e = expand_hi
    return (jnp.dot(hi, e, preferred_element_type=F32) + jnp.dot(mid, e, preferred_element_type=F32)
            + jnp.dot(lo, e, preferred_element_type=F32))


def _ssd_body(z_ref, xbc_ref, hs_ref, cw_ref, cb_ref, par_ref, dskip_ref, nw_ref, o_ref,
              tail_ref, xbuf_ref, xs_ref, bc_ref, yd_ref, hin_ref, state_ref):
    ts = z_ref.shape[0]
    nc = ts // CHUNK
    gw = SSM_INNER // SSM_GROUPS

    @pl.when(pl.program_id(1) == 0)
    def _():
        tail_ref[...] = jnp.zeros_like(tail_ref)
        state_ref[...] = jnp.zeros_like(state_ref)

    taps = _shift_taps(nc)
    for slot, dst in enumerate((xs_ref, bc_ref)):
        cols = slice(slot * HB, (slot + 1) * HB)
        dst[...] = _conv_silu(xbc_ref[:, cols], tail_ref, slot, xbuf_ref, cw_ref[:, cols], cb_ref[:, cols], taps)

    hs = hs_ref[...]
    dt = _softplus(hs + par_ref[1:2, :])
    a_all = -jnp.exp(par_ref[0:1, :]) * dt
    r, c = _chunk_iota()
    tril = _mask(r >= c)
    lincl = jnp.broadcast_to(tril.astype(BF16), (nc, CHUNK, CHUNK))
    a3 = a_all.reshape(nc, CHUNK, LANES)
    ahi = a3.astype(BF16)
    ar = a3 - ahi.astype(F32)
    amid = ar.astype(BF16)
    alo = (ar - amid.astype(F32)).astype(BF16)
    acum = _bdot(lincl, ahi) + _bdot(lincl, amid) + _bdot(lincl, alo)
    lane = lax.broadcasted_iota(jnp.int32, (LANES, SSM_INNER), 0)
    chan = lax.broadcasted_iota(jnp.int32, (LANES, SSM_INNER), 1)
    expand = jnp.where(lane - SSM_HEADS == chan // SSM_HEADDIM, 1.0, 0.0).astype(BF16)
    dt_x = _expand_heads(dt, expand)
    acum_x = _expand_heads(acum.reshape(ts, LANES), expand).reshape(nc, CHUNK, SSM_INNER)
    a_last_x = acum_x[:, CHUNK - 1:CHUNK, :]
    ea_x = jnp.exp(acum_x).reshape(ts, SSM_INNER)
    din_x = jnp.exp(a_last_x - acum_x).reshape(ts, SSM_INNER)
    cd_x = jnp.exp(a_last_x)

    xs = xs_ref[...]
    xdt = xs * dt_x
    xw = xdt * din_x
    for g in range(SSM_GROUPS):
        gcols = slice(g * gw, (g + 1) * gw)
        b3 = bc_ref[:, g * SSM_STATE:(g + 1) * SSM_STATE].reshape(nc, CHUNK, SSM_STATE)
        c3 = bc_ref[:, gw + g * SSM_STATE:gw + (g + 1) * SSM_STATE].reshape(nc, CHUNK, SSM_STATE)
        cb = _bdot_nt(c3, b3)
        hpg = SSM_HEADS // SSM_GROUPS
        heads = [g * hpg + hr for hr in range(hpg)]
        ah3 = jnp.concatenate([a_all[:, SSM_HEADS + hd:SSM_HEADS + hd + 1].reshape(nc, CHUNK, 1) for hd in heads],
                              axis=0)
        x4 = jnp.concatenate([xdt[:, hd * SSM_HEADDIM:(hd + 1) * SSM_HEADDIM].reshape(nc, CHUNK, SSM_HEADDIM)
                              for hd in heads], axis=0)
        seg = jnp.exp(_segment_logdecay(ah3, r, c)) * tril
        yd = _bdot(seg * jnp.concatenate([cb] * hpg, axis=0), x4)
        for hr, hd in enumerate(heads):
            yd_ref[:, hd * SSM_HEADDIM:(hd + 1) * SSM_HEADDIM] = yd[hr * nc:(hr + 1) * nc].reshape(ts, SSM_HEADDIM)
        bt = jnp.swapaxes(b3, 1, 2)
        st = _bdot(bt, xw[:, gcols].reshape(nc, CHUNK, gw))
        hcur = state_ref[g]
        for ci in range(nc):
            hin_ref[ci] = hcur
            hcur = hcur * cd_x[ci, 0:1, gcols] + st[ci]
        state_ref[g] = hcur
        y_off = _bdot(c3, hin_ref[...]).reshape(ts, gw) * ea_x[:, gcols]
        y = yd_ref[:, gcols] + y_off + dskip_ref[:, gcols] * xs[:, gcols]
        gated = y * _silu(z_ref[:, gcols].astype(F32))
        o_ref[:, gcols] = _rms(gated, nw_ref[:, gcols]).astype(o_ref.dtype)


def _ssd(h, hs, conv_w, conv_b, par, dskip, norm_w, bsz, seq):
    ts = min(SEQ_TILE, seq)
    ns = seq // ts
    nc = ts // CHUNK
    gw = SSM_INNER // SSM_GROUPS
    full = lambda shape: pl.BlockSpec(shape, lambda b, s: (0,) * len(shape))
    return pl.pallas_call(
        _ssd_body,
        grid=(bsz, ns),
        in_specs=[pl.BlockSpec((ts, HB), lambda b, s: (b * ns + s, COL_SZ)),
                  pl.BlockSpec((ts, 2 * HB), lambda b, s: (b * ns + s, COL_SXBC)),
                  pl.BlockSpec((ts, LANES), lambda b, s: (b * ns + s, 0)),
                  full((CONV_WIDTH, 2 * HB)), full((1, 2 * HB)), full((SUBLANES, LANES)),
                  full((1, SSM_INNER)), full((1, SSM_INNER))],
        out_specs=pl.BlockSpec((ts, HB), lambda b, s: (b * ns + s, 0)),
        out_shape=jax.ShapeDtypeStruct((bsz * seq, HB), BF16),
        scratch_shapes=[
            pltpu.VMEM((2, HALO, HB), BF16),
            pltpu.VMEM((ts + HALO, HB), BF16),
            pltpu.VMEM((ts, HB), F32), pltpu.VMEM((ts, HB), F32), pltpu.VMEM((ts, HB), F32),
            pltpu.VMEM((nc, SSM_STATE, gw), F32),
            pltpu.VMEM((SSM_GROUPS, SSM_STATE, gw), F32),
        ],
        compiler_params=_cparams("parallel", "arbitrary"),
        name="ssd",
    )(h, h, hs, conv_w, conv_b, par, dskip, norm_w)


def _rope(x, cos, sin_signed):
    width = x.shape[-1]
    half = DIFF_DQK // 2
    lane = lax.broadcasted_iota(jnp.int32, x.shape, x.ndim - 1)
    fwd = pltpu.roll(x, width - half, axis=x.ndim - 1)
    bwd = pltpu.roll(x, half, axis=x.ndim - 1)
    partner = jnp.where((lane & half) == 0, fwd, bwd)
    return x * cos + partner * sin_signed


def _diff_body(q_ref, k_ref, v_ref, cos_ref, sin_ref, lam_ref, nw_ref, o_ref, kr_ref, *, lambda_init, nq):
    qi = pl.program_id(2)
    tq = q_ref.shape[0]
    q0 = pl.multiple_of(qi * tq, tq)
    cos = cos_ref[pl.ds(q0, tq), :]
    sin = sin_ref[pl.ds(q0, tq), :]
    kr_ref[pl.ds(q0, tq), :] = _rope(k_ref[...].astype(F32), cos, sin).astype(BF16)
    qr = _rope(q_ref[...].astype(F32), cos, sin) * (DIFF_DQK ** -0.5 * LOG2_E)

    lp = lam_ref[...]
    prod01 = jnp.sum(lp[0:1, :] * lp[1:2, :], axis=-1, keepdims=True)
    prod23 = jnp.sum(lp[2:3, :] * lp[3:4, :], axis=-1, keepdims=True)
    lam = jnp.exp(prod01) - jnp.exp(prod23) + lambda_init

    lane = lax.broadcasted_iota(jnp.int32, (tq, 2 * DIFF_DQK), 1)
    qmaps = (jnp.where(lane < DIFF_DQK, qr, 0.0).astype(BF16), jnp.where(lane >= DIFF_DQK, qr, 0.0).astype(BF16))
    th = tq // DIFF_ROW_SPLIT

    for i in range(nq):

        @pl.when(qi == i)
        def _(i=i):
            lo = i * tq
            for rh in range(DIFF_ROW_SPLIT):
                rows = slice(rh * th, (rh + 1) * th)
                own = (rh + 1) * th
                row = lax.broadcasted_iota(jnp.int32, (th, own), 0) + rh * th
                col = lax.broadcasted_iota(jnp.int32, (th, own), 1)
                diag_ok = (row // CHUNK) >= (col // CHUNK)
                outs = []
                for qmap in qmaps:
                    qm = qmap[rows, :]
                    s_diag = jnp.where(diag_ok, _dot_nt(qm, kr_ref[lo:lo + own, :]), -jnp.inf)
                    mx = jnp.max(s_diag, axis=-1, keepdims=True)
                    if i > 0:
                        s_off = _dot_nt(qm, kr_ref[0:lo, :])
                        mx = jnp.maximum(mx, jnp.max(s_off, axis=-1, keepdims=True))
                    p_diag = jnp.exp2(s_diag - mx)
                    denom = jnp.sum(p_diag, axis=-1, keepdims=True)
                    acc = _dot(p_diag, v_ref[lo:lo + own, :])
                    if i > 0:
                        p_off = jnp.exp2(s_off - mx)
                        denom = denom + jnp.sum(p_off, axis=-1, keepdims=True)
                        acc = acc + _dot(p_off, v_ref[0:lo, :])
                    outs.append(acc / denom)
                o = outs[0] - lam * outs[1]
                o_ref[rows, :] = (_rms(o, nw_ref[...]) * (1.0 - lambda_init)).astype(o_ref.dtype)


def _diff_attention(h, cos, sin, lam_params, norm_w, lambda_init, bsz, seq):
    tq = min(SEQ_TILE, seq)
    nq = seq // tq
    per_head = HB // DIFF_DV
    full = lambda shape: pl.BlockSpec(shape, lambda b, hd, q: (0,) * len(shape))
    tok = lambda col: pl.BlockSpec((tq, DIFF_DV), lambda b, hd, q: (b * nq + q, col * per_head + hd))
    return pl.pallas_call(
        functools.partial(_diff_body, lambda_init=lambda_init, nq=nq),
        grid=(bsz, DIFF_HEADS, nq),
        in_specs=[tok(COL_DQ), tok(COL_DK),
                  pl.BlockSpec((seq, DIFF_DV), lambda b, hd, q: (b, COL_DV * per_head + hd)),
                  full((seq, DIFF_DV)), full((seq, DIFF_DV)),
                  full((4, DIFF_DQK)), full((1, DIFF_DV))],
        out_specs=pl.BlockSpec((tq, DIFF_DV), lambda b, hd, q: (b * nq + q, hd)),
        out_shape=jax.ShapeDtypeStruct((bsz * seq, HB), BF16),
        scratch_shapes=[pltpu.VMEM((seq, DIFF_DV), BF16)],
        compiler_params=_cparams("parallel", "parallel", "arbitrary"),
        name="diff_attn",
    )(h, h, h, cos, sin, lam_params, norm_w)


def _mem_body(q_ref, kv_ref, o_ref):
    width = MEM_HEADS * MEM_HEADDIM
    for h in range(MEM_HEADS):
        cols = slice(h * MEM_HEADDIM, (h + 1) * MEM_HEADDIM)
        q = q_ref[:, cols].astype(F32) * (MEM_HEADDIM ** -0.5)
        s = _dot_nt(q, kv_ref[:, cols])
        m = jnp.max(s, axis=-1, keepdims=True)
        p = jnp.exp(s - m)
        p = p / jnp.sum(p, axis=-1, keepdims=True)
        o_ref[:, cols] = _dot(p, kv_ref[:, width + h * MEM_HEADDIM:width + (h + 1) * MEM_HEADDIM]).astype(o_ref.dtype)


def _mem_attention(h, kv, bsz, seq, mem_len):
    tm = min(ROW_TILE, seq)
    ns = seq // tm
    return pl.pallas_call(
        _mem_body,
        grid=(bsz, ns),
        in_specs=[pl.BlockSpec((tm, HB), lambda b, s: (b * ns + s, COL_MQ)),
                  pl.BlockSpec((mem_len, 2 * HB), lambda b, s: (b, 0))],
        out_specs=pl.BlockSpec((tm, HB), lambda b, s: (b * ns + s, 0)),
        out_shape=jax.ShapeDtypeStruct((bsz * seq, HB), BF16),
        compiler_params=_cparams("parallel", "parallel"),
        name="mem_attn",
    )(h, kv)


def _pack_halves(x):
    n = x.shape[1] // 2
    lo = pltpu.bitcast(x[:, :n].astype(BF16).astype(F32), U32)
    hi = pltpu.bitcast(x[:, n:].astype(BF16).astype(F32), U32)
    return (lo >> 16) | hi


def _unpack_halves(w):
    lo = pltpu.bitcast(w << 16, F32)
    hi = pltpu.bitcast(w & jnp.uint32(0xFFFF0000), F32)
    return lo.astype(BF16), hi.astype(BF16)


def _merge_body(g_ref, y0_ref, y1_ref, y2_ref, y3_ref, x_ref, wb_ref, wo_ref, lng_ref, lnb_ref,
                rwh_ref, rwl_ref, rb_ref, xo_ref, pa_ref, pb_ref, route_ref, cnt_ref):
    tm = x_ref.shape[0]
    merged = jnp.zeros((tm, D_MODEL), F32)
    for i, y_ref in enumerate((y0_ref, y1_ref, y2_ref, y3_ref)):
        gate = _sigmoid(g_ref[:, i * D_MODEL:(i + 1) * D_MODEL].astype(F32))
        merged = merged + gate * jnp.dot(y_ref[...], wb_ref[i], preferred_element_type=F32)
    mix = _dot(merged, wo_ref[...])
    xn = _layer_norm(DEEPNORM_ALPHA * x_ref[...] + mix, lng_ref[...], lnb_ref[...])
    xo_ref[...] = xn
    pa_ref[...] = _pack_halves(xn[:, :D_MODEL // 2])
    pb_ref[...] = _pack_halves(xn[:, D_MODEL // 2:])

    xh, xl = _split2(xn)
    logits = (_dot_nt(rwh_ref[...], xh) + (_dot_nt(rwl_ref[...], xh) + _dot_nt(rwh_ref[...], xl)))[:N_EXPERTS, :]
    work = logits + rb_ref[0:N_EXPERTS, 0:1]
    expert = lax.broadcasted_iota(jnp.int32, (N_EXPERTS, tm), 0)
    vals, ids, sels = [], [], []
    for _ in range(TOP_K):
        mx = jnp.max(work, axis=0, keepdims=True)
        idx = jnp.min(jnp.where(work == mx, expert, N_EXPERTS), axis=0, keepdims=True)
        sel = expert == idx
        vals.append(mx)
        ids.append(idx)
        sels.append(sel)
        work = jnp.where(sel, -jnp.inf, work)
    exps = [jnp.exp(v - vals[0]) for v in vals]
    denom = exps[0] + exps[1] + exps[2] + exps[3]
    chosen = jnp.where(sels[0] | sels[1] | sels[2] | sels[3], 1.0, 0.0)
    r = lax.broadcasted_iota(jnp.int32, (tm, tm), 0)
    c = lax.broadcasted_iota(jnp.int32, (tm, tm), 1)
    utri = jnp.where(r <= c, 1.0, 0.0).astype(BF16)
    cum = jnp.dot(chosen.astype(BF16), utri, preferred_element_type=F32)
    ranks = [jnp.sum(jnp.where(sels[k], cum, 0.0), axis=0, keepdims=True) - 1.0 for k in range(TOP_K)]
    rows = [e / denom for e in exps] + [i.astype(F32) for i in ids] + ranks
    rows.append(jnp.zeros((ROUTE_ROWS - len(rows), tm), F32))
    route_ref[...] = jnp.concatenate(rows, axis=0)
    cnt_ref[0] = jnp.broadcast_to(cum[:, tm - 1:tm], (N_EXPERTS, LANES))


def _merge(h, ys, x, wb, wo, lng, lnb, rwh, rwl, rb):
    t = x.shape[0]
    tm = min(ROW_TILE, t)
    nt = t // tm
    tok = lambda width: pl.BlockSpec((tm, width), lambda i: (i, 0))
    full = lambda shape: pl.BlockSpec(shape, lambda i: (0,) * len(shape))
    return pl.pallas_call(
        _merge_body,
        grid=(nt,),
        in_specs=[tok(N_BRANCH * D_MODEL), tok(HB), tok(HB), tok(HB), tok(HB), tok(D_MODEL),
                  full((N_BRANCH, HB, D_MODEL)), full((D_MODEL, D_MODEL)), full((1, D_MODEL)), full((1, D_MODEL)),
                  full((LANES, D_MODEL)), full((LANES, D_MODEL)), full((LANES, 1))],
        out_specs=[tok(D_MODEL), tok(SC_WORDS), tok(SC_WORDS), pl.BlockSpec((ROUTE_ROWS, tm), lambda i: (0, i)),
                   pl.BlockSpec((1, N_EXPERTS, LANES), lambda i: (i, 0, 0))],
        out_shape=[jax.ShapeDtypeStruct((t, D_MODEL), F32), jax.ShapeDtypeStruct((t, SC_WORDS), U32),
                   jax.ShapeDtypeStruct((t, SC_WORDS), U32), jax.ShapeDtypeStruct((ROUTE_ROWS, t), F32),
                   jax.ShapeDtypeStruct((nt, N_EXPERTS, LANES), F32)],
        compiler_params=_cparams("parallel"),
        name="merge_ln_router",
    )(h, *ys, x, wb, wo, lng, lnb, rwh, rwl, rb)


def _sc_mesh():
    return plsc.VectorSubcoreMesh(core_axis_name="core", subcore_axis_name="subcore")


def _sc_dispatch(xa, xb, pos, n_out):
    t, words = xa.shape
    k = pos.shape[0]
    out = jax.ShapeDtypeStruct((n_out, words), xa.dtype)

    @functools.partial(pl.kernel, out_type=(out, out), mesh=_sc_mesh())
    def scatter_rows(xa_hbm, xb_hbm, i_hbm, oa_hbm, ob_hbm):
        for x_hbm, o_hbm in ((xa_hbm, oa_hbm), (xb_hbm, ob_hbm)):
            def body(x_vmem, i_vmem, o_hbm=o_hbm):
                for kk in range(k):
                    pltpu.sync_copy(x_vmem, o_hbm.at[i_vmem.at[kk]])

            pltpu.emit_pipeline(
                body, grid=(t // SC_WINDOW,),
                in_specs=[pl.BlockSpec((SC_WINDOW, words), lambda i: (i, 0)),
                          pl.BlockSpec((k, SC_WINDOW), lambda i: (0, i))],
                out_specs=[],
                core_axis_name=("core", "subcore"),
                dimension_semantics=(pltpu.PARALLEL,),
            )(x_hbm, i_hbm)

    return scatter_rows(xa, xb, pos)


def _sc_collect(ya, yb, pos_flat):
    n = pos_flat.shape[1]
    words = ya.shape[1]
    out = jax.ShapeDtypeStruct((n, words), ya.dtype)

    @functools.partial(pl.kernel, out_type=(out, out), mesh=_sc_mesh())
    def gather_rows(ya_hbm, yb_hbm, i_hbm, oa_hbm, ob_hbm):
        for y_hbm, o_hbm in ((ya_hbm, oa_hbm), (yb_hbm, ob_hbm)):
            def body(i_vmem, o_vmem, y_hbm=y_hbm):
                pltpu.sync_copy(y_hbm.at[i_vmem.at[0]], o_vmem)

            pltpu.emit_pipeline(
                body, grid=(n // SC_WINDOW,),
                in_specs=[pl.BlockSpec((1, SC_WINDOW), lambda i: (0, i))],
                out_specs=[pl.BlockSpec((SC_WINDOW, words), lambda i: (i, 0))],
                core_axis_name=("core", "subcore"),
                dimension_semantics=(pltpu.PARALLEL,),
            )(i_hbm, o_hbm)

    return gather_rows(ya, yb, pos_flat)


def _expert_body(te_ref, xa_ref, xb_ref, wgl_ref, bgl_ref, wd_ref, bd_ref, ya_ref, yb_ref):
    pl.when(pl.program_id(0) < te_ref[pl.num_programs(0)])(
        functools.partial(_expert_tile, xa_ref, xb_ref, wgl_ref, bgl_ref, wd_ref, bd_ref, ya_ref, yb_ref))


def _expert_tile(xa_ref, xb_ref, wgl_ref, bgl_ref, wd_ref, bd_ref, ya_ref, yb_ref):
    x0, x1 = _unpack_halves(xa_ref[...])
    x2, x3 = _unpack_halves(xb_ref[...])
    x = jnp.concatenate([x0, x1, x2, x3], axis=1)
    gl = jnp.dot(x, wgl_ref[0], preferred_element_type=F32) + bgl_ref[0]
    nblk = 2 * D_FF // DEINT_BLOCK
    half = DEINT_BLOCK // 2
    glu = jnp.concatenate([gl[:, b * DEINT_BLOCK:b * DEINT_BLOCK + half] for b in range(nblk)], axis=1)
    lin = jnp.concatenate([gl[:, b * DEINT_BLOCK + half:(b + 1) * DEINT_BLOCK] for b in range(nblk)], axis=1)
    glu = jnp.minimum(glu, SWIGLU_LIMIT)
    lin = jnp.clip(lin, -SWIGLU_LIMIT, SWIGLU_LIMIT)
    act = (lin + 1.0) * glu * (0.5 * jnp.tanh((0.5 * SWIGLU_ALPHA) * glu) + 0.5)
    y = _dot(act, wd_ref[0]) + bd_ref[0]
    ya_ref[...] = _pack_halves(y[:, :D_MODEL // 2])
    yb_ref[...] = _pack_halves(y[:, D_MODEL // 2:])


def _expert_mlp(tile_expert, xa, xb, wgl, bgl, wd, bd, first):
    n = xa.shape[0]
    tr = EXPERT_TILE
    tok = pl.BlockSpec((tr, SC_WORDS), lambda i, te: (i, 0))
    per_expert = lambda shape: pl.BlockSpec((1,) + shape, lambda i, te: (te[i] + first,) + (0,) * len(shape))
    out = jax.ShapeDtypeStruct((n, SC_WORDS), U32)
    return pl.pallas_call(
        _expert_body,
        grid_spec=pltpu.PrefetchScalarGridSpec(
            num_scalar_prefetch=1,
            grid=(n // tr,),
            in_specs=[tok, tok, per_expert((D_MODEL, 2 * D_FF)), per_expert((1, 2 * D_FF)),
                      per_expert((D_FF, D_MODEL)), per_expert((1, D_MODEL))],
            out_specs=[tok, tok],
        ),
        out_shape=[out, out],
        compiler_params=_cparams("arbitrary"),
        name="expert_mlp",
    )(tile_expert, xa, xb, wgl, bgl, wd, bd)


def _deinterleave_body(w_ref, p_ref, o_ref):
    for b in range(w_ref.shape[1] // DEINT_BLOCK):
        cols = slice(b * DEINT_BLOCK, (b + 1) * DEINT_BLOCK)
        o_ref[:, cols] = jnp.dot(w_ref[:, cols].astype(BF16), p_ref[...], preferred_element_type=F32).astype(o_ref.dtype)


def _deinterleave_perm():
    src = lax.broadcasted_iota(jnp.int32, (DEINT_BLOCK, DEINT_BLOCK), 0)
    dst = lax.broadcasted_iota(jnp.int32, (DEINT_BLOCK, DEINT_BLOCK), 1)
    half = DEINT_BLOCK // 2
    return jnp.where(src == jnp.where(dst < half, 2 * dst, 2 * (dst - half) + 1), 1.0, 0.0)


def _deinterleave(w):
    rows, n2 = w.shape
    tm = min(ROW_TILE, rows)
    return pl.pallas_call(
        _deinterleave_body,
        grid=(rows // tm,),
        in_specs=[pl.BlockSpec((tm, n2), lambda i: (i, 0)),
                  pl.BlockSpec((DEINT_BLOCK, DEINT_BLOCK), lambda i: (0, 0))],
        out_specs=pl.BlockSpec((tm, n2), lambda i: (i, 0)),
        out_shape=jax.ShapeDtypeStruct((rows, n2), BF16),
        compiler_params=_cparams("parallel"),
        name="deinterleave",
    )(w, _deinterleave_perm().astype(BF16))


def _combine_body(ya_ref, yb_ref, route_ref, x_ref, lng_ref, lnb_ref, xo_ref, xb_ref):
    x = x_ref[...]
    route = route_ref[...]
    half = D_MODEL // 2
    q = D_MODEL // 4
    ffn_a = jnp.zeros((x.shape[0], half), F32)
    ffn_b = jnp.zeros((x.shape[0], half), F32)
    for k in range(TOP_K):
        wk = route[:, k:k + 1]
        a0, a1 = _unpack_halves(ya_ref[k])
        b0, b1 = _unpack_halves(yb_ref[k])
        ffn_a = ffn_a + wk * jnp.concatenate([a0.astype(F32), a1.astype(F32)], axis=-1)
        ffn_b = ffn_b + wk * jnp.concatenate([b0.astype(F32), b1.astype(F32)], axis=-1)
    del q
    ffn = jnp.concatenate([ffn_a, ffn_b], axis=-1)
    xn = _layer_norm(DEEPNORM_ALPHA * x + ffn, lng_ref[...], lnb_ref[...])
    xo_ref[...] = xn
    xb_ref[...] = xn.astype(BF16)


def _combine(ya, yb, route, x, lng, lnb):
    t = x.shape[0]
    tm = min(ROW_TILE, t)
    tok = lambda width: pl.BlockSpec((tm, width), lambda i: (i, 0))
    gathered = pl.BlockSpec((TOP_K, tm, SC_WORDS), lambda i: (0, i, 0))
    full = lambda shape: pl.BlockSpec(shape, lambda i: (0,) * len(shape))
    return pl.pallas_call(
        _combine_body,
        grid=(t // tm,),
        in_specs=[gathered, gathered, tok(LANES), tok(D_MODEL), full((1, D_MODEL)), full((1, D_MODEL))],
        out_specs=[tok(D_MODEL), tok(D_MODEL)],
        out_shape=[jax.ShapeDtypeStruct((t, D_MODEL), F32), jax.ShapeDtypeStruct((t, D_MODEL), BF16)],
        compiler_params=_cparams("parallel"),
        name="combine_ln",
    )(ya, yb, route, x, lng, lnb)


def _route_positions(route, counts):
    t = route.shape[1]
    nt = counts.shape[0]
    tm = t // nt
    ids = route[TOP_K:2 * TOP_K].astype(jnp.int32).reshape(TOP_K, nt, tm)
    rank = route[2 * TOP_K:3 * TOP_K].astype(jnp.int32).reshape(TOP_K, nt, tm)
    cnt = counts[:, :, 0].astype(jnp.int32)
    before = jnp.cumsum(cnt, axis=0) - cnt
    total = jnp.sum(cnt, axis=0)
    padded = ((total + EXPERT_TILE - 1) // EXPERT_TILE) * EXPERT_TILE
    ends = jnp.cumsum(padded)
    offs = (ends - padded)[None, :] + before
    experts = jnp.arange(N_EXPERTS, dtype=jnp.int32)
    pos = rank + jnp.sum(jnp.where(ids[..., None] == experts, offs[None, :, None, :], 0), axis=-1)
    n_rows = TOP_K * t + N_EXPERTS * EXPERT_TILE
    tile_start = jnp.arange(n_rows // EXPERT_TILE, dtype=jnp.int32) * EXPERT_TILE
    tile_expert = jnp.minimum(jnp.sum((tile_start[:, None] >= ends[None, :]).astype(jnp.int32), axis=1), N_EXPERTS - 1)
    tiles_used = ends[N_EXPERTS - 1:] // EXPERT_TILE
    return pos.reshape(TOP_K, t), jnp.concatenate([tile_expert, tiles_used]), n_rows


def _rope_tables(seq):
    half = DIFF_DQK // 2
    inv_freq = ROPE_THETA ** (-jnp.arange(half, dtype=F32) / half)
    ang = jnp.arange(seq, dtype=F32)[:, None] * inv_freq[None, :]
    cos = jnp.cos(ang)
    sin = jnp.sin(ang)
    reps = DIFF_DV // DIFF_DQK
    cos_t = jnp.tile(jnp.concatenate([cos, cos], axis=-1), (1, reps))
    sin_t = jnp.tile(jnp.concatenate([-sin, sin], axis=-1), (1, reps))
    return cos_t, sin_t


def _lane_row(vec, offset):
    return jnp.zeros((LANES,), F32).at[offset:offset + vec.shape[0]].set(vec.astype(F32))


def _layer_weights(l, p):
    w_in = p['w_in'][l]
    w_big = jnp.concatenate([w_in[:, 5648:9744], w_in[:, 0:2048], w_in[:, 2056:5128], w_in[:, 5136:5648]],
                            axis=1).astype(BF16)
    w_small = jnp.concatenate([w_in[:, 2048:2056], w_in[:, 5128:5136],
                               jnp.zeros((D_MODEL, LANES - 16), F32)], axis=1).astype(BF16)
    par = jnp.zeros((SUBLANES, LANES), F32)
    gdn_par = par.at[0].set(_lane_row(p['gdn_a_log'][l], GDN_HEADS)).at[1].set(_lane_row(p['gdn_dt_bias'][l], GDN_HEADS))
    ssm_par = par.at[0].set(_lane_row(p['ssm_a_log'][l], SSM_HEADS)).at[1].set(_lane_row(p['ssm_dt_bias'][l], SSM_HEADS))
    rw = jnp.concatenate([p['router_w'][l].T, jnp.zeros((LANES - N_EXPERTS, D_MODEL), F32)], axis=0)
    rwh = rw.astype(BF16)
    bgu = p['b_gate_up'][l]
    return dict(
        w_big=w_big, w_small=w_small, gdn_par=gdn_par, ssm_par=ssm_par,
        gdn_conv_w=p['gdn_conv_w'][l], gdn_norm_w=p['gdn_norm_w'][l][None, :],
        lambda_init=0.8 - 0.6 * math.exp(-0.3 * l), diff_lambda=p['diff_lambda'][l],
        diff_norm_w=p['diff_norm_w'][l][None, :],
        ssm_conv_w=p['ssm_conv_w'][l], ssm_conv_b=p['ssm_conv_b'][l][None, :],
        dskip=jnp.repeat(p['ssm_d'][l], SSM_HEADDIM)[None, :], ssm_norm_w=p['ssm_norm_w'][l][None, :],
        w_mem=p['w_mem'][l].astype(BF16), w_branch=p['w_branch'][l].astype(BF16), w_out=p['w_out'][l].astype(BF16),
        ln1_g=p['ln1_g'][l][None, :], ln1_b=p['ln1_b'][l][None, :],
        rwh=rwh, rwl=(rw - rwh.astype(F32)).astype(BF16), rb=_lane_row(p['router_b'][l], 0)[:, None],
        first_expert=l * bgu.shape[0],
        ln2_g=p['ln2_g'][l][None, :], ln2_b=p['ln2_b'][l][None, :])


def _mix_and_route(w, x, xbf, kv_src, bsz, seq, cos, sin):
    t = bsz * seq
    h, hs = _in_proj(xbf, w['w_big'], w['w_small'], min(1024, t), H_WIDTH // 4)
    y_gdn = _gdn(h, hs, w['gdn_conv_w'], w['gdn_par'], w['gdn_norm_w'], bsz, seq)
    y_diff = _diff_attention(h, cos, sin, w['diff_lambda'], w['diff_norm_w'], w['lambda_init'], bsz, seq)
    y_ssm = _ssd(h, hs, w['ssm_conv_w'], w['ssm_conv_b'], w['ssm_par'], w['dskip'], w['ssm_norm_w'], bsz, seq)
    kv = _matmul(kv_src, w['w_mem'], BF16, min(1024, kv_src.shape[0]), HB, "mem_kv")
    y_mem = _mem_attention(h, kv, bsz, seq, kv_src.shape[0] // bsz)
    x1, xa, xb, route, counts = _merge(h, (y_gdn, y_diff, y_ssm, y_mem), x, w['w_branch'], w['w_out'],
                                       w['ln1_g'], w['ln1_b'], w['rwh'], w['rwl'], w['rb'])
    pos, tile_expert, n_rows = _route_positions(route, counts)
    xsa, xsb = _sc_dispatch(xa, xb, pos, n_rows)
    cw = jnp.zeros((t, LANES), F32).at[:, :TOP_K].set(route[:TOP_K].T)
    return dict(x1=x1, route=cw, pos=pos, tile_expert=tile_expert, xsa=xsa, xsb=xsb)


def _experts(w, ew, st):
    ysa, ysb = _expert_mlp(st['tile_expert'], st['xsa'], st['xsb'], ew['w_gate_lin'], ew['b_gate_lin'], ew['w_down'],
                           ew['b_down'], w['first_expert'])
    t = st['x1'].shape[0]
    ga, gb = _sc_collect(ysa, ysb, st['pos'].reshape(1, TOP_K * t))
    return dict(st, ga=ga.reshape(TOP_K, t, SC_WORDS), gb=gb.reshape(TOP_K, t, SC_WORDS))


def _finish(w, st):
    return _combine(st['ga'], st['gb'], st['route'], st['x1'], w['ln2_g'], w['ln2_b'])


def kernel(x, mem, w_in, gdn_conv_w, gdn_a_log, gdn_dt_bias, gdn_norm_w, diff_lambda, diff_norm_w, ssm_conv_w, ssm_conv_b, ssm_a_log, ssm_dt_bias, ssm_d, ssm_norm_w, w_mem, w_branch, w_out, ln1_g, ln1_b, router_w, router_b, w_gate_up, b_gate_up, w_down, b_down, ln2_g, ln2_b):
    p = dict(w_in=w_in, gdn_conv_w=gdn_conv_w, gdn_a_log=gdn_a_log, gdn_dt_bias=gdn_dt_bias, gdn_norm_w=gdn_norm_w,
             diff_lambda=diff_lambda, diff_norm_w=diff_norm_w, ssm_conv_w=ssm_conv_w, ssm_conv_b=ssm_conv_b,
             ssm_a_log=ssm_a_log, ssm_dt_bias=ssm_dt_bias, ssm_d=ssm_d, ssm_norm_w=ssm_norm_w, w_mem=w_mem,
             w_branch=w_branch, w_out=w_out, ln1_g=ln1_g, ln1_b=ln1_b, router_w=router_w, router_b=router_b,
             w_gate_up=w_gate_up, b_gate_up=b_gate_up, w_down=w_down, b_down=b_down, ln2_g=ln2_g, ln2_b=ln2_b)
    bsz, seq, d = x.shape
    depth, n_exp = w_gate_up.shape[0], w_gate_up.shape[1]
    expert_w = dict(
        w_gate_lin=_deinterleave(w_gate_up.reshape(depth * n_exp * d, 2 * D_FF)).reshape(depth * n_exp, d, 2 * D_FF),
        b_gate_lin=b_gate_up.reshape(depth * n_exp, 2 * D_FF // DEINT_BLOCK, DEINT_BLOCK // 2, 2)
        .transpose(0, 1, 3, 2).reshape(depth * n_exp, 1, 2 * D_FF),
        w_down=w_down.astype(BF16).reshape(depth * n_exp, D_FF, d),
        b_down=b_down.reshape(depth * n_exp, 1, d))
    cos, sin = _rope_tables(seq)
    groups = TOKEN_GROUPS if bsz % TOKEN_GROUPS == 0 else 1
    gb = bsz // groups
    xs = [x[g * gb:(g + 1) * gb].reshape(gb * seq, d) for g in range(groups)]
    xbfs = [v.astype(BF16) for v in xs]
    kvs = [mem[g * gb:(g + 1) * gb].reshape(gb * mem.shape[1], d).astype(BF16) for g in range(groups)]
    for l in range(depth):
        w = _layer_weights(l, p)
        sts = [_mix_and_route(w, xs[g], xbfs[g], kvs[g], gb, seq, cos, sin) for g in range(groups)]
        sts = [_experts(w, expert_w, st) for st in sts]
        outs = [_finish(w, st) for st in sts]
        xs = [o[0] for o in outs]
        xbfs = [o[1] for o in outs]
    return jnp.concatenate([v.reshape(gb, seq, d) for v in xs], axis=0)
```

```python
import functools
import math

import jax
import jax.numpy as jnp
from jax import lax
from jax.experimental import pallas as pl
from jax.experimental.pallas import tpu as pltpu
from jax.experimental.pallas import tpu_sc as plsc

F32 = jnp.float32
BF16 = jnp.bfloat16
U32 = jnp.uint32

D_MODEL = 1024
DEPTH = 4
CHUNK = 64
Q_BLOCK = 128
CONV_WIDTH = 4
ROPE_THETA = 10000.0
NORM_EPS = 1e-6
LN_EPS = 1e-5

GDN_HEADS = 4
GDN_DK = 128
GDN_DV = 128
DIFF_HEADS = 4
DIFF_DQK = 64
DIFF_DV = 128
SSM_HEADS = 8
SSM_HEADDIM = 64
SSM_GROUPS = 2
SSM_STATE = 128
SSM_INNER = 512
MEM_HEADS = 4
MEM_HEADDIM = 128
N_BRANCH = 4
N_EXPERTS = 32
TOP_K = 4
D_FF = 1024
SWIGLU_LIMIT = 7.0
SWIGLU_ALPHA = 1.702
DEEPNORM_ALPHA = (2.0 * DEPTH) ** 0.25
LOG2_E = math.log2(math.e)

LANES = 128
SUBLANES = 8
HALO = 16
VMEM_LIMIT = 56 * 1024 * 1024

HB = 512
COL_GATES = 0
COL_GQ, COL_GK, COL_GV, COL_GZ = 8, 9, 10, 11
COL_DQ, COL_DK, COL_DV = 12, 13, 14
COL_SZ = 15
COL_SXBC = 8
COL_MQ = 18
H_WIDTH = 19 * HB

SEQ_TILE = 512
ROW_TILE = 512
EXPERT_TILE = 512
SC_WINDOW = 128
SC_WORDS = 256
ROUTE_ROWS = 16
DEINT_BLOCK = 2 * LANES
GDN_TILE = 512
DIFF_ROW_SPLIT = 2
TOKEN_GROUPS = 2


def _cparams(*sem):
    return pltpu.CompilerParams(dimension_semantics=sem, vmem_limit_bytes=VMEM_LIMIT)


def _dot(a, b):
    return jnp.dot(a.astype(BF16), b.astype(BF16), preferred_element_type=F32)


def _dot_nt(a, b):
    return lax.dot_general(a.astype(BF16), b.astype(BF16), (((1,), (1,)), ((), ())), preferred_element_type=F32)


def _bdot(a, b):
    return jnp.einsum('cmk,ckn->cmn', a.astype(BF16), b.astype(BF16), preferred_element_type=F32)


def _bdot_nt(a, b):
    return jnp.einsum('cmk,cnk->cmn', a.astype(BF16), b.astype(BF16), preferred_element_type=F32)


def _split2(a):
    hi = a.astype(BF16)
    lo = (a - hi.astype(F32)).astype(BF16)
    return hi, lo


def _sigmoid(x):
    return 0.5 * jnp.tanh(0.5 * x) + 0.5


def _silu(x):
    h = 0.5 * x
    return h + h * jnp.tanh(h)


def _softplus(x):
    return jnp.maximum(x, 0.0) + jnp.log1p(jnp.exp(-jnp.abs(x)))


def _rms(x, w):
    return x * lax.rsqrt(jnp.mean(x * x, axis=-1, keepdims=True) + NORM_EPS) * w


def _layer_norm(x, g, b):
    mu = jnp.mean(x, axis=-1, keepdims=True)
    xc = x - mu
    var = jnp.mean(xc * xc, axis=-1, keepdims=True)
    return xc * lax.rsqrt(var + LN_EPS) * g + b


def _mm_body(x_ref, w_ref, o_ref):
    o_ref[...] = jnp.dot(x_ref[...], w_ref[...], preferred_element_type=F32).astype(o_ref.dtype)


def _matmul(x, w, out_dtype, tm, tn, name):
    m, k = x.shape
    n = w.shape[1]
    return pl.pallas_call(
        _mm_body,
        grid=(m // tm, n // tn),
        in_specs=[pl.BlockSpec((tm, k), lambda i, j: (i, 0)), pl.BlockSpec((k, tn), lambda i, j: (0, j))],
        out_specs=pl.BlockSpec((tm, tn), lambda i, j: (i, j)),
        out_shape=jax.ShapeDtypeStruct((m, n), out_dtype),
        compiler_params=_cparams("parallel", "parallel"),
        name=name,
    )(x, w)


def _in_proj_body(x_ref, w_ref, ws_ref, o_ref, os_ref):
    x = x_ref[...]
    o_ref[...] = jnp.dot(x, w_ref[...], preferred_element_type=F32).astype(o_ref.dtype)

    @pl.when(pl.program_id(1) == 0)
    def _():
        os_ref[...] = jnp.dot(x, ws_ref[...], preferred_element_type=F32)


def _in_proj(x, w, w_small, tm, tn):
    m, k = x.shape
    n = w.shape[1]
    return pl.pallas_call(
        _in_proj_body,
        grid=(m // tm, n // tn),
        in_specs=[pl.BlockSpec((tm, k), lambda i, j: (i, 0)), pl.BlockSpec((k, tn), lambda i, j: (0, j)),
                  pl.BlockSpec((k, LANES), lambda i, j: (0, 0))],
        out_specs=[pl.BlockSpec((tm, tn), lambda i, j: (i, j)), pl.BlockSpec((tm, LANES), lambda i, j: (i, 0))],
        out_shape=[jax.ShapeDtypeStruct((m, n), BF16), jax.ShapeDtypeStruct((m, LANES), F32)],
        compiler_params=_cparams("parallel", "arbitrary"),
        name="in_proj",
    )(x, w, w_small)


def _shift_taps(nc):
    m = lax.broadcasted_iota(jnp.int32, (1, (CONV_WIDTH - 1) * CHUNK, CHUNK + HALO), 1)
    k = lax.broadcasted_iota(jnp.int32, (1, (CONV_WIDTH - 1) * CHUNK, CHUNK + HALO), 2)
    src = (m % CHUNK) + HALO - (CONV_WIDTH - 1 - m // CHUNK)
    return jnp.broadcast_to(_mask(k == src).astype(BF16), (nc, (CONV_WIDTH - 1) * CHUNK, CHUNK + HALO))


def _conv_silu(x, tail_ref, slot, xbuf_ref, w, bias, taps):
    rows, width = x.shape
    nc = rows // CHUNK
    xbuf_ref[0:HALO, :] = tail_ref[slot]
    xbuf_ref[HALO:HALO + rows, :] = x
    tail_ref[slot] = x[rows - HALO:rows, :]
    xext = jnp.concatenate([xbuf_ref[i * CHUNK:(i + 1) * CHUNK + HALO, :][None] for i in range(nc)], axis=0)
    shifted = _bdot(taps, xext)
    y = x.astype(F32).reshape(nc, CHUNK, width) * w[CONV_WIDTH - 1:CONV_WIDTH, :]
    for j in range(CONV_WIDTH - 1):
        y = y + shifted[:, j * CHUNK:(j + 1) * CHUNK, :] * w[j:j + 1, :]
    y = y.reshape(rows, width)
    if bias is not None:
        y = y + bias
    return _silu(y)


def _chunk_iota():
    r = lax.broadcasted_iota(jnp.int32, (1, CHUNK, CHUNK), 1)
    c = lax.broadcasted_iota(jnp.int32, (1, CHUNK, CHUNK), 2)
    return r, c


def _mask(cond):
    return jnp.where(cond, 1.0, 0.0)


def _segment_logdecay(g3, r, c):
    nb = g3.shape[0]
    lincl = jnp.broadcast_to(_mask(r >= c).astype(BF16), (nb, CHUNK, CHUNK))
    upper = _mask(r > c)
    ghi = g3.astype(BF16).astype(F32)
    glo = g3 - ghi
    return _bdot(lincl, ghi * upper) + _bdot(lincl, glo * upper)


def _inv_unit_lower(a, r, c):
    d = _mask(r == c) - a * _mask(((r >> 1) == (c >> 1)) & (r > c))
    for lb in range(1, 6):
        lower_left = _mask(((r >> (lb + 1)) == (c >> (lb + 1))) & ((r >> lb) > (c >> lb)))
        d = d - _bdot(_bdot(d, a * lower_left), d)
    return d


def _gdn_body(q_ref, k_ref, v_ref, z_ref, hs_ref, cw_ref, par_ref, nw_ref, o_ref,
              tail_ref, xbuf_ref, qs_ref, ks_ref, vs_ref, pt_ref, bt_ref, qwt_ref, qut_ref, ot_ref, gl_ref,
              state_ref):
    ts = q_ref.shape[0]
    nc = ts // CHUNK

    @pl.when(pl.program_id(1) == 0)
    def _():
        tail_ref[...] = jnp.zeros_like(tail_ref)
        state_ref[...] = jnp.zeros_like(state_ref)

    taps = _shift_taps(nc)
    for slot, (src, dst) in enumerate(((q_ref, qs_ref), (k_ref, ks_ref), (v_ref, vs_ref))):
        w = cw_ref[:, slot * HB:(slot + 1) * HB]
        dst[...] = _conv_silu(src[...], tail_ref, slot, xbuf_ref, w, None, taps)

    hs = hs_ref[...]
    beta_all = _sigmoid(hs)
    g_all = -jnp.exp(par_ref[0:1, :]) * _softplus(hs + par_ref[1:2, :])

    def stack(per_head):
        return jnp.concatenate([per_head(h) for h in range(GDN_HEADS)], axis=0)

    def l2(ref, h):
        x = ref[:, h * GDN_DK:(h + 1) * GDN_DK]
        return (x * lax.rsqrt(jnp.sum(x * x, axis=-1, keepdims=True) + NORM_EPS)).reshape(nc, CHUNK, GDN_DK)

    q3 = stack(lambda h: l2(qs_ref, h)) * (GDN_DK ** -0.5)
    k3 = stack(lambda h: l2(ks_ref, h))
    v3 = stack(lambda h: vs_ref[:, h * GDN_DV:(h + 1) * GDN_DV].reshape(nc, CHUNK, GDN_DV))
    b3 = stack(lambda h: beta_all[:, h:h + 1].reshape(nc, CHUNK, 1))
    g3 = stack(lambda h: g_all[:, GDN_HEADS + h:GDN_HEADS + h + 1].reshape(nc, CHUNK, 1))
    r, c = _chunk_iota()
    m = _segment_logdecay(g3, r, c)
    em = jnp.exp(m)
    a = _bdot_nt(k3, k3) * b3 * (em * _mask(r > c))
    t = _inv_unit_lower(a, r, c)
    gc = m[:, :, 0:1] + g3[:, 0:1, :]
    eg = jnp.exp(gc)
    g_last = gc[:, CHUNK - 1:CHUNK, :]
    sol = _bdot(t, jnp.concatenate([v3 * b3, k3 * (b3 * eg)], axis=-1))
    u = sol[:, :, :GDN_DV]
    w = sol[:, :, GDN_DV:]
    kd = k3 * jnp.exp(g_last - gc)
    qk = _bdot_nt(q3, k3) * (em * _mask(r >= c))
    pt = _bdot(jnp.swapaxes(w, 1, 2), kd).astype(BF16)
    bt = _bdot(jnp.swapaxes(u, 1, 2), kd)
    qwt = jnp.swapaxes(q3 * eg - _bdot(qk, w), 1, 2).astype(BF16)
    qut = jnp.swapaxes(_bdot(qk, u), 1, 2)
    gl = jnp.broadcast_to(jnp.exp(g_last), (GDN_HEADS * nc, SUBLANES, LANES))
    for h in range(GDN_HEADS):
        hb = slice(h * nc, (h + 1) * nc)
        pt_ref[h] = pt[hb]
        bt_ref[h] = bt[hb]
        qwt_ref[h] = qwt[hb]
        qut_ref[h] = qut[hb]
        gl_ref[h] = gl[hb]

    def chunk_step(ci, carry):
        for h in range(GDN_HEADS):
            st = state_ref[h]
            stb = st.astype(BF16)
            ot_ref[h, ci] = jnp.dot(stb, qwt_ref[h, ci], preferred_element_type=F32) + qut_ref[h, ci]
            state_ref[h] = (st * gl_ref[h, ci][0:1, :] - jnp.dot(stb, pt_ref[h, ci], preferred_element_type=F32)
                            + bt_ref[h, ci])
        return carry

    lax.fori_loop(0, nc, chunk_step, 0)

    for h in range(GDN_HEADS):
        sl = slice(h * GDN_DV, (h + 1) * GDN_DV)
        o = jnp.swapaxes(ot_ref[h], 1, 2).reshape(ts, GDN_DV)
        o_ref[:, sl] = (_rms(o, nw_ref[...]) * _silu(z_ref[:, sl].astype(F32))).astype(o_ref.dtype)


def _gdn(h, hs, conv_w, par, norm_w, bsz, seq):
    ts = min(GDN_TILE, seq)
    ns = seq // ts
    tok = lambda col: pl.BlockSpec((ts, HB), lambda b, s: (b * ns + s, col))
    full = lambda shape: pl.BlockSpec(shape, lambda b, s: (0,) * len(shape))
    nc = ts // CHUNK
    return pl.pallas_call(
        _gdn_body,
        grid=(bsz, ns),
        in_specs=[tok(COL_GQ), tok(COL_GK), tok(COL_GV), tok(COL_GZ),
                  pl.BlockSpec((ts, LANES), lambda b, s: (b * ns + s, 0)),
                  full((CONV_WIDTH, 3 * HB)), full((SUBLANES, LANES)), full((1, GDN_DV))],
        out_specs=pl.BlockSpec((ts, HB), lambda b, s: (b * ns + s, 0)),
        out_shape=jax.ShapeDtypeStruct((bsz * seq, HB), BF16),
        scratch_shapes=[
            pltpu.VMEM((3, HALO, HB), BF16),
            pltpu.VMEM((ts + HALO, HB), BF16),
            pltpu.VMEM((ts, HB), F32), pltpu.VMEM((ts, HB), F32), pltpu.VMEM((ts, HB), F32),
            pltpu.VMEM((GDN_HEADS, nc, GDN_DK, GDN_DK), BF16), pltpu.VMEM((GDN_HEADS, nc, GDN_DV, GDN_DK), F32),
            pltpu.VMEM((GDN_HEADS, nc, GDN_DK, CHUNK), BF16), pltpu.VMEM((GDN_HEADS, nc, GDN_DV, CHUNK), F32),
            pltpu.VMEM((GDN_HEADS, nc, GDN_DV, CHUNK), F32),
            pltpu.VMEM((GDN_HEADS, nc, SUBLANES, LANES), F32),
            pltpu.VMEM((GDN_HEADS, GDN_DV, GDN_DK), F32),
        ],
        compiler_params=_cparams("parallel", "arbitrary"),
        name="gdn",
    )(h, h, h, h, hs, conv_w, par, norm_w)


def _expand_heads(x, expand_hi):
    hi = x.astype(BF16)
    r1 = x - hi.astype(F32)
    mid = r1.astype(BF16)
    lo = (r1 - mid.astype(F32)).astype(BF16)
    e = expand_hi
    return (jnp.dot(hi, e, preferred_element_type=F32) + jnp.dot(mid, e, preferred_element_type=F32)
            + jnp.dot(lo, e, preferred_element_type=F32))


def _ssd_body(z_ref, xbc_ref, hs_ref, cw_ref, cb_ref, par_ref, dskip_ref, nw_ref, o_ref,
              tail_ref, xbuf_ref, xs_ref, bc_ref, yd_ref, hin_ref, state_ref):
    ts = z_ref.shape[0]
    nc = ts // CHUNK
    gw = SSM_INNER // SSM_GROUPS

    @pl.when(pl.program_id(1) == 0)
    def _():
        tail_ref[...] = jnp.zeros_like(tail_ref)
        state_ref[...] = jnp.zeros_like(state_ref)

    taps = _shift_taps(nc)
    for slot, dst in enumerate((xs_ref, bc_ref)):
        cols = slice(slot * HB, (slot + 1) * HB)
        dst[...] = _conv_silu(xbc_ref[:, cols], tail_ref, slot, xbuf_ref, cw_ref[:, cols], cb_ref[:, cols], taps)

    hs = hs_ref[...]
    dt = _softplus(hs + par_ref[1:2, :])
    a_all = -jnp.exp(par_ref[0:1, :]) * dt
    r, c = _chunk_iota()
    tril = _mask(r >= c)
    lincl = jnp.broadcast_to(tril.astype(BF16), (nc, CHUNK, CHUNK))
    a3 = a_all.reshape(nc, CHUNK, LANES)
    ahi = a3.astype(BF16)
    ar = a3 - ahi.astype(F32)
    amid = ar.astype(BF16)
    alo = (ar - amid.astype(F32)).astype(BF16)
    acum = _bdot(lincl, ahi) + _bdot(lincl, amid) + _bdot(lincl, alo)
    lane = lax.broadcasted_iota(jnp.int32, (LANES, SSM_INNER), 0)
    chan = lax.broadcasted_iota(jnp.int32, (LANES, SSM_INNER), 1)
    expand = jnp.where(lane - SSM_HEADS == chan // SSM_HEADDIM, 1.0, 0.0).astype(BF16)
    dt_x = _expand_heads(dt, expand)
    acum_x = _expand_heads(acum.reshape(ts, LANES), expand).reshape(nc, CHUNK, SSM_INNER)
    a_last_x = acum_x[:, CHUNK - 1:CHUNK, :]
    ea_x = jnp.exp(acum_x).reshape(ts, SSM_INNER)
    din_x = jnp.exp(a_last_x - acum_x).reshape(ts, SSM_INNER)
    cd_x = jnp.exp(a_last_x)

    xs = xs_ref[...]
    xdt = xs * dt_x
    xw = xdt * din_x
    for g in range(SSM_GROUPS):
        gcols = slice(g * gw, (g + 1) * gw)
        b3 = bc_ref[:, g * SSM_STATE:(g + 1) * SSM_STATE].reshape(nc, CHUNK, SSM_STATE)
        c3 = bc_ref[:, gw + g * SSM_STATE:gw + (g + 1) * SSM_STATE].reshape(nc, CHUNK, SSM_STATE)
        cb = _bdot_nt(c3, b3)
        hpg = SSM_HEADS // SSM_GROUPS
        heads = [g * hpg + hr for hr in range(hpg)]
        ah3 = jnp.concatenate([a_all[:, SSM_HEADS + hd:SSM_HEADS + hd + 1].reshape(nc, CHUNK, 1) for hd in heads],
                              axis=0)
        x4 = jnp.concatenate([xdt[:, hd * SSM_HEADDIM:(hd + 1) * SSM_HEADDIM].reshape(nc, CHUNK, SSM_HEADDIM)
                              for hd in heads], axis=0)
        seg = jnp.exp(_segment_logdecay(ah3, r, c)) * tril
        yd = _bdot(seg * jnp.concatenate([cb] * hpg, axis=0), x4)
        for hr, hd in enumerate(heads):
            yd_ref[:, hd * SSM_HEADDIM:(hd + 1) * SSM_HEADDIM] = yd[hr * nc:(hr + 1) * nc].reshape(ts, SSM_HEADDIM)
        bt = jnp.swapaxes(b3, 1, 2)
        st = _bdot(bt, xw[:, gcols].reshape(nc, CHUNK, gw))
        hcur = state_ref[g]
        for ci in range(nc):
            hin_ref[ci] = hcur
            hcur = hcur * cd_x[ci, 0:1, gcols] + st[ci]
        state_ref[g] = hcur
        y_off = _bdot(c3, hin_ref[...]).reshape(ts, gw) * ea_x[:, gcols]
        y = yd_ref[:, gcols] + y_off + dskip_ref[:, gcols] * xs[:, gcols]
        gated = y * _silu(z_ref[:, gcols].astype(F32))
        o_ref[:, gcols] = _rms(gated, nw_ref[:, gcols]).astype(o_ref.dtype)


def _ssd(h, hs, conv_w, conv_b, par, dskip, norm_w, bsz, seq):
    ts = min(SEQ_TILE, seq)
    ns = seq // ts
    nc = ts // CHUNK
    gw = SSM_INNER // SSM_GROUPS
    full = lambda shape: pl.BlockSpec(shape, lambda b, s: (0,) * len(shape))
    return pl.pallas_call(
        _ssd_body,
        grid=(bsz, ns),
        in_specs=[pl.BlockSpec((ts, HB), lambda b, s: (b * ns + s, COL_SZ)),
                  pl.BlockSpec((ts, 2 * HB), lambda b, s: (b * ns + s, COL_SXBC)),
                  pl.BlockSpec((ts, LANES), lambda b, s: (b * ns + s, 0)),
                  full((CONV_WIDTH, 2 * HB)), full((1, 2 * HB)), full((SUBLANES, LANES)),
                  full((1, SSM_INNER)), full((1, SSM_INNER))],
        out_specs=pl.BlockSpec((ts, HB), lambda b, s: (b * ns + s, 0)),
        out_shape=jax.ShapeDtypeStruct((bsz * seq, HB), BF16),
        scratch_shapes=[
            pltpu.VMEM((2, HALO, HB), BF16),
            pltpu.VMEM((ts + HALO, HB), BF16),
            pltpu.VMEM((ts, HB), F32), pltpu.VMEM((ts, HB), F32), pltpu.VMEM((ts, HB), F32),
            pltpu.VMEM((nc, SSM_STATE, gw), F32),
            pltpu.VMEM((SSM_GROUPS, SSM_STATE, gw), F32),
        ],
        compiler_params=_cparams("parallel", "arbitrary"),
        name="ssd",
    )(h, h, hs, conv_w, conv_b, par, dskip, norm_w)


def _rope(x, cos, sin_signed):
    width = x.shape[-1]
    half = DIFF_DQK // 2
    lane = lax.broadcasted_iota(jnp.int32, x.shape, x.ndim - 1)
    fwd = pltpu.roll(x, width - half, axis=x.ndim - 1)
    bwd = pltpu.roll(x, half, axis=x.ndim - 1)
    partner = jnp.where((lane & half) == 0, fwd, bwd)
    return x * cos + partner * sin_signed


def _diff_body(q_ref, k_ref, v_ref, cos_ref, sin_ref, lam_ref, nw_ref, o_ref, kr_ref, *, lambda_init, nq):
    qi = pl.program_id(2)
    tq = q_ref.shape[0]
    q0 = pl.multiple_of(qi * tq, tq)
    cos = cos_ref[pl.ds(q0, tq), :]
    sin = sin_ref[pl.ds(q0, tq), :]
    kr_ref[pl.ds(q0, tq), :] = _rope(k_ref[...].astype(F32), cos, sin).astype(BF16)
    qr = _rope(q_ref[...].astype(F32), cos, sin) * (DIFF_DQK ** -0.5 * LOG2_E)

    lp = lam_ref[...]
    prod01 = jnp.sum(lp[0:1, :] * lp[1:2, :], axis=-1, keepdims=True)
    prod23 = jnp.sum(lp[2:3, :] * lp[3:4, :], axis=-1, keepdims=True)
    lam = jnp.exp(prod01) - jnp.exp(prod23) + lambda_init

    lane = lax.broadcasted_iota(jnp.int32, (tq, 2 * DIFF_DQK), 1)
    qmaps = (jnp.where(lane < DIFF_DQK, qr, 0.0).astype(BF16), jnp.where(lane >= DIFF_DQK, qr, 0.0).astype(BF16))
    th = tq // DIFF_ROW_SPLIT

    for i in range(nq):

        @pl.when(qi == i)
        def _(i=i):
            lo = i * tq
            for rh in range(DIFF_ROW_SPLIT):
                rows = slice(rh * th, (rh + 1) * th)
                own = (rh + 1) * th
                row = lax.broadcasted_iota(jnp.int32, (th, own), 0) + rh * th
                col = lax.broadcasted_iota(jnp.int32, (th, own), 1)
                diag_ok = (row // CHUNK) >= (col // CHUNK)
                outs = []
                for qmap in qmaps:
                    qm = qmap[rows, :]
                    s_diag = jnp.where(diag_ok, _dot_nt(qm, kr_ref[lo:lo + own, :]), -jnp.inf)
                    mx = jnp.max(s_diag, axis=-1, keepdims=True)
                    if i > 0:
                        s_off = _dot_nt(qm, kr_ref[0:lo, :])
                        mx = jnp.maximum(mx, jnp.max(s_off, axis=-1, keepdims=True))
                    p_diag = jnp.exp2(s_diag - mx)
                    denom = jnp.sum(p_diag, axis=-1, keepdims=True)
                    acc = _dot(p_diag, v_ref[lo:lo + own, :])
                    if i > 0:
                        p_off = jnp.exp2(s_off - mx)
                        denom = denom + jnp.sum(p_off, axis=-1, keepdims=True)
                        acc = acc + _dot(p_off, v_ref[0:lo, :])
                    outs.append(acc / denom)
                o = outs[0] - lam * outs[1]
                o_ref[rows, :] = (_rms(o, nw_ref[...]) * (1.0 - lambda_init)).astype(o_ref.dtype)


def _diff_attention(h, cos, sin, lam_params, norm_w, lambda_init, bsz, seq):
    tq = min(2 * SEQ_TILE, seq)
    nq = seq // tq
    per_head = HB // DIFF_DV
    full = lambda shape: pl.BlockSpec(shape, lambda b, hd, q: (0,) * len(shape))
    tok = lambda col: pl.BlockSpec((tq, DIFF_DV), lambda b, hd, q: (b * nq + q, col * per_head + hd))
    return pl.pallas_call(
        functools.partial(_diff_body, lambda_init=lambda_init, nq=nq),
        grid=(bsz, DIFF_HEADS, nq),
        in_specs=[tok(COL_DQ), tok(COL_DK),
                  pl.BlockSpec((seq, DIFF_DV), lambda b, hd, q: (b, COL_DV * per_head + hd)),
                  full((seq, DIFF_DV)), full((seq, DIFF_DV)),
                  full((4, DIFF_DQK)), full((1, DIFF_DV))],
        out_specs=pl.BlockSpec((tq, DIFF_DV), lambda b, hd, q: (b * nq + q, hd)),
        out_shape=jax.ShapeDtypeStruct((bsz * seq, HB), BF16),
        scratch_shapes=[pltpu.VMEM((seq, DIFF_DV), BF16)],
        compiler_params=_cparams("parallel", "parallel", "arbitrary"),
        name="diff_attn",
    )(h, h, h, cos, sin, lam_params, norm_w)


def _mem_body(q_ref, kv_ref, o_ref):
    width = MEM_HEADS * MEM_HEADDIM
    for h in range(MEM_HEADS):
        cols = slice(h * MEM_HEADDIM, (h + 1) * MEM_HEADDIM)
        q = q_ref[:, cols].astype(F32) * (MEM_HEADDIM ** -0.5)
        s = _dot_nt(q, kv_ref[:, cols])
        m = jnp.max(s, axis=-1, keepdims=True)
        p = jnp.exp(s - m)
        p = p / jnp.sum(p, axis=-1, keepdims=True)
        o_ref[:, cols] = _dot(p, kv_ref[:, width + h * MEM_HEADDIM:width + (h + 1) * MEM_HEADDIM]).astype(o_ref.dtype)


def _mem_attention(h, kv, bsz, seq, mem_len):
    tm = min(ROW_TILE, seq)
    ns = seq // tm
    return pl.pallas_call(
        _mem_body,
        grid=(bsz, ns),
        in_specs=[pl.BlockSpec((tm, HB), lambda b, s: (b * ns + s, COL_MQ)),
                  pl.BlockSpec((mem_len, 2 * HB), lambda b, s: (b, 0))],
        out_specs=pl.BlockSpec((tm, HB), lambda b, s: (b * ns + s, 0)),
        out_shape=jax.ShapeDtypeStruct((bsz * seq, HB), BF16),
        compiler_params=_cparams("parallel", "parallel"),
        name="mem_attn",
    )(h, kv)


def _pack_halves(x):
    n = x.shape[1] // 2
    lo = pltpu.bitcast(x[:, :n].astype(BF16).astype(F32), U32)
    hi = pltpu.bitcast(x[:, n:].astype(BF16).astype(F32), U32)
    return (lo >> 16) | hi


def _unpack_halves(w):
    lo = pltpu.bitcast(w << 16, F32)
    hi = pltpu.bitcast(w & jnp.uint32(0xFFFF0000), F32)
    return lo.astype(BF16), hi.astype(BF16)


def _merge_body(g_ref, y0_ref, y1_ref, y2_ref, y3_ref, x_ref, wb_ref, wo_ref, lng_ref, lnb_ref,
                rwh_ref, rwl_ref, rb_ref, xo_ref, pa_ref, pb_ref, route_ref, cnt_ref):
    tm = x_ref.shape[0]
    merged = jnp.zeros((tm, D_MODEL), F32)
    for i, y_ref in enumerate((y0_ref, y1_ref, y2_ref, y3_ref)):
        gate = _sigmoid(g_ref[:, i * D_MODEL:(i + 1) * D_MODEL].astype(F32))
        merged = merged + gate * jnp.dot(y_ref[...], wb_ref[i], preferred_element_type=F32)
    mix = _dot(merged, wo_ref[...])
    xn = _layer_norm(DEEPNORM_ALPHA * x_ref[...] + mix, lng_ref[...], lnb_ref[...])
    xo_ref[...] = xn
    pa_ref[...] = _pack_halves(xn[:, :D_MODEL // 2])
    pb_ref[...] = _pack_halves(xn[:, D_MODEL // 2:])

    xh, xl = _split2(xn)
    logits = (_dot_nt(rwh_ref[...], xh) + (_dot_nt(rwl_ref[...], xh) + _dot_nt(rwh_ref[...], xl)))[:N_EXPERTS, :]
    work = logits + rb_ref[0:N_EXPERTS, 0:1]
    expert = lax.broadcasted_iota(jnp.int32, (N_EXPERTS, tm), 0)
    vals, ids, sels = [], [], []
    for _ in range(TOP_K):
        mx = jnp.max(work, axis=0, keepdims=True)
        idx = jnp.min(jnp.where(work == mx, expert, N_EXPERTS), axis=0, keepdims=True)
        sel = expert == idx
        vals.append(mx)
        ids.append(idx)
        sels.append(sel)
        work = jnp.where(sel, -jnp.inf, work)
    exps = [jnp.exp(v - vals[0]) for v in vals]
    denom = exps[0] + exps[1] + exps[2] + exps[3]
    chosen = jnp.where(sels[0] | sels[1] | sels[2] | sels[3], 1.0, 0.0)
    r = lax.broadcasted_iota(jnp.int32, (tm, tm), 0)
    c = lax.broadcasted_iota(jnp.int32, (tm, tm), 1)
    utri = jnp.where(r <= c, 1.0, 0.0).astype(BF16)
    cum = jnp.dot(chosen.astype(BF16), utri, preferred_element_type=F32)
    ranks = [jnp.sum(jnp.where(sels[k], cum, 0.0), axis=0, keepdims=True) - 1.0 for k in range(TOP_K)]
    rows = [e / denom for e in exps] + [i.astype(F32) for i in ids] + ranks
    rows.append(jnp.zeros((ROUTE_ROWS - len(rows), tm), F32))
    route_ref[...] = jnp.concatenate(rows, axis=0)
    cnt_ref[0] = jnp.broadcast_to(cum[:, tm - 1:tm], (N_EXPERTS, LANES))


def _merge(h, ys, x, wb, wo, lng, lnb, rwh, rwl, rb):
    t = x.shape[0]
    tm = min(ROW_TILE, t)
    nt = t // tm
    tok = lambda width: pl.BlockSpec((tm, width), lambda i: (i, 0))
    full = lambda shape: pl.BlockSpec(shape, lambda i: (0,) * len(shape))
    return pl.pallas_call(
        _merge_body,
        grid=(nt,),
        in_specs=[tok(N_BRANCH * D_MODEL), tok(HB), tok(HB), tok(HB), tok(HB), tok(D_MODEL),
                  full((N_BRANCH, HB, D_MODEL)), full((D_MODEL, D_MODEL)), full((1, D_MODEL)), full((1, D_MODEL)),
                  full((LANES, D_MODEL)), full((LANES, D_MODEL)), full((LANES, 1))],
        out_specs=[tok(D_MODEL), tok(SC_WORDS), tok(SC_WORDS), pl.BlockSpec((ROUTE_ROWS, tm), lambda i: (0, i)),
                   pl.BlockSpec((1, N_EXPERTS, LANES), lambda i: (i, 0, 0))],
        out_shape=[jax.ShapeDtypeStruct((t, D_MODEL), F32), jax.ShapeDtypeStruct((t, SC_WORDS), U32),
                   jax.ShapeDtypeStruct((t, SC_WORDS), U32), jax.ShapeDtypeStruct((ROUTE_ROWS, t), F32),
                   jax.ShapeDtypeStruct((nt, N_EXPERTS, LANES), F32)],
        compiler_params=_cparams("parallel"),
        name="merge_ln_router",
    )(h, *ys, x, wb, wo, lng, lnb, rwh, rwl, rb)


def _sc_mesh():
    return plsc.VectorSubcoreMesh(core_axis_name="core", subcore_axis_name="subcore")


def _sc_dispatch(xa, xb, pos, n_out):
    t, words = xa.shape
    k = pos.shape[0]
    out = jax.ShapeDtypeStruct((n_out, words), xa.dtype)

    @functools.partial(pl.kernel, out_type=(out, out), mesh=_sc_mesh())
    def scatter_rows(xa_hbm, xb_hbm, i_hbm, oa_hbm, ob_hbm):
        for x_hbm, o_hbm in ((xa_hbm, oa_hbm), (xb_hbm, ob_hbm)):
            def body(x_vmem, i_vmem, o_hbm=o_hbm):
                for kk in range(k):
                    pltpu.sync_copy(x_vmem, o_hbm.at[i_vmem.at[kk]])

            pltpu.emit_pipeline(
                body, grid=(t // SC_WINDOW,),
                in_specs=[pl.BlockSpec((SC_WINDOW, words), lambda i: (i, 0)),
                          pl.BlockSpec((k, SC_WINDOW), lambda i: (0, i))],
                out_specs=[],
                core_axis_name=("core", "subcore"),
                dimension_semantics=(pltpu.PARALLEL,),
            )(x_hbm, i_hbm)

    return scatter_rows(xa, xb, pos)


def _sc_collect(ya, yb, pos_flat):
    n = pos_flat.shape[1]
    words = ya.shape[1]
    out = jax.ShapeDtypeStruct((n, words), ya.dtype)

    @functools.partial(pl.kernel, out_type=(out, out), mesh=_sc_mesh())
    def gather_rows(ya_hbm, yb_hbm, i_hbm, oa_hbm, ob_hbm):
        for y_hbm, o_hbm in ((ya_hbm, oa_hbm), (yb_hbm, ob_hbm)):
            def body(i_vmem, o_vmem, y_hbm=y_hbm):
                pltpu.sync_copy(y_hbm.at[i_vmem.at[0]], o_vmem)

            pltpu.emit_pipeline(
                body, grid=(n // SC_WINDOW,),
                in_specs=[pl.BlockSpec((1, SC_WINDOW), lambda i: (0, i))],
                out_specs=[pl.BlockSpec((SC_WINDOW, words), lambda i: (i, 0))],
                core_axis_name=("core", "subcore"),
                dimension_semantics=(pltpu.PARALLEL,),
            )(i_hbm, o_hbm)

    return gather_rows(ya, yb, pos_flat)


def _expert_body(te_ref, xa_ref, xb_ref, wgl_ref, bgl_ref, wd_ref, bd_ref, ya_ref, yb_ref):
    pl.when(pl.program_id(0) < te_ref[pl.num_programs(0)])(
        functools.partial(_expert_tile, xa_ref, xb_ref, wgl_ref, bgl_ref, wd_ref, bd_ref, ya_ref, yb_ref))


def _expert_tile(xa_ref, xb_ref, wgl_ref, bgl_ref, wd_ref, bd_ref, ya_ref, yb_ref):
    x0, x1 = _unpack_halves(xa_ref[...])
    x2, x3 = _unpack_halves(xb_ref[...])
    x = jnp.concatenate([x0, x1, x2, x3], axis=1)
    gl = jnp.dot(x, wgl_ref[0], preferred_element_type=F32) + bgl_ref[0]
    nblk = 2 * D_FF // DEINT_BLOCK
    half = DEINT_BLOCK // 2
    glu = jnp.concatenate([gl[:, b * DEINT_BLOCK:b * DEINT_BLOCK + half] for b in range(nblk)], axis=1)
    lin = jnp.concatenate([gl[:, b * DEINT_BLOCK + half:(b + 1) * DEINT_BLOCK] for b in range(nblk)], axis=1)
    glu = jnp.minimum(glu, SWIGLU_LIMIT)
    lin = jnp.clip(lin, -SWIGLU_LIMIT, SWIGLU_LIMIT)
    act = (lin + 1.0) * glu * (0.5 * jnp.tanh((0.5 * SWIGLU_ALPHA) * glu) + 0.5)
    y = _dot(act, wd_ref[0]) + bd_ref[0]
    ya_ref[...] = _pack_halves(y[:, :D_MODEL // 2])
    yb_ref[...] = _pack_halves(y[:, D_MODEL // 2:])


def _expert_mlp(tile_expert, xa, xb, wgl, bgl, wd, bd, first):
    n = xa.shape[0]
    tr = EXPERT_TILE
    tok = pl.BlockSpec((tr, SC_WORDS), lambda i, te: (i, 0))
    per_expert = lambda shape: pl.BlockSpec((1,) + shape, lambda i, te: (te[i] + first,) + (0,) * len(shape))
    out = jax.ShapeDtypeStruct((n, SC_WORDS), U32)
    return pl.pallas_call(
        _expert_body,
        grid_spec=pltpu.PrefetchScalarGridSpec(
            num_scalar_prefetch=1,
            grid=(n // tr,),
            in_specs=[tok, tok, per_expert((D_MODEL, 2 * D_FF)), per_expert((1, 2 * D_FF)),
                      per_expert((D_FF, D_MODEL)), per_expert((1, D_MODEL))],
            out_specs=[tok, tok],
        ),
        out_shape=[out, out],
        compiler_params=_cparams("arbitrary"),
        name="expert_mlp",
    )(tile_expert, xa, xb, wgl, bgl, wd, bd)


def _deinterleave_body(w_ref, p_ref, o_ref):
    for b in range(w_ref.shape[1] // DEINT_BLOCK):
        cols = slice(b * DEINT_BLOCK, (b + 1) * DEINT_BLOCK)
        o_ref[:, cols] = jnp.dot(w_ref[:, cols].astype(BF16), p_ref[...], preferred_element_type=F32).astype(o_ref.dtype)


def _deinterleave_perm():
    src = lax.broadcasted_iota(jnp.int32, (DEINT_BLOCK, DEINT_BLOCK), 0)
    dst = lax.broadcasted_iota(jnp.int32, (DEINT_BLOCK, DEINT_BLOCK), 1)
    half = DEINT_BLOCK // 2
    return jnp.where(src == jnp.where(dst < half, 2 * dst, 2 * (dst - half) + 1), 1.0, 0.0)


def _deinterleave(w):
    rows, n2 = w.shape
    tm = min(ROW_TILE, rows)
    return pl.pallas_call(
        _deinterleave_body,
        grid=(rows // tm,),
        in_specs=[pl.BlockSpec((tm, n2), lambda i: (i, 0)),
                  pl.BlockSpec((DEINT_BLOCK, DEINT_BLOCK), lambda i: (0, 0))],
        out_specs=pl.BlockSpec((tm, n2), lambda i: (i, 0)),
        out_shape=jax.ShapeDtypeStruct((rows, n2), BF16),
        compiler_params=_cparams("parallel"),
        name="deinterleave",
    )(w, _deinterleave_perm().astype(BF16))


def _combine_body(ya_ref, yb_ref, route_ref, x_ref, lng_ref, lnb_ref, xo_ref, xb_ref):
    x = x_ref[...]
    route = route_ref[...]
    half = D_MODEL // 2
    q = D_MODEL // 4
    ffn_a = jnp.zeros((x.shape[0], half), F32)
    ffn_b = jnp.zeros((x.shape[0], half), F32)
    for k in range(TOP_K):
        wk = route[:, k:k + 1]
        a0, a1 = _unpack_halves(ya_ref[k])
        b0, b1 = _unpack_halves(yb_ref[k])
        ffn_a = ffn_a + wk * jnp.concatenate([a0.astype(F32), a1.astype(F32)], axis=-1)
        ffn_b = ffn_b + wk * jnp.concatenate([b0.astype(F32), b1.astype(F32)], axis=-1)
    del q
    ffn = jnp.concatenate([ffn_a, ffn_b], axis=-1)
    xn = _layer_norm(DEEPNORM_ALPHA * x + ffn, lng_ref[...], lnb_ref[...])
    xo_ref[...] = xn
    xb_ref[...] = xn.astype(BF16)


def _combine(ya, yb, route, x, lng, lnb):
    t = x.shape[0]
    tm = min(ROW_TILE, t)
    tok = lambda width: pl.BlockSpec((tm, width), lambda i: (i, 0))
    gathered = pl.BlockSpec((TOP_K, tm, SC_WORDS), lambda i: (0, i, 0))
    full = lambda shape: pl.BlockSpec(shape, lambda i: (0,) * len(shape))
    return pl.pallas_call(
        _combine_body,
        grid=(t // tm,),
        in_specs=[gathered, gathered, tok(LANES), tok(D_MODEL), full((1, D_MODEL)), full((1, D_MODEL))],
        out_specs=[tok(D_MODEL), tok(D_MODEL)],
        out_shape=[jax.ShapeDtypeStruct((t, D_MODEL), F32), jax.ShapeDtypeStruct((t, D_MODEL), BF16)],
        compiler_params=_cparams("parallel"),
        name="combine_ln",
    )(ya, yb, route, x, lng, lnb)


def _route_positions(route, counts):
    t = route.shape[1]
    nt = counts.shape[0]
    tm = t // nt
    ids = route[TOP_K:2 * TOP_K].astype(jnp.int32).reshape(TOP_K, nt, tm)
    rank = route[2 * TOP_K:3 * TOP_K].astype(jnp.int32).reshape(TOP_K, nt, tm)
    cnt = counts[:, :, 0].astype(jnp.int32)
    before = jnp.cumsum(cnt, axis=0) - cnt
    total = jnp.sum(cnt, axis=0)
    padded = ((total + EXPERT_TILE - 1) // EXPERT_TILE) * EXPERT_TILE
    ends = jnp.cumsum(padded)
    offs = (ends - padded)[None, :] + before
    experts = jnp.arange(N_EXPERTS, dtype=jnp.int32)
    pos = rank + jnp.sum(jnp.where(ids[..., None] == experts, offs[None, :, None, :], 0), axis=-1)
    n_rows = TOP_K * t + N_EXPERTS * EXPERT_TILE
    tile_start = jnp.arange(n_rows // EXPERT_TILE, dtype=jnp.int32) * EXPERT_TILE
    tile_expert = jnp.minimum(jnp.sum((tile_start[:, None] >= ends[None, :]).astype(jnp.int32), axis=1), N_EXPERTS - 1)
    tiles_used = ends[N_EXPERTS - 1:] // EXPERT_TILE
    return pos.reshape(TOP_K, t), jnp.concatenate([tile_expert, tiles_used]), n_rows


def _rope_tables(seq):
    half = DIFF_DQK // 2
    inv_freq = ROPE_THETA ** (-jnp.arange(half, dtype=F32) / half)
    ang = jnp.arange(seq, dtype=F32)[:, None] * inv_freq[None, :]
    cos = jnp.cos(ang)
    sin = jnp.sin(ang)
    reps = DIFF_DV // DIFF_DQK
    cos_t = jnp.tile(jnp.concatenate([cos, cos], axis=-1), (1, reps))
    sin_t = jnp.tile(jnp.concatenate([-sin, sin], axis=-1), (1, reps))
    return cos_t, sin_t


def _lane_row(vec, offset):
    return jnp.zeros((LANES,), F32).at[offset:offset + vec.shape[0]].set(vec.astype(F32))


def _layer_weights(l, p):
    w_in = p['w_in'][l]
    w_big = jnp.concatenate([w_in[:, 5648:9744], w_in[:, 0:2048], w_in[:, 2056:5128], w_in[:, 5136:5648]],
                            axis=1).astype(BF16)
    w_small = jnp.concatenate([w_in[:, 2048:2056], w_in[:, 5128:5136],
                               jnp.zeros((D_MODEL, LANES - 16), F32)], axis=1).astype(BF16)
    par = jnp.zeros((SUBLANES, LANES), F32)
    gdn_par = par.at[0].set(_lane_row(p['gdn_a_log'][l], GDN_HEADS)).at[1].set(_lane_row(p['gdn_dt_bias'][l], GDN_HEADS))
    ssm_par = par.at[0].set(_lane_row(p['ssm_a_log'][l], SSM_HEADS)).at[1].set(_lane_row(p['ssm_dt_bias'][l], SSM_HEADS))
    rw = jnp.concatenate([p['router_w'][l].T, jnp.zeros((LANES - N_EXPERTS, D_MODEL), F32)], axis=0)
    rwh = rw.astype(BF16)
    bgu = p['b_gate_up'][l]
    return dict(
        w_big=w_big, w_small=w_small, gdn_par=gdn_par, ssm_par=ssm_par,
        gdn_conv_w=p['gdn_conv_w'][l], gdn_norm_w=p['gdn_norm_w'][l][None, :],
        lambda_init=0.8 - 0.6 * math.exp(-0.3 * l), diff_lambda=p['diff_lambda'][l],
        diff_norm_w=p['diff_norm_w'][l][None, :],
        ssm_conv_w=p['ssm_conv_w'][l], ssm_conv_b=p['ssm_conv_b'][l][None, :],
        dskip=jnp.repeat(p['ssm_d'][l], SSM_HEADDIM)[None, :], ssm_norm_w=p['ssm_norm_w'][l][None, :],
        w_mem=p['w_mem'][l].astype(BF16), w_branch=p['w_branch'][l].astype(BF16), w_out=p['w_out'][l].astype(BF16),
        ln1_g=p['ln1_g'][l][None, :], ln1_b=p['ln1_b'][l][None, :],
        rwh=rwh, rwl=(rw - rwh.astype(F32)).astype(BF16), rb=_lane_row(p['router_b'][l], 0)[:, None],
        first_expert=l * bgu.shape[0],
        ln2_g=p['ln2_g'][l][None, :], ln2_b=p['ln2_b'][l][None, :])


def _mix_and_route(w, x, xbf, kv_src, bsz, seq, cos, sin):
    t = bsz * seq
    h, hs = _in_proj(xbf, w['w_big'], w['w_small'], min(1024, t), H_WIDTH // 4)
    y_gdn = _gdn(h, hs, w['gdn_conv_w'], w['gdn_par'], w['gdn_norm_w'], bsz, seq)
    y_diff = _diff_attention(h, cos, sin, w['diff_lambda'], w['diff_norm_w'], w['lambda_init'], bsz, seq)
    y_ssm = _ssd(h, hs, w['ssm_conv_w'], w['ssm_conv_b'], w['ssm_par'], w['dskip'], w['ssm_norm_w'], bsz, seq)
    kv = _matmul(kv_src, w['w_mem'], BF16, min(1024, kv_src.shape[0]), HB, "mem_kv")
    y_mem = _mem_attention(h, kv, bsz, seq, kv_src.shape[0] // bsz)
    x1, xa, xb, route, counts = _merge(h, (y_gdn, y_diff, y_ssm, y_mem), x, w['w_branch'], w['w_out'],
                                       w['ln1_g'], w['ln1_b'], w['rwh'], w['rwl'], w['rb'])
    pos, tile_expert, n_rows = _route_positions(route, counts)
    xsa, xsb = _sc_dispatch(xa, xb, pos, n_rows)
    cw = jnp.zeros((t, LANES), F32).at[:, :TOP_K].set(route[:TOP_K].T)
    return dict(x1=x1, route=cw, pos=pos, tile_expert=tile_expert, xsa=xsa, xsb=xsb)


def _experts(w, ew, st):
    ysa, ysb = _expert_mlp(st['tile_expert'], st['xsa'], st['xsb'], ew['w_gate_lin'], ew['b_gate_lin'], ew['w_down'],
                           ew['b_down'], w['first_expert'])
    t = st['x1'].shape[0]
    ga, gb = _sc_collect(ysa, ysb, st['pos'].reshape(1, TOP_K * t))
    return dict(st, ga=ga.reshape(TOP_K, t, SC_WORDS), gb=gb.reshape(TOP_K, t, SC_WORDS))


def _finish(w, st):
    return _combine(st['ga'], st['gb'], st['route'], st['x1'], w['ln2_g'], w['ln2_b'])


def kernel(x, mem, w_in, gdn_conv_w, gdn_a_log, gdn_dt_bias, gdn_norm_w, diff_lambda, diff_norm_w, ssm_conv_w, ssm_conv_b, ssm_a_log, ssm_dt_bias, ssm_d, ssm_norm_w, w_mem, w_branch, w_out, ln1_g, ln1_b, router_w, router_b, w_gate_up, b_gate_up, w_down, b_down, ln2_g, ln2_b):
    p = dict(w_in=w_in, gdn_conv_w=gdn_conv_w, gdn_a_log=gdn_a_log, gdn_dt_bias=gdn_dt_bias, gdn_norm_w=gdn_norm_w,
             diff_lambda=diff_lambda, diff_norm_w=diff_norm_w, ssm_conv_w=ssm_conv_w, ssm_conv_b=ssm_conv_b,
             ssm_a_log=ssm_a_log, ssm_dt_bias=ssm_dt_bias, ssm_d=ssm_d, ssm_norm_w=ssm_norm_w, w_mem=w_mem,
             w_branch=w_branch, w_out=w_out, ln1_g=ln1_g, ln1_b=ln1_b, router_w=router_w, router_b=router_b,
             w_gate_up=w_gate_up, b_gate_up=b_gate_up, w_down=w_down, b_down=b_down, ln2_g=ln2_g, ln2_b=ln2_b)
    bsz, seq, d = x.shape
    depth, n_exp = w_gate_up.shape[0], w_gate_up.shape[1]
    expert_w = dict(
        w_gate_lin=_deinterleave(w_gate_up.reshape(depth * n_exp * d, 2 * D_FF)).reshape(depth * n_exp, d, 2 * D_FF),
        b_gate_lin=b_gate_up.reshape(depth * n_exp, 2 * D_FF // DEINT_BLOCK, DEINT_BLOCK // 2, 2)
        .transpose(0, 1, 3, 2).reshape(depth * n_exp, 1, 2 * D_FF),
        w_down=w_down.astype(BF16).reshape(depth * n_exp, D_FF, d),
        b_down=b_down.reshape(depth * n_exp, 1, d))
    cos, sin = _rope_tables(seq)
    groups = TOKEN_GROUPS if bsz % TOKEN_GROUPS == 0 else 1
    gb = bsz // groups
    xs = [x[g * gb:(g + 1) * gb].reshape(gb * seq, d) for g in range(groups)]
    xbfs = [v.astype(BF16) for v in xs]
    kvs = [mem[g * gb:(g + 1) * gb].reshape(gb * mem.shape[1], d).astype(BF16) for g in range(groups)]
    for l in range(depth):
        w = _layer_weights(l, p)
        sts = [_mix_and_route(w, xs[g], xbfs[g], kvs[g], gb, seq, cos, sin) for g in range(groups)]
        sts = [_experts(w, expert_w, st) for st in sts]
        outs = [_finish(w, st) for st in sts]
        xs = [o[0] for o in outs]
        xbfs = [o[1] for o in outs]
    return jnp.concatenate([v.reshape(gb, seq, d) for v in xs], axis=0)
```
